```python
import jax, jax.numpy as jnp
from jax import lax
import numpy as np

D_MODEL = 2048
BATCH = 4
SEQ = 2048
DEPTH = 4
DEC_BATCH = 128
DEC_SEQ = 4
PAST_LEN = 16384
PAGE_SIZE = 128

N_BRANCH = 4
W_A = D_MODEL // 4
W_B = D_MODEL // 4
W_C = D_MODEL // 4
W_D = D_MODEL // 4
CONV_A = 3
CONV_B = 31
CHUNK = 128
C_GROUPS = 4
POOL_WINDOWS = (2, 4, 8, 16)
POOL_GROUPS = len(POOL_WINDOWS)
POOL_GW = W_D // POOL_GROUPS
POOL_BUF = max(POOL_WINDOWS) - 1
IN_COLS = 3 * W_A + 2 * W_B + 2 * W_C + W_D + N_BRANCH * D_MODEL
D_FF = 5632
N_EXPERTS = 8
TOP_K = 2
D_FF_EXPERT = 2816
N_DENSE = (DEPTH + 1) // 2
N_MOE = DEPTH // 2
EPS = 1e-6

kernel_name = "parallel_conv_pool_chunkmlp_moe_decoder_step"


def _rmsnorm(x, g):
    xf = x.astype(jnp.float32)
    y = xf * lax.rsqrt(jnp.mean(xf * xf, axis=-1, keepdims=True) + EPS)
    return (y * g.astype(jnp.float32)).astype(x.dtype)


def _layernorm(x, g, b):
    xf = x.astype(jnp.float32)
    mu = jnp.mean(xf, axis=-1, keepdims=True)
    var = jnp.mean(jnp.square(xf - mu), axis=-1, keepdims=True)
    y = (xf - mu) * lax.rsqrt(var + EPS)
    return (y * g.astype(jnp.float32) + b.astype(jnp.float32)).astype(x.dtype)


def _causal_dwconv(x, buf, w):
    k = w.shape[0]
    xp = jnp.concatenate([buf.astype(x.dtype), x], axis=1)
    y = lax.conv_general_dilated(xp, w[:, None, :].astype(x.dtype), window_strides=(1,), padding='VALID',
                                 dimension_numbers=('NWC', 'WIO', 'NWC'), feature_group_count=x.shape[-1])
    return y, xp[:, xp.shape[1] - (k - 1):]


def _chunk_spatial_gate(u, v, ws, bs):
    n, l, c = v.shape
    lc = min(l, CHUNK)
    nc = l // lc
    gw = c // C_GROUPS
    mask = jnp.tril(jnp.ones((lc, lc), dtype=bool))
    wm = jnp.where(mask[None], ws[:, :lc, :lc], jnp.zeros((), ws.dtype)).astype(v.dtype)
    vr = v.reshape(n, nc, lc, C_GROUPS, gw)
    s = jnp.einsum('gts,ncsgd->nctgd', wm, vr) + bs[:, :lc].T.astype(v.dtype)[:, :, None]
    return u * s.reshape(n, l, c)


def _multiscale_pool(p, buf, start_pos):
    n, l, c = p.shape
    nb = buf.shape[1]
    pp = jnp.concatenate([buf.astype(p.dtype), p], axis=1)
    s = jnp.concatenate([jnp.zeros((n, 1, c), jnp.float32), lax.cumsum(pp.astype(jnp.float32), axis=1)], axis=1)
    pos = start_pos + jnp.arange(l)
    outs = []
    for gi, w in enumerate(POOL_WINDOWS):
        sl = slice(gi * POOL_GW, (gi + 1) * POOL_GW)
        hi = s[:, nb + 1: nb + 1 + l, sl]
        lo = s[:, nb + 1 - w: nb + 1 - w + l, sl]
        cnt = jnp.minimum(pos + 1, w).astype(jnp.float32)[None, :, None]
        outs.append((hi - lo) / cnt)
    mean = jnp.concatenate(outs, axis=-1)
    d = (mean - p.astype(jnp.float32)).astype(p.dtype)
    return d, pp[:, pp.shape[1] - nb:]


def _swiglu(h, w1, w3, w2):
    return (jax.nn.silu(h @ w1) * (h @ w3)) @ w2


def _moe(h, router_w, w1, w3, w2):
    logits = (h @ router_w).astype(jnp.float32)
    probs = jax.nn.softmax(logits, axis=-1)
    topv, topi = lax.top_k(probs, TOP_K)
    topv = topv / jnp.sum(topv, axis=-1, keepdims=True)
    comb = jnp.sum(jax.nn.one_hot(topi, N_EXPERTS, dtype=jnp.float32) * topv[..., None], axis=-2).astype(h.dtype)
    out = jnp.zeros_like(h)
    for e in range(N_EXPERTS):
        out = out + comb[..., e:e + 1] * _swiglu(h, w1[e], w3[e], w2[e])
    return out


def _mixer(h, buf_a, buf_b, buf_d, start_pos, w_in, conv_a_w, w_br_a, conv_b_w, conv_b_b, ln_b_g, ln_b_b,
           w_br_b, ln_c_g, ln_c_b, sp_w, sp_b, w_br_c, pool_w, pool_scale, w_br_d, w_out):
    n, l, _ = h.shape
    z = h @ w_in
    cuts = np.cumsum([W_A, W_A, W_A, W_B, W_B, W_C, W_C, W_D]).tolist()
    c_a, x_a, b_a, val_b, gate_b, u_c, v_c, p_d, zg = jnp.split(z, cuts, axis=-1)
    conv_a, nbuf_a = _causal_dwconv(c_a * x_a, buf_a, conv_a_w)
    br_a = (b_a * conv_a) @ w_br_a
    glu = val_b * jax.nn.sigmoid(gate_b)
    conv_b, nbuf_b = _causal_dwconv(glu, buf_b, conv_b_w)
    yb = jax.nn.silu(_layernorm(conv_b + conv_b_b, ln_b_g, ln_b_b))
    br_b = yb @ w_br_b
    v_n = _layernorm(v_c, ln_c_g, ln_c_b)
    br_c = _chunk_spatial_gate(u_c, v_n, sp_w, sp_b) @ w_br_c
    dpool, nbuf_d = _multiscale_pool(p_d, buf_d, start_pos)
    yd = jnp.einsum('nlgc,gcd->nlgd', dpool.reshape(n, l, POOL_GROUPS, POOL_GW), pool_w).reshape(n, l, W_D) * pool_scale
    br_d = yd @ w_br_d
    g = jax.nn.sigmoid(zg.astype(jnp.float32)).astype(h.dtype).reshape(n, l, N_BRANCH, D_MODEL)
    merged = g[:, :, 0] * br_a + g[:, :, 1] * br_b + g[:, :, 2] * br_c + g[:, :, 3] * br_d
    return merged @ w_out, nbuf_a, nbuf_b, nbuf_d, v_n


def _trunk(x, bufs_a, bufs_b, bufs_d, start_pos, return_chunk_v, norm_mix_g, w_in, conv_a_w, w_br_a, conv_b_w,
           conv_b_b, ln_b_g, ln_b_b, w_br_b, ln_c_g, ln_c_b, sp_w, sp_b, w_br_c, pool_w, pool_scale, w_br_d,
           w_out, norm_ffn_g, ffn_w1, ffn_w3, ffn_w2, router_w, moe_w1, moe_w3, moe_w2, final_norm_g):
    new_a, new_b, new_d, new_v = [], [], [], []
    for i in range(DEPTH):
        h = _rmsnorm(x, norm_mix_g[i])
        mix, na, nb, nd, vn = _mixer(h, bufs_a[i], bufs_b[i], bufs_d[i], start_pos, w_in[i], conv_a_w[i],
                                     w_br_a[i], conv_b_w[i], conv_b_b[i], ln_b_g[i], ln_b_b[i], w_br_b[i],
                                     ln_c_g[i], ln_c_b[i], sp_w[i], sp_b[i], w_br_c[i], pool_w[i],
                                     pool_scale[i], w_br_d[i], w_out[i])
        x = x + mix
        h = _rmsnorm(x, norm_ffn_g[i])
        j = i // 2
        if i % 2 == 0:
            x = x + _swiglu(h, ffn_w1[j], ffn_w3[j], ffn_w2[j])
        else:
            x = x + _moe(h, router_w[j], moe_w1[j], moe_w3[j], moe_w2[j])
        new_a.append(na)
        new_b.append(nb)
        new_d.append(nd)
        if return_chunk_v:
            new_v.append(vn)
    y = _rmsnorm(x, final_norm_g)
    v_out = jnp.stack(new_v) if return_chunk_v else None
    return y, jnp.stack(new_a), jnp.stack(new_b), jnp.stack(new_d), v_out


def setup_inputs(seed: int = 0) -> dict:
    key = jax.random.key(seed)
    ks = iter(jax.random.split(key, 40))
    f32 = jnp.float32

    def nrm(shape, fan_in):
        return jax.random.normal(next(ks), shape, f32) * (fan_in ** -0.5)

    def gain(shape):
        return 1.0 + 0.02 * jax.random.normal(next(ks), shape, f32)

    def small(shape):
        return 0.02 * jax.random.normal(next(ks), shape, f32)

    return {
        "x_prompt": jax.random.normal(next(ks), (BATCH, SEQ, D_MODEL), f32),
        "x_sample": jax.random.normal(next(ks), (DEC_BATCH, DEC_SEQ, D_MODEL), f32),
        "state_conv_a": jax.random.normal(next(ks), (DEPTH, DEC_BATCH, CONV_A - 1, W_A), f32),
        "state_conv_b": 0.5 * jax.random.normal(next(ks), (DEPTH, DEC_BATCH, CONV_B - 1, W_B), f32),
        "state_pool": jax.random.normal(next(ks), (DEPTH, DEC_BATCH, POOL_BUF, W_D), f32),
        "norm_mix_g": gain((DEPTH, D_MODEL)),
        "w_in": nrm((DEPTH, D_MODEL, IN_COLS), D_MODEL),
        "conv_a_w": nrm((DEPTH, CONV_A, W_A), CONV_A),
        "w_br_a": nrm((DEPTH, W_A, D_MODEL), W_A),
        "conv_b_w": nrm((DEPTH, CONV_B, W_B), CONV_B),
        "conv_b_b": small((DEPTH, W_B)),
        "ln_b_g": gain((DEPTH, W_B)),
        "ln_b_b": small((DEPTH, W_B)),
        "w_br_b": nrm((DEPTH, W_B, D_MODEL), W_B),
        "ln_c_g": gain((DEPTH, W_C)),
        "ln_c_b": small((DEPTH, W_C)),
        "sp_w": nrm((DEPTH, C_GROUPS, CHUNK, CHUNK), CHUNK),
        "sp_b": gain((DEPTH, C_GROUPS, CHUNK)),
        "w_br_c": nrm((DEPTH, W_C, D_MODEL), W_C),
        "pool_w": nrm((DEPTH, POOL_GROUPS, POOL_GW, POOL_GW), POOL_GW),
        "pool_scale": gain((DEPTH, W_D)),
        "w_br_d": nrm((DEPTH, W_D, D_MODEL), W_D),
        "w_out": nrm((DEPTH, D_MODEL, D_MODEL), D_MODEL),
        "norm_ffn_g": gain((DEPTH, D_MODEL)),
        "ffn_w1": nrm((N_DENSE, D_MODEL, D_FF), D_MODEL),
        "ffn_w3": nrm((N_DENSE, D_MODEL, D_FF), D_MODEL),
        "ffn_w2": nrm((N_DENSE, D_FF, D_MODEL), D_FF),
        "router_w": nrm((N_MOE, D_MODEL, N_EXPERTS), D_MODEL),
        "moe_w1": nrm((N_MOE, N_EXPERTS, D_MODEL, D_FF_EXPERT), D_MODEL),
        "moe_w3": nrm((N_MOE, N_EXPERTS, D_MODEL, D_FF_EXPERT), D_MODEL),
        "moe_w2": nrm((N_MOE, N_EXPERTS, D_FF_EXPERT, D_MODEL), D_FF_EXPERT),
        "final_norm_g": gain((D_MODEL,)),
    }


def reference(x_prompt, x_sample, state_conv_a, state_conv_b, state_pool, norm_mix_g, w_in, conv_a_w, w_br_a,
              conv_b_w, conv_b_b, ln_b_g, ln_b_b, w_br_b, ln_c_g, ln_c_b, sp_w, sp_b, w_br_c, pool_w,
              pool_scale, w_br_d, w_out, norm_ffn_g, ffn_w1, ffn_w3, ffn_w2, router_w, moe_w1, moe_w3, moe_w2,
              final_norm_g):
    params = (norm_mix_g, w_in, conv_a_w, w_br_a, conv_b_w, conv_b_b, ln_b_g, ln_b_b, w_br_b, ln_c_g, ln_c_b,
              sp_w, sp_b, w_br_c, pool_w, pool_scale, w_br_d, w_out, norm_ffn_g, ffn_w1, ffn_w3, ffn_w2,
              router_w, moe_w1, moe_w3, moe_w2, final_norm_g)
    nbp = x_prompt.shape[0]
    dt = x_prompt.dtype
    zero_a = jnp.zeros((DEPTH, nbp, CONV_A - 1, W_A), dt)
    zero_b = jnp.zeros((DEPTH, nbp, CONV_B - 1, W_B), dt)
    zero_d = jnp.zeros((DEPTH, nbp, POOL_BUF, W_D), dt)
    y_prompt, a_p, b_p, d_p, _ = _trunk(x_prompt, zero_a, zero_b, zero_d, 0, False, *params)
    y_sample, a_s, b_s, d_s, v_s = _trunk(x_sample, state_conv_a, state_conv_b, state_pool, PAST_LEN, True, *params)
    return (y_prompt, y_sample, a_p, b_p, d_p, a_s, b_s, d_s, v_s)
```

```python
import functools

import jax
import jax.numpy as jnp
from jax import lax
from jax.experimental import pallas as pl
from jax.experimental.pallas import tpu as pltpu

F32 = jnp.float32
BF16 = jnp.bfloat16
I32 = jnp.int32

EPS = 1e-6
PAST_LEN = 16384
POOL_WINDOWS = (2, 4, 8, 16)
TOP_K = 2
LANES = 128
HIST = 32
VMEM_LIMIT = 56 * 1024 * 1024


def _cparams(sem, vmem=VMEM_LIMIT):
    return pltpu.CompilerParams(dimension_semantics=sem, vmem_limit_bytes=vmem)


def _rms(x, g):
    return x * lax.rsqrt(jnp.mean(x * x, axis=-1, keepdims=True) + EPS) * g


def _layernorm(x, g, b):
    mu = jnp.mean(x, axis=-1, keepdims=True)
    xc = x - mu
    var = jnp.mean(xc * xc, axis=-1, keepdims=True)
    return xc * lax.rsqrt(var + EPS) * g + b


def _silu(x):
    return x * jax.nn.sigmoid(x)


def _rmsnorm_body(x_ref, g_ref, o_ref):
    o_ref[...] = _rms(x_ref[...], g_ref[...]).astype(o_ref.dtype)


def _rmsnorm(x, g, out_dtype, tm):
    m, d = x.shape
    return pl.pallas_call(
        _rmsnorm_body,
        grid=(m // tm,),
        in_specs=[pl.BlockSpec((tm, d), lambda i: (i, 0)),
                  pl.BlockSpec((1, d), lambda i: (0, 0))],
        out_specs=pl.BlockSpec((tm, d), lambda i: (i, 0)),
        out_shape=jax.ShapeDtypeStruct((m, d), out_dtype),
        compiler_params=_cparams(("arbitrary",)),
        name="rmsnorm",
    )(x, g.reshape(1, d))


CAST_ROWS = 256


def _mm_body(grp_ref, first_ref, nv_ref, *refs, n_w, mode):
    del grp_ref
    x_ref = refs[0]
    w_refs = refs[1:1 + n_w]
    pos = 1 + n_w
    res_ref = None
    if mode == "resid":
        res_ref = refs[pos]
        pos += 1
    o_ref = refs[pos]
    wb_refs = refs[pos + 1:pos + 1 + n_w]
    m = pl.program_id(1)
    valid = m < nv_ref[0]

    @pl.when(jnp.logical_and(valid, first_ref[m] == 1))
    def _cast():
        for w_ref, wb_ref in zip(w_refs, wb_refs):
            def body(c, carry, w_ref=w_ref, wb_ref=wb_ref):
                r = pl.multiple_of(c * CAST_ROWS, CAST_ROWS)
                wb_ref[pl.ds(r, CAST_ROWS), :] = w_ref[pl.ds(r, CAST_ROWS), :].astype(BF16)
                return carry
            lax.fori_loop(0, w_ref.shape[0] // CAST_ROWS, body, 0)

    @pl.when(valid)
    def _compute():
        x = x_ref[...]
        if mode == "gated":
            a = jnp.dot(x, wb_refs[0][...], preferred_element_type=F32)
            b = jnp.dot(x, wb_refs[1][...], preferred_element_type=F32)
            o_ref[...] = (_silu(a) * b).astype(o_ref.dtype)
        else:
            acc = jnp.dot(x, wb_refs[0][...], preferred_element_type=F32)
            if mode == "resid":
                acc = res_ref[...] + acc
            o_ref[...] = acc.astype(o_ref.dtype)

    @pl.when(jnp.logical_not(valid))
    def _zero():
        o_ref[...] = jnp.zeros_like(o_ref)


def _grouped_matmul(x, ws, tile_grp, tile_first, n_valid, *, tm, tn, mode, out_dtype, res=None, name):
    m, k = x.shape
    n = ws[0].shape[2]
    assert m % tm == 0 and n % tn == 0 and k % CAST_ROWS == 0
    n_w = len(ws)

    def row_of(mi, nv):
        return jnp.minimum(mi, nv[0] - 1)

    in_specs = [pl.BlockSpec((tm, k), lambda ni, mi, g, f, nv: (row_of(mi, nv), 0))]
    for _ in ws:
        in_specs.append(pl.BlockSpec((None, k, tn), lambda ni, mi, g, f, nv: (g[row_of(mi, nv)], 0, ni)))
    args = [x] + list(ws)
    if mode == "resid":
        in_specs.append(pl.BlockSpec((tm, tn), lambda ni, mi, g, f, nv: (row_of(mi, nv), ni)))
        args.append(res)
    grid_spec = pltpu.PrefetchScalarGridSpec(
        num_scalar_prefetch=3,
        grid=(n // tn, m // tm),
        in_specs=in_specs,
        out_specs=pl.BlockSpec((tm, tn), lambda ni, mi, g, f, nv: (mi, ni)),
        scratch_shapes=[pltpu.VMEM((k, tn), BF16) for _ in ws],
    )
    return pl.pallas_call(
        functools.partial(_mm_body, n_w=n_w, mode=mode),
        grid_spec=grid_spec,
        out_shape=jax.ShapeDtypeStruct((m, n), out_dtype),
        compiler_params=_cparams(("arbitrary", "arbitrary")),
        name=name,
    )(tile_grp, tile_first, n_valid, *args)


def _dense_matmul(x, w, layer, *, tm, tn, mode, out_dtype, res=None, w2=None, name):
    tiles = x.shape[0] // tm
    grp = jnp.full((tiles,), layer, I32)
    first = jnp.zeros((tiles,), I32).at[0].set(1)
    nv = jnp.full((1,), tiles, I32)
    ws = [w] if w2 is None else [w, w2]
    return _grouped_matmul(x, ws, grp, first, nv, tm=tm, tn=tn, mode=mode, out_dtype=out_dtype,
                           res=res, name=name)


def _mix_prompt_body(z_ref, caw_ref, cbw_ref, cbb_ref, lbg_ref, lbb_ref, lcg_ref, lcb_ref, spw_ref,
                     spbt_ref, pw_ref, ps_ref, y_ref, na_ref, nb_ref, nd_ref,
                     qa_ext, gb_ext, pd_ext, vn_buf, wm_ref, *, rows, width, ka, kb, nbuf_d):
    i = pl.program_id(1)
    last = pl.num_programs(1) - 1
    w = width
    gw = w // len(POOL_WINDOWS)
    chunk = wm_ref.shape[1]
    cg = wm_ref.shape[0]
    cgw = w // cg
    sub = 32

    @pl.when(i == 0)
    def _init():
        zeros = jnp.zeros((HIST, w), F32)
        qa_ext[0:HIST, :] = zeros
        gb_ext[0:HIST, :] = zeros
        pd_ext[0:HIST, :] = zeros
        tri = lax.broadcasted_iota(I32, (chunk, chunk), 0) >= lax.broadcasted_iota(I32, (chunk, chunk), 1)
        for g in range(cg):
            wm_ref[g] = jnp.where(tri, spw_ref[g], 0.0).astype(BF16)

    def col(k):
        return z_ref[:, k * w:(k + 1) * w].astype(F32)

    qa_ext[HIST:HIST + rows, :] = col(0) * col(1)
    gb_ext[HIST:HIST + rows, :] = col(3) * jax.nn.sigmoid(col(4))
    pd_ext[HIST:HIST + rows, :] = col(7)
    vn_buf[...] = _layernorm(col(6), lcg_ref[...], lcb_ref[...]).astype(BF16)

    conv_a = None
    for j in range(ka):
        term = caw_ref[j:j + 1, :] * qa_ext[HIST - (ka - 1) + j:HIST - (ka - 1) + j + rows, :]
        conv_a = term if conv_a is None else conv_a + term
    y_ref[:, 0:w] = (col(2) * conv_a).astype(y_ref.dtype)

    for r0 in range(0, rows, sub):
        acc = None
        for j in range(kb):
            s = HIST + r0 - (kb - 1) + j
            term = cbw_ref[j:j + 1, :] * gb_ext[s:s + sub, :]
            acc = term if acc is None else acc + term
        yb = _silu(_layernorm(acc + cbb_ref[...], lbg_ref[...], lbb_ref[...]))
        y_ref[r0:r0 + sub, w:2 * w] = yb.astype(y_ref.dtype)

    for c0 in range(0, rows, chunk):
        for g in range(cg):
            lanes = slice(g * cgw, (g + 1) * cgw)
            s = jnp.dot(wm_ref[g], vn_buf[c0:c0 + chunk, lanes], preferred_element_type=F32)
            s = s + spbt_ref[:, g:g + 1]
            u = z_ref[c0:c0 + chunk, 5 * w + g * cgw:5 * w + (g + 1) * cgw].astype(F32)
            y_ref[c0:c0 + chunk, 2 * w + g * cgw:2 * w + (g + 1) * cgw] = (u * s).astype(y_ref.dtype)

    pos = i * rows + lax.broadcasted_iota(I32, (rows, 1), 0)
    for gi, win in enumerate(POOL_WINDOWS):
        lanes = slice(gi * gw, (gi + 1) * gw)
        tot = None
        for k in range(win):
            term = pd_ext[HIST - k:HIST - k + rows, lanes]
            tot = term if tot is None else tot + term
        cnt = jnp.minimum(pos + 1, win).astype(F32)
        d = tot / cnt - pd_ext[HIST:HIST + rows, lanes]
        yd = jnp.dot(d.astype(BF16), pw_ref[gi].astype(BF16), preferred_element_type=F32)
        y_ref[:, 3 * w + gi * gw:3 * w + (gi + 1) * gw] = (yd * ps_ref[:, lanes]).astype(y_ref.dtype)

    @pl.when(i == last)
    def _state():
        na_ref[...] = qa_ext[HIST + rows - (ka - 1):HIST + rows, :]
        nb_ref[...] = gb_ext[HIST + rows - (kb - 1):HIST + rows, :]
        nd_ref[...] = pd_ext[HIST + rows - nbuf_d:HIST + rows, :]

    qa_ext[0:HIST, :] = qa_ext[rows:rows + HIST, :]
    gb_ext[0:HIST, :] = gb_ext[rows:rows + HIST, :]
    pd_ext[0:HIST, :] = pd_ext[rows:rows + HIST, :]


def _mix_prompt(z, lw, *, batch, seq, rows):
    w = lw["conv_a_w"].shape[1]
    ka = lw["conv_a_w"].shape[0]
    kb = lw["conv_b_w"].shape[0]
    nbuf_d = max(POOL_WINDOWS) - 1
    cg, chunk = lw["sp_w"].shape[0], lw["sp_w"].shape[1]
    assert seq % rows == 0 and rows % chunk == 0 and kb - 1 <= HIST and nbuf_d <= HIST
    nt = seq // rows
    full2 = lambda a: pl.BlockSpec(a.shape, lambda b, i: (0, 0))
    full3 = lambda a: pl.BlockSpec(a.shape, lambda b, i: (0, 0, 0))
    vec = lambda a: a.reshape(1, w)
    small = [lw["conv_a_w"], lw["conv_b_w"], vec(lw["conv_b_b"]), vec(lw["ln_b_g"]), vec(lw["ln_b_b"]),
             vec(lw["ln_c_g"]), vec(lw["ln_c_b"])]
    spbt = lw["sp_b"].T
    in_specs = ([pl.BlockSpec((rows, 8 * w), lambda b, i: (b * nt + i, 0))]
                + [full2(a) for a in small]
                + [full3(lw["sp_w"]), full2(spbt), full3(lw["pool_w"]), full2(vec(lw["pool_scale"]))])
    out_shapes = (jax.ShapeDtypeStruct((batch * seq, 4 * w), BF16),
                  jax.ShapeDtypeStruct((batch, ka - 1, w), F32),
                  jax.ShapeDtypeStruct((batch, kb - 1, w), F32),
                  jax.ShapeDtypeStruct((batch, nbuf_d, w), F32))
    out_specs = (pl.BlockSpec((rows, 4 * w), lambda b, i: (b * nt + i, 0)),
                 pl.BlockSpec((None, ka - 1, w), lambda b, i: (b, 0, 0)),
                 pl.BlockSpec((None, kb - 1, w), lambda b, i: (b, 0, 0)),
                 pl.BlockSpec((None, nbuf_d, w), lambda b, i: (b, 0, 0)))
    scratch = [pltpu.VMEM((HIST + rows, w), F32), pltpu.VMEM((HIST + rows, w), F32),
               pltpu.VMEM((HIST + rows, w), F32), pltpu.VMEM((rows, w), BF16),
               pltpu.VMEM((cg, chunk, chunk), BF16)]
    return pl.pallas_call(
        functools.partial(_mix_prompt_body, rows=rows, width=w, ka=ka, kb=kb, nbuf_d=nbuf_d),
        grid=(batch, nt),
        in_specs=in_specs,
        out_specs=out_specs,
        out_shape=out_shapes,
        scratch_shapes=scratch,
        compiler_params=_cparams(("arbitrary", "arbitrary")),
        name="mix_prompt",
    )(z, *small, lw["sp_w"], spbt, lw["pool_w"], vec(lw["pool_scale"]))


def _mix_sample_body(spw_ref, spb_ref, z_ref, sa_ref, sb_ref, sd_ref, caw_ref, cbw_ref, cbb_ref,
                     lbg_ref, lbb_ref, lcg_ref, lcb_ref, pw_ref, ps_ref,
                     y_ref, na_ref, nb_ref, nd_ref, cv_ref, *, nseq, steps, width, ka, kb, nbuf_d, cg):
    w = width
    gw = w // len(POOL_WINDOWS)
    cgw = w // cg
    lane = lax.broadcasted_iota(I32, (1, w), 1)

    def col(t, k):
        return z_ref[t * nseq:(t + 1) * nseq, k * w:(k + 1) * w].astype(F32)

    def put(t, k, val):
        y_ref[t * nseq:(t + 1) * nseq, k * w:(k + 1) * w] = val.astype(y_ref.dtype)

    ext = [sa_ref[k] for k in range(ka - 1)] + [col(t, 0) * col(t, 1) for t in range(steps)]
    for t in range(steps):
        conv = None
        for j in range(ka):
            term = caw_ref[j:j + 1, :] * ext[t + j]
            conv = term if conv is None else conv + term
        put(t, 0, col(t, 2) * conv)
    for k in range(ka - 1):
        na_ref[k] = ext[steps + k]

    for t in range(steps):
        nb_ref[kb - 1 - steps + t] = col(t, 3) * jax.nn.sigmoid(col(t, 4))
    for k in range(kb - 1 - steps):
        nb_ref[k] = sb_ref[k + steps]

    def ext_b(k):
        return sb_ref[k] if k < kb - 1 else nb_ref[k - steps]

    for t in range(steps):
        acc = None
        for j in range(kb):
            term = cbw_ref[j:j + 1, :] * ext_b(t + j)
            acc = term if acc is None else acc + term
        put(t, 1, _silu(_layernorm(acc + cbb_ref[...], lbg_ref[...], lbb_ref[...])))

    def lane_groups(vals):
        out = jnp.full((1, w), vals[cg - 1], F32)
        for g in range(cg - 2, -1, -1):
            out = jnp.where(lane < (g + 1) * cgw, vals[g], out)
        return out

    for t in range(steps):
        cv_ref[t] = _layernorm(col(t, 6), lcg_ref[...], lcb_ref[...])
    for t in range(steps):
        s = lane_groups([spb_ref[g, t] for g in range(cg)])
        for sp in range(t + 1):
            s = s + lane_groups([spw_ref[g, t, sp] for g in range(cg)]) * cv_ref[sp]
        put(t, 2, col(t, 5) * s)

    for t in range(steps):
        nd_ref[nbuf_d - steps + t] = col(t, 7)
    for k in range(nbuf_d - steps):
        nd_ref[k] = sd_ref[k + steps]

    def ext_d(k):
        return sd_ref[k] if k < nbuf_d else nd_ref[k - steps]

    for t in range(steps):
        for gi, win in enumerate(POOL_WINDOWS):
            lanes = slice(gi * gw, (gi + 1) * gw)
            tot = None
            for k in range(win):
                term = ext_d(nbuf_d + t - k)[:, lanes]
                tot = term if tot is None else tot + term
            cnt = float(min(PAST_LEN + t + 1, win))
            d = tot / cnt - ext_d(nbuf_d + t)[:, lanes]
            yd = jnp.dot(d.astype(BF16), pw_ref[gi].astype(BF16), preferred_element_type=F32)
            y_ref[t * nseq:(t + 1) * nseq, 3 * w + gi * gw:3 * w + (gi + 1) * gw] = (
                yd * ps_ref[:, lanes]).astype(y_ref.dtype)


def _mix_sample(z, sa, sb, sd, lw, *, nseq, steps, m_prompt):
    w = lw["conv_a_w"].shape[1]
    ka = lw["conv_a_w"].shape[0]
    kb = lw["conv_b_w"].shape[0]
    nbuf_d = max(POOL_WINDOWS) - 1
    cg = lw["sp_w"].shape[0]
    ms = nseq * steps
    assert m_prompt % ms == 0 and steps <= nbuf_d and steps <= kb - 1
    rb = m_prompt // ms
    spw_small = lw["sp_w"][:, :steps, :steps]
    spb_small = lw["sp_b"][:, :steps]
    vec = lambda a: a.reshape(1, w)
    smem = pl.BlockSpec(memory_space=pltpu.SMEM)
    full2 = lambda a: pl.BlockSpec(a.shape, lambda i: (0, 0))
    full3 = lambda a: pl.BlockSpec(a.shape, lambda i: (0, 0, 0))
    small = [lw["conv_a_w"], lw["conv_b_w"], vec(lw["conv_b_b"]), vec(lw["ln_b_g"]), vec(lw["ln_b_b"]),
             vec(lw["ln_c_g"]), vec(lw["ln_c_b"])]
    in_specs = ([smem, smem,
                 pl.BlockSpec((ms, 8 * w), lambda i: (rb, 0)),
                 full3(sa), full3(sb), full3(sd)]
                + [full2(a) for a in small]
                + [full3(lw["pool_w"]), full2(vec(lw["pool_scale"]))])
    out_shapes = (jax.ShapeDtypeStruct((ms, 4 * w), BF16),
                  jax.ShapeDtypeStruct(sa.shape, F32), jax.ShapeDtypeStruct(sb.shape, F32),
                  jax.ShapeDtypeStruct(sd.shape, F32), jax.ShapeDtypeStruct((steps, nseq, w), F32))
    out_specs = (pl.BlockSpec((ms, 4 * w), lambda i: (0, 0)),
                 full3(sa), full3(sb), full3(sd),
                 pl.BlockSpec((steps, nseq, w), lambda i: (0, 0, 0)))
    return pl.pallas_call(
        functools.partial(_mix_sample_body, nseq=nseq, steps=steps, width=w, ka=ka, kb=kb,
                          nbuf_d=nbuf_d, cg=cg),
        grid=(1,),
        in_specs=in_specs,
        out_specs=out_specs,
        out_shape=out_shapes,
        compiler_params=_cparams(("arbitrary",)),
        name="mix_sample",
    )(spw_small, spb_small, z, sa, sb, sd, *small, lw["pool_w"], vec(lw["pool_scale"]))


def _merge_body(yp_ref, ys_ref, g0_ref, g1_ref, g2_ref, g3_ref, wa_ref, wb_ref, wc_ref, wd_ref, o_ref, wbf_ref,
                *, width, prompt_tiles):
    i = pl.program_id(0)

    @pl.when(i == 0)
    def _cast():
        for k, w_ref in enumerate((wa_ref, wb_ref, wc_ref, wd_ref)):
            wbf_ref[k] = w_ref[...].astype(BF16)

    def run(y_ref):
        acc = None
        for k, g_ref in enumerate((g0_ref, g1_ref, g2_ref, g3_ref)):
            br = jnp.dot(y_ref[:, k * width:(k + 1) * width], wbf_ref[k], preferred_element_type=F32)
            term = jax.nn.sigmoid(g_ref[...].astype(F32)) * br
            acc = term if acc is None else acc + term
        o_ref[...] = acc.astype(o_ref.dtype)

    @pl.when(i < prompt_tiles)
    def _prompt():
        run(yp_ref)

    @pl.when(i >= prompt_tiles)
    def _sample():
        run(ys_ref)


def _merge(y_p, y_s, z, w_brs, layer, *, tm):
    mp, yw = y_p.shape
    ms = y_s.shape[0]
    assert mp % tm == 0 and ms % tm == 0
    tp = mp // tm
    w = yw // 4
    d = w_brs[0].shape[2]
    assert (2 * yw) % d == 0
    goff = (2 * yw) // d
    gate_spec = lambda k: pl.BlockSpec((tm, d), lambda i, k=k: (i, goff + k))
    w_spec = pl.BlockSpec((None, w, d), lambda i: (layer, 0, 0), pipeline_mode=pl.Buffered(1))
    return pl.pallas_call(
        functools.partial(_merge_body, width=w, prompt_tiles=tp),
        grid=((mp + ms) // tm,),
        in_specs=[pl.BlockSpec((tm, yw), lambda i: (jnp.minimum(i, tp - 1), 0)),
                  pl.BlockSpec((tm, yw), lambda i: (jnp.maximum(i - tp, 0), 0))]
                 + [gate_spec(k) for k in range(4)] + [w_spec] * 4,
        out_specs=pl.BlockSpec((tm, d), lambda i: (i, 0)),
        out_shape=jax.ShapeDtypeStruct((mp + ms, d), BF16),
        scratch_shapes=[pltpu.VMEM((4, w, d), BF16)],
        compiler_params=_cparams(("arbitrary",)),
        name="merge",
    )(y_p, y_s, z, z, z, z, *w_brs)


def _outproj_body(m_ref, x_ref, w_ref, g_ref, *refs, with_h):
    x1_ref = refs[0]
    wbf_ref = refs[-1]

    @pl.when(pl.program_id(0) == 0)
    def _cast():
        wbf_ref[...] = w_ref[...].astype(BF16)

    x1 = x_ref[...] + jnp.dot(m_ref[...], wbf_ref[...], preferred_element_type=F32)
    x1_ref[...] = x1
    if with_h:
        refs[1][...] = _rms(x1, g_ref[...]).astype(BF16)


def _outproj(merged, x, w_out, layer, g, *, tm, with_h):
    m, d = x.shape
    row = pl.BlockSpec((tm, d), lambda i: (i, 0))
    out_shape = [jax.ShapeDtypeStruct((m, d), F32)]
    out_specs = [row]
    if with_h:
        out_shape.append(jax.ShapeDtypeStruct((m, d), BF16))
        out_specs.append(row)
    return pl.pallas_call(
        functools.partial(_outproj_body, with_h=with_h),
        grid=(m // tm,),
        in_specs=[row, row,
                  pl.BlockSpec((None, d, d), lambda i: (layer, 0, 0), pipeline_mode=pl.Buffered(1)),
                  pl.BlockSpec((1, d), lambda i: (0, 0))],
        out_specs=out_specs,
        out_shape=out_shape,
        scratch_shapes=[pltpu.VMEM((d, d), BF16)],
        compiler_params=_cparams(("arbitrary",)),
        name="outproj",
    )(merged, x, w_out, g.reshape(1, d))


def _route_body(x_ref, g_ref, rwt_ref, e_ref, r_ref, wt_ref, cnt_ref, carry_ref, tri_ref):
    i = pl.program_id(0)
    ne = rwt_ref.shape[0]
    tm = x_ref.shape[0]

    @pl.when(i == 0)
    def _init():
        carry_ref[...] = jnp.zeros_like(carry_ref)
        upper = lax.broadcasted_iota(I32, (tm, tm), 0) <= lax.broadcasted_iota(I32, (tm, tm), 1)
        tri_ref[...] = jnp.where(upper, 1.0, 0.0).astype(BF16)

    h = _rms(x_ref[...], g_ref[...])
    h_hi = h.astype(BF16)
    h_lo = (h - h_hi.astype(F32)).astype(BF16)
    rw = rwt_ref[...]
    rw_hi = rw.astype(BF16)
    rw_lo = (rw - rw_hi.astype(F32)).astype(BF16)
    dn = (((1,), (1,)), ((), ()))
    dg = lambda a, b: lax.dot_general(a, b, dn, preferred_element_type=F32)
    logits = dg(rw_hi, h_hi) + dg(rw_hi, h_lo) + dg(rw_lo, h_hi)
    ex = jnp.exp(logits - jnp.max(logits, axis=0, keepdims=True))
    p = ex / jnp.sum(ex, axis=0, keepdims=True)
    eid = lax.broadcasted_iota(I32, (ne, tm), 0)
    m1 = jnp.max(p, axis=0, keepdims=True)
    i1 = jnp.min(jnp.where(p == m1, eid, ne), axis=0, keepdims=True)
    oh1 = eid == i1
    p2 = jnp.where(oh1, -1.0, p)
    m2 = jnp.max(p2, axis=0, keepdims=True)
    i2 = jnp.min(jnp.where(p2 == m2, eid, ne), axis=0, keepdims=True)
    oh2 = eid == i2
    den = m1 + m2
    sel = jnp.logical_or(oh1, oh2)
    sel_f = jnp.where(sel, 1.0, 0.0)
    cum = jnp.dot(sel_f.astype(BF16), tri_ref[...], preferred_element_type=F32) + carry_ref[...]
    rank1 = jnp.sum(jnp.where(oh1, cum, 0.0), axis=0, keepdims=True) - 1.0
    rank2 = jnp.sum(jnp.where(oh2, cum, 0.0), axis=0, keepdims=True) - 1.0
    carry_ref[...] = carry_ref[...] + jnp.sum(sel_f, axis=1, keepdims=True)
    e_ref[0:1, :] = i1
    e_ref[1:2, :] = i2
    r_ref[0:1, :] = rank1.astype(I32)
    r_ref[1:2, :] = rank2.astype(I32)
    wt_ref[0:1, :] = m1 / den
    wt_ref[1:2, :] = m2 / den

    @pl.when(i == pl.num_programs(0) - 1)
    def _counts():
        cnt_ref[...] = jnp.broadcast_to(carry_ref[...], cnt_ref.shape).astype(I32)


def _route(x1, g, router_w, *, tm):
    m, d = x1.shape
    ne = router_w.shape[1]
    tok = lambda dt: jax.ShapeDtypeStruct((TOP_K, m), dt)
    tok_spec = pl.BlockSpec((TOP_K, tm), lambda i: (0, i))
    return pl.pallas_call(
        _route_body,
        grid=(m // tm,),
        in_specs=[pl.BlockSpec((tm, d), lambda i: (i, 0)),
                  pl.BlockSpec((1, d), lambda i: (0, 0)),
                  pl.BlockSpec((ne, d), lambda i: (0, 0))],
        out_specs=(tok_spec, tok_spec, tok_spec, pl.BlockSpec((ne, LANES), lambda i: (0, 0))),
        out_shape=(tok(I32), tok(I32), tok(F32), jax.ShapeDtypeStruct((ne, LANES), I32)),
        scratch_shapes=[pltpu.VMEM((ne, 1), F32), pltpu.VMEM((tm, tm), BF16)],
        compiler_params=_cparams(("arbitrary",)),
        name="route",
    )(x1, g.reshape(1, d), router_w.T)


def _plan_body(e_ref, r_ref, cnt_ref, pos_ref, tgrp_ref, tfirst_ref, misc_ref, *, tile_shift):
    ne = cnt_ref.shape[0]
    tile = 1 << tile_shift
    sub = lax.broadcasted_iota(I32, (ne, LANES), 0)
    lane = lax.broadcasted_iota(I32, (ne, LANES), 1)

    def as_row(col):
        return jnp.sum(jnp.where(sub == lane, col, 0), axis=0, keepdims=True)

    cnt = cnt_ref[:, 0:1]
    padded = lax.shift_left(lax.shift_right_logical(cnt + (tile - 1), tile_shift), tile_shift)
    gstart = jnp.sum(jnp.where(lane < sub, as_row(padded), 0), axis=1, keepdims=True)
    gend = gstart + padded
    m = e_ref.shape[1]
    eid = lax.broadcasted_iota(I32, (ne, m), 0)
    for k in range(TOP_K):
        base = jnp.sum(jnp.where(eid == e_ref[k:k + 1, :], gstart, 0), axis=0, keepdims=True)
        pos_ref[k:k + 1, :] = base + r_ref[k:k + 1, :]
    start = lax.broadcasted_iota(I32, (1, LANES), 1) * tile
    grp = jnp.sum((gend <= start).astype(I32), axis=0, keepdims=True)
    tgrp_ref[...] = jnp.minimum(grp, ne - 1)
    tfirst_ref[...] = jnp.sum(jnp.logical_and(gstart == start, padded > 0).astype(I32), axis=0, keepdims=True)
    n_tiles = lax.shift_right_logical(jnp.max(gend, axis=0, keepdims=True), tile_shift)
    misc = jnp.where(sub == 0, n_tiles, 0)
    misc = jnp.where(sub == 1, as_row(gstart + cnt), misc)
    misc = jnp.where(sub == 2, as_row(gend), misc)
    misc_ref[...] = misc


def _plan(e, r, cnt, *, tile):
    tile_shift = tile.bit_length() - 1
    assert 1 << tile_shift == tile
    ne = cnt.shape[0]
    m = e.shape[1]
    whole = lambda a: pl.BlockSpec(a.shape, lambda i: (0, 0))
    row = jax.ShapeDtypeStruct((1, LANES), I32)
    out_shape = (jax.ShapeDtypeStruct((TOP_K, m), I32), row, row, jax.ShapeDtypeStruct((ne, LANES), I32))
    return pl.pallas_call(
        functools.partial(_plan_body, tile_shift=tile_shift),
        grid=(1,),
        in_specs=[whole(e), whole(r), whole(cnt)],
        out_specs=tuple(pl.BlockSpec(s.shape, lambda i: (0, 0)) for s in out_shape),
        out_shape=out_shape,
        compiler_params=_cparams(("arbitrary",)),
        name="plan",
    )(e, r, cnt)


def _dispatch_body(pos0_ref, pos1_ref, nv_ref, fill_ref, end_ref, x_hbm, g_ref, o_ref, src_ref, buf_ref, sem_ref):
    j = pl.program_id(0)
    tm = o_ref.shape[0]
    n_tok = pos0_ref.shape[0]
    nv = nv_ref[0]

    def start_tile(tile, slot):
        def body(r, carry):
            tok = src_ref[tile * tm + r]
            pltpu.make_async_copy(x_hbm.at[pl.ds(tok, 1), :], buf_ref.at[slot, pl.ds(r, 1), :],
                                  sem_ref.at[slot]).start()
            return carry
        lax.fori_loop(0, tm, body, 0)

    @pl.when(j == 0)
    def _prologue():
        for g in range(fill_ref.shape[0]):
            def fill(p, carry):
                src_ref[p] = 0
                return carry
            lax.fori_loop(fill_ref[g], end_ref[g], fill, 0)

        def inv(t, carry):
            src_ref[pos0_ref[t]] = t
            src_ref[pos1_ref[t]] = t
            return carry
        lax.fori_loop(0, n_tok, inv, 0)
        start_tile(0, 0)

    slot = lax.rem(j, 2)

    @pl.when(j < nv)
    def _work():
        @pl.when(j + 1 < nv)
        def _prefetch():
            start_tile(j + 1, 1 - slot)

        pltpu.make_async_copy(x_hbm.at[pl.ds(0, tm), :], buf_ref.at[slot], sem_ref.at[slot]).wait()
        o_ref[...] = _rms(buf_ref[slot], g_ref[...]).astype(o_ref.dtype)

    @pl.when(j >= nv)
    def _skip():
        o_ref[...] = jnp.zeros_like(o_ref)


def _dispatch(x1, g, pos0, pos1, n_tiles, fill_row, end_row, *, tm, max_tiles):
    m, d = x1.shape
    grid_spec = pltpu.PrefetchScalarGridSpec(
        num_scalar_prefetch=5,
        grid=(max_tiles,),
        in_specs=[pl.BlockSpec(memory_space=pl.ANY),
                  pl.BlockSpec((1, d), lambda j, *_: (0, 0))],
        out_specs=pl.BlockSpec((tm, d), lambda j, *_: (j, 0)),
        scratch_shapes=[pltpu.SMEM((max_tiles * tm,), I32),
                        pltpu.VMEM((2, tm, d), F32),
                        pltpu.SemaphoreType.DMA((2,))],
    )
    return pl.pallas_call(
        _dispatch_body,
        grid_spec=grid_spec,
        out_shape=jax.ShapeDtypeStruct((max_tiles * tm, d), BF16),
        compiler_params=_cparams(("arbitrary",)),
        name="dispatch",
    )(pos0, pos1, n_tiles, fill_row, end_row, x1, g.reshape(1, d))


def _combine_body(pos0_ref, pos1_ref, ys_hbm, x_ref, wt_ref, g_ref, xn_ref, h_ref, buf_ref, sem_ref):
    i = pl.program_id(0)
    n = pl.num_programs(0)
    tm = x_ref.shape[0]

    def start_tile(tile, slot):
        def body(r, carry):
            t = tile * tm + r
            pltpu.make_async_copy(ys_hbm.at[pl.ds(pos0_ref[t], 1), :], buf_ref.at[slot, 0, pl.ds(r, 1), :],
                                  sem_ref.at[slot, 0]).start()
            pltpu.make_async_copy(ys_hbm.at[pl.ds(pos1_ref[t], 1), :], buf_ref.at[slot, 1, pl.ds(r, 1), :],
                                  sem_ref.at[slot, 1]).start()
            return carry
        lax.fori_loop(0, tm, body, 0)

    @pl.when(i == 0)
    def _prologue():
        start_tile(0, 0)

    slot = lax.rem(i, 2)

    @pl.when(i + 1 < n)
    def _prefetch():
        start_tile(i + 1, 1 - slot)

    for k in range(TOP_K):
        pltpu.make_async_copy(ys_hbm.at[pl.ds(0, tm), :], buf_ref.at[slot, k], sem_ref.at[slot, k]).wait()
    xn = x_ref[...] + wt_ref[:, 0:1] * buf_ref[slot, 0] + wt_ref[:, 1:2] * buf_ref[slot, 1]
    xn_ref[...] = xn
    h_ref[...] = _rms(xn, g_ref[...]).astype(h_ref.dtype)


def _combine(ys, x1, wt_cols, g, pos0, pos1, *, tm, h_dtype):
    m, d = x1.shape
    row = lambda: pl.BlockSpec((tm, d), lambda i, *_: (i, 0))
    grid_spec = pltpu.PrefetchScalarGridSpec(
        num_scalar_prefetch=2,
        grid=(m // tm,),
        in_specs=[pl.BlockSpec(memory_space=pl.ANY), row(),
                  pl.BlockSpec((tm, TOP_K), lambda i, *_: (i, 0)),
                  pl.BlockSpec((1, d), lambda i, *_: (0, 0))],
        out_specs=(row(), row()),
        scratch_shapes=[pltpu.VMEM((2, TOP_K, tm, d), F32), pltpu.SemaphoreType.DMA((2, TOP_K))],
    )
    return pl.pallas_call(
        _combine_body,
        grid_spec=grid_spec,
        out_shape=(jax.ShapeDtypeStruct((m, d), F32), jax.ShapeDtypeStruct((m, d), h_dtype)),
        compiler_params=_cparams(("arbitrary",)),
        name="combine",
    )(pos0, pos1, ys, x1, wt_cols, g.reshape(1, d))


def _pick(total, prefs):
    for p in prefs:
        if total % p == 0:
            return p
    raise ValueError(f"no tile for {total} in {prefs}")


def _row_tile(m, cap):
    best = None
    for t in range(16, cap + 1, 16):
        if m % t == 0:
            best = t
    assert best is not None
    return best


def kernel(x_prompt, x_sample, state_conv_a, state_conv_b, state_pool, norm_mix_g, w_in, conv_a_w, w_br_a,
           conv_b_w, conv_b_b, ln_b_g, ln_b_b, w_br_b, ln_c_g, ln_c_b, sp_w, sp_b, w_br_c, pool_w,
           pool_scale, w_br_d, w_out, norm_ffn_g, ffn_w1, ffn_w3, ffn_w2, router_w, moe_w1, moe_w3, moe_w2,
           final_norm_g):
    batch, seq, d = x_prompt.shape
    nseq, steps, _ = x_sample.shape
    depth = w_in.shape[0]
    n_exp = router_w.shape[2]
    mp, ms = batch * seq, nseq * steps
    m = mp + ms

    x = jnp.concatenate([x_prompt.reshape(mp, d), jnp.swapaxes(x_sample, 0, 1).reshape(ms, d)], axis=0)
    st_a = jnp.swapaxes(state_conv_a, 1, 2)
    st_b = jnp.swapaxes(state_conv_b, 1, 2)
    st_d = jnp.swapaxes(state_pool, 1, 2)

    tm_big = _row_tile(m, 1088)
    tm_mid = _row_tile(m, 544)
    tm_small = _row_tile(m, 256)
    tm_merge = min(_row_tile(mp, 256), _row_tile(ms, 256))
    assert mp % tm_merge == 0 and ms % tm_merge == 0
    tm_moe = 512
    max_tiles = -(-(m * TOP_K) // tm_moe) + n_exp
    assert max_tiles <= LANES
    mix_rows = _pick(seq, (256, 128))
    moe_w1f = moe_w1.reshape((-1,) + moe_w1.shape[2:])
    moe_w3f = moe_w3.reshape((-1,) + moe_w3.shape[2:])
    moe_w2f = moe_w2.reshape((-1,) + moe_w2.shape[2:])
    dffe = moe_w1.shape[3]
    dff = ffn_w1.shape[2]

    h = _rmsnorm(x, norm_mix_g[0], BF16, tm_mid)
    outs_p = ([], [], [])
    outs_s = ([], [], [], [])
    y_final = None
    for i in range(depth):
        lw = dict(conv_a_w=conv_a_w[i], conv_b_w=conv_b_w[i], conv_b_b=conv_b_b[i], ln_b_g=ln_b_g[i],
                  ln_b_b=ln_b_b[i], ln_c_g=ln_c_g[i], ln_c_b=ln_c_b[i], sp_w=sp_w[i], sp_b=sp_b[i],
                  pool_w=pool_w[i], pool_scale=pool_scale[i])
        z = _dense_matmul(h, w_in, i, tm=tm_big, tn=_pick(w_in.shape[2], (1024, 512, 256, 128)),
                          mode="plain", out_dtype=BF16, name="inproj")
        y_p, na_p, nb_p, nd_p = _mix_prompt(z, lw, batch=batch, seq=seq, rows=mix_rows)
        y_s, na_s, nb_s, nd_s, cv_s = _mix_sample(z, st_a[i], st_b[i], st_d[i], lw, nseq=nseq, steps=steps,
                                                  m_prompt=mp)
        for acc, val in zip(outs_p, (na_p, nb_p, nd_p)):
            acc.append(val)
        for acc, val in zip(outs_s, (na_s, nb_s, nd_s, cv_s)):
            acc.append(val)
        merged = _merge(y_p, y_s, z, (w_br_a, w_br_b, w_br_c, w_br_d), i, tm=tm_merge)
        last = i == depth - 1
        g_next = final_norm_g if last else norm_mix_g[i + 1]
        h_dtype = F32 if last else BF16
        j = i // 2
        if i % 2 == 0:
            x1, h2 = _outproj(merged, x, w_out, i, norm_ffn_g[i], tm=tm_small, with_h=True)
            a = _dense_matmul(h2, ffn_w1, j, w2=ffn_w3, tm=tm_big, tn=_pick(dff, (512, 256, 128)),
                              mode="gated", out_dtype=BF16, name="ffn_up")
            x = _dense_matmul(a, ffn_w2, j, tm=tm_mid, tn=_pick(d, (512, 256, 128)), mode="resid",
                              out_dtype=F32, res=x1, name="ffn_down")
            h = _rmsnorm(x, g_next, h_dtype, tm_mid)
        else:
            (x1,) = _outproj(merged, x, w_out, i, norm_ffn_g[i], tm=tm_small, with_h=False)
            e, r, wt, cnt = _route(x1, norm_ffn_g[i], router_w[j], tm=_pick(m, (512, 256, 128)))
            pos, tgrp, tfirst, misc = _plan(e, r, cnt, tile=tm_moe)
            n_tiles = misc[0, 0:1]
            hs = _dispatch(x1, norm_ffn_g[i], pos[0], pos[1], n_tiles, misc[1, :n_exp], misc[2, :n_exp],
                           tm=tm_moe, max_tiles=max_tiles)
            grp = tgrp[0, :max_tiles] + j * n_exp
            first = tfirst[0, :max_tiles]
            a = _grouped_matmul(hs, [moe_w1f, moe_w3f], grp, first, n_tiles, tm=tm_moe,
                                tn=_pick(dffe, (256, 128)), mode="gated", out_dtype=BF16, name="moe_up")
            ys = _grouped_matmul(a, [moe_w2f], grp, first, n_tiles, tm=tm_moe,
                                 tn=_pick(d, (1024, 512, 256, 128)), mode="plain", out_dtype=F32, name="moe_down")
            x, h = _combine(ys, x1, wt.T, g_next, pos[0], pos[1], tm=tm_small, h_dtype=h_dtype)
        if last:
            y_final = h

    y_prompt = y_final[:mp].reshape(batch, seq, d)
    y_sample = jnp.swapaxes(y_final[mp:].reshape(steps, nseq, d), 0, 1)
    stack_s = lambda vals: jnp.swapaxes(jnp.stack(vals), 1, 2)
    return (y_prompt, y_sample, jnp.stack(outs_p[0]), jnp.stack(outs_p[1]), jnp.stack(outs_p[2]),
            stack_s(outs_s[0]), stack_s(outs_s[1]), stack_s(outs_s[2]), stack_s(outs_s[3]))
```

```python
import functools

import jax
import jax.numpy as jnp
from jax import lax
from jax.experimental import pallas as pl
from jax.experimental.pallas import tpu as pltpu

F32 = jnp.float32
BF16 = jnp.bfloat16
I32 = jnp.int32

EPS = 1e-6
PAST_LEN = 16384
POOL_WINDOWS = (2, 4, 8, 16)
TOP_K = 2
LANES = 128
SUBLANES = 8
HIST = 32
ISSUE_UNROLL = 8
VMEM_LIMIT = 56 * 1024 * 1024


def _cparams(sem, vmem=VMEM_LIMIT):
    return pltpu.CompilerParams(dimension_semantics=sem, vmem_limit_bytes=vmem)


def _rms(x, g):
    return x * lax.rsqrt(jnp.mean(x * x, axis=-1, keepdims=True) + EPS) * g


def _layernorm(x, g, b):
    mu = jnp.mean(x, axis=-1, keepdims=True)
    xc = x - mu
    var = jnp.mean(xc * xc, axis=-1, keepdims=True)
    return xc * lax.rsqrt(var + EPS) * g + b


def _silu(x):
    return x * jax.nn.sigmoid(x)


def _rmsnorm_body(x_ref, g_ref, o_ref):
    o_ref[...] = _rms(x_ref[...], g_ref[...]).astype(o_ref.dtype)


def _rmsnorm(x, g, out_dtype, tm):
    m, d = x.shape
    return pl.pallas_call(
        _rmsnorm_body,
        grid=(m // tm,),
        in_specs=[pl.BlockSpec((tm, d), lambda i: (i, 0)),
                  pl.BlockSpec((1, d), lambda i: (0, 0))],
        out_specs=pl.BlockSpec((tm, d), lambda i: (i, 0)),
        out_shape=jax.ShapeDtypeStruct((m, d), out_dtype),
        compiler_params=_cparams(("arbitrary",)),
        name="rmsnorm",
    )(x, g.reshape(1, d))


CAST_ROWS = 256


def _mm_body(grp_ref, first_ref, nv_ref, *refs, n_w, mode):
    del grp_ref
    x_ref = refs[0]
    w_refs = refs[1:1 + n_w]
    pos = 1 + n_w
    res_ref = None
    if mode == "resid":
        res_ref = refs[pos]
        pos += 1
    o_ref = refs[pos]
    wb_refs = refs[pos + 1:pos + 1 + n_w]
    m = pl.program_id(1)
    valid = m < nv_ref[0]

    @pl.when(jnp.logical_and(valid, first_ref[m] == 1))
    def _cast():
        for w_ref, wb_ref in zip(w_refs, wb_refs):
            def body(c, carry, w_ref=w_ref, wb_ref=wb_ref):
                r = pl.multiple_of(c * CAST_ROWS, CAST_ROWS)
                wb_ref[pl.ds(r, CAST_ROWS), :] = w_ref[pl.ds(r, CAST_ROWS), :].astype(BF16)
                return carry
            lax.fori_loop(0, w_ref.shape[0] // CAST_ROWS, body, 0)

    @pl.when(valid)
    def _compute():
        x = x_ref[...]
        if mode == "gated":
            a = jnp.dot(x, wb_refs[0][...], preferred_element_type=F32)
            b = jnp.dot(x, wb_refs[1][...], preferred_element_type=F32)
            o_ref[...] = (_silu(a) * b).astype(o_ref.dtype)
        else:
            acc = jnp.dot(x, wb_refs[0][...], preferred_element_type=F32)
            if mode == "resid":
                acc = res_ref[...] + acc
            o_ref[...] = acc.astype(o_ref.dtype)

    @pl.when(jnp.logical_not(valid))
    def _zero():
        o_ref[...] = jnp.zeros_like(o_ref)


def _grouped_matmul(x, ws, tile_grp, tile_first, n_valid, *, tm, tn, mode, out_dtype, res=None, name,
                    weight_buffers=2):
    m, k = x.shape
    n = ws[0].shape[2]
    assert m % tm == 0 and n % tn == 0 and k % CAST_ROWS == 0
    n_w = len(ws)

    def row_of(mi, nv):
        return jnp.minimum(mi, nv[0] - 1)

    in_specs = [pl.BlockSpec((tm, k), lambda ni, mi, g, f, nv: (row_of(mi, nv), 0))]
    for _ in ws:
        in_specs.append(pl.BlockSpec((None, k, tn), lambda ni, mi, g, f, nv: (g[row_of(mi, nv)], 0, ni),
                                     pipeline_mode=pl.Buffered(weight_buffers)))
    args = [x] + list(ws)
    if mode == "resid":
        in_specs.append(pl.BlockSpec((tm, tn), lambda ni, mi, g, f, nv: (row_of(mi, nv), ni)))
        args.append(res)
    grid_spec = pltpu.PrefetchScalarGridSpec(
        num_scalar_prefetch=3,
        grid=(n // tn, m // tm),
        in_specs=in_specs,
        out_specs=pl.BlockSpec((tm, tn), lambda ni, mi, g, f, nv: (mi, ni)),
        scratch_shapes=[pltpu.VMEM((k, tn), BF16) for _ in ws],
    )
    return pl.pallas_call(
        functools.partial(_mm_body, n_w=n_w, mode=mode),
        grid_spec=grid_spec,
        out_shape=jax.ShapeDtypeStruct((m, n), out_dtype),
        compiler_params=_cparams(("arbitrary", "arbitrary")),
        name=name,
    )(tile_grp, tile_first, n_valid, *args)


def _dense_matmul(x, w, layer, *, tm, tn, mode, out_dtype, res=None, w2=None, name):
    tiles = x.shape[0] // tm
    grp = jnp.full((tiles,), layer, I32)
    first = jnp.zeros((tiles,), I32).at[0].set(1)
    nv = jnp.full((1,), tiles, I32)
    ws = [w] if w2 is None else [w, w2]
    return _grouped_matmul(x, ws, grp, first, nv, tm=tm, tn=tn, mode=mode, out_dtype=out_dtype,
                           res=res, name=name)


def _mix_prompt_body(z_ref, caw_ref, cbw_ref, cbb_ref, lbg_ref, lbb_ref, lcg_ref, lcb_ref, spw_ref,
                     spbt_ref, pw_ref, ps_ref, y_ref, na_ref, nb_ref, nd_ref,
                     qa_ext, gb_ext, pd_ext, vn_buf, wm_ref, gsh_ref, *, rows, width, ka, kb, nbuf_d):
    i = pl.program_id(1)
    last = pl.num_programs(1) - 1
    w = width
    gw = w // len(POOL_WINDOWS)
    chunk = wm_ref.shape[1]
    cg = wm_ref.shape[0]
    cgw = w // cg
    sub = 32

    @pl.when(i == 0)
    def _init():
        zeros = jnp.zeros((HIST, w), F32)
        qa_ext[0:HIST, :] = zeros
        gb_ext[0:HIST, :] = zeros
        pd_ext[0:HIST, :] = zeros
        tri = lax.broadcasted_iota(I32, (chunk, chunk), 0) >= lax.broadcasted_iota(I32, (chunk, chunk), 1)
        for g in range(cg):
            wm_ref[g] = jnp.where(tri, spw_ref[g], 0.0).astype(BF16)

    def col(k):
        return z_ref[:, k * w:(k + 1) * w].astype(F32)

    qa_ext[HIST:HIST + rows, :] = col(0) * col(1)
    gb_ext[HIST:HIST + rows, :] = col(3) * jax.nn.sigmoid(col(4))
    pd_ext[HIST:HIST + rows, :] = col(7)
    vn_buf[...] = _layernorm(col(6), lcg_ref[...], lcb_ref[...]).astype(BF16)

    conv_a = None
    for j in range(ka):
        term = caw_ref[j:j + 1, :] * qa_ext[HIST - (ka - 1) + j:HIST - (ka - 1) + j + rows, :]
        conv_a = term if conv_a is None else conv_a + term
    y_ref[:, 0:w] = (col(2) * conv_a).astype(y_ref.dtype)

    for f in range(1, SUBLANES):
        gsh_ref[f - 1, SUBLANES:HIST + rows, :] = gb_ext[SUBLANES - f:HIST + rows - f, :]
    for r0 in range(0, rows, sub):
        acc = None
        for j in range(kb):
            back = kb - 1 - j
            f, s = back % SUBLANES, HIST + r0 - (back - back % SUBLANES)
            src = gb_ext[s:s + sub, :] if f == 0 else gsh_ref[f - 1, s:s + sub, :]
            term = cbw_ref[j:j + 1, :] * src
            acc = term if acc is None else acc + term
        yb = _silu(_layernorm(acc + cbb_ref[...], lbg_ref[...], lbb_ref[...]))
        y_ref[r0:r0 + sub, w:2 * w] = yb.astype(y_ref.dtype)

    for c0 in range(0, rows, chunk):
        for g in range(cg):
            lanes = slice(g * cgw, (g + 1) * cgw)
            s = jnp.dot(wm_ref[g], vn_buf[c0:c0 + chunk, lanes], preferred_element_type=F32)
            s = s + spbt_ref[:, g:g + 1]
            u = z_ref[c0:c0 + chunk, 5 * w + g * cgw:5 * w + (g + 1) * cgw].astype(F32)
            y_ref[c0:c0 + chunk, 2 * w + g * cgw:2 * w + (g + 1) * cgw] = (u * s).astype(y_ref.dtype)

    pos = i * rows + lax.broadcasted_iota(I32, (rows, 1), 0)
    for gi, win in enumerate(POOL_WINDOWS):
        lanes = slice(gi * gw, (gi + 1) * gw)
        tot = None
        for k in range(win):
            term = pd_ext[HIST - k:HIST - k + rows, lanes]
            tot = term if tot is None else tot + term
        cnt = jnp.minimum(pos + 1, win).astype(F32)
        d = tot / cnt - pd_ext[HIST:HIST + rows, lanes]
        yd = jnp.dot(d.astype(BF16), pw_ref[gi].astype(BF16), preferred_element_type=F32)
        y_ref[:, 3 * w + gi * gw:3 * w + (gi + 1) * gw] = (yd * ps_ref[:, lanes]).astype(y_ref.dtype)

    @pl.when(i == last)
    def _state():
        na_ref[...] = qa_ext[HIST + rows - (ka - 1):HIST + rows, :]
        nb_ref[...] = gb_ext[HIST + rows - (kb - 1):HIST + rows, :]
        nd_ref[...] = pd_ext[HIST + rows - nbuf_d:HIST + rows, :]

    qa_ext[0:HIST, :] = qa_ext[rows:rows + HIST, :]
    gb_ext[0:HIST, :] = gb_ext[rows:rows + HIST, :]
    pd_ext[0:HIST, :] = pd_ext[rows:rows + HIST, :]


def _mix_prompt(z, lw, *, batch, seq, rows):
    w = lw["conv_a_w"].shape[1]
    ka = lw["conv_a_w"].shape[0]
    kb = lw["conv_b_w"].shape[0]
    nbuf_d = max(POOL_WINDOWS) - 1
    cg, chunk = lw["sp_w"].shape[0], lw["sp_w"].shape[1]
    assert seq % rows == 0 and rows % chunk == 0 and kb - 1 <= HIST and nbuf_d <= HIST
    nt = seq // rows
    full2 = lambda a: pl.BlockSpec(a.shape, lambda b, i: (0, 0))
    full3 = lambda a: pl.BlockSpec(a.shape, lambda b, i: (0, 0, 0))
    vec = lambda a: a.reshape(1, w)
    small = [lw["conv_a_w"], lw["conv_b_w"], vec(lw["conv_b_b"]), vec(lw["ln_b_g"]), vec(lw["ln_b_b"]),
             vec(lw["ln_c_g"]), vec(lw["ln_c_b"])]
    spbt = lw["sp_b"].T
    in_specs = ([pl.BlockSpec((rows, 8 * w), lambda b, i: (b * nt + i, 0))]
                + [full2(a) for a in small]
                + [full3(lw["sp_w"]), full2(spbt), full3(lw["pool_w"]), full2(vec(lw["pool_scale"]))])
    out_shapes = (jax.ShapeDtypeStruct((batch * seq, 4 * w), BF16),
                  jax.ShapeDtypeStruct((batch, ka - 1, w), F32),
                  jax.ShapeDtypeStruct((batch, kb - 1, w), F32),
                  jax.ShapeDtypeStruct((batch, nbuf_d, w), F32))
    out_specs = (pl.BlockSpec((rows, 4 * w), lambda b, i: (b * nt + i, 0)),
                 pl.BlockSpec((None, ka - 1, w), lambda b, i: (b, 0, 0)),
                 pl.BlockSpec((None, kb - 1, w), lambda b, i: (b, 0, 0)),
                 pl.BlockSpec((None, nbuf_d, w), lambda b, i: (b, 0, 0)))
    scratch = [pltpu.VMEM((HIST + rows, w), F32), pltpu.VMEM((HIST + rows, w), F32),
               pltpu.VMEM((HIST + rows, w), F32), pltpu.VMEM((rows, w), BF16),
               pltpu.VMEM((cg, chunk, chunk), BF16),
               pltpu.VMEM((SUBLANES - 1, HIST + rows, w), F32)]
    return pl.pallas_call(
        functools.partial(_mix_prompt_body, rows=rows, width=w, ka=ka, kb=kb, nbuf_d=nbuf_d),
        grid=(batch, nt),
        in_specs=in_specs,
        out_specs=out_specs,
        out_shape=out_shapes,
        scratch_shapes=scratch,
        compiler_params=_cparams(("arbitrary", "arbitrary")),
        name="mix_prompt",
    )(z, *small, lw["sp_w"], spbt, lw["pool_w"], vec(lw["pool_scale"]))


def _mix_sample_body(spw_ref, spb_ref, z_ref, sa_ref, sb_ref, sd_ref, caw_ref, cbw_ref, cbb_ref,
                     lbg_ref, lbb_ref, lcg_ref, lcb_ref, pw_ref, ps_ref,
                     y_ref, na_ref, nb_ref, nd_ref, cv_ref, *, nseq, steps, width, ka, kb, nbuf_d, cg):
    w = width
    gw = w // len(POOL_WINDOWS)
    cgw = w // cg
    lane = lax.broadcasted_iota(I32, (1, w), 1)

    def col(t, k):
        return z_ref[t * nseq:(t + 1) * nseq, k * w:(k + 1) * w].astype(F32)

    def put(t, k, val):
        y_ref[t * nseq:(t + 1) * nseq, k * w:(k + 1) * w] = val.astype(y_ref.dtype)

    ext = [sa_ref[k] for k in range(ka - 1)] + [col(t, 0) * col(t, 1) for t in range(steps)]
    for t in range(steps):
        conv = None
        for j in range(ka):
            term = caw_ref[j:j + 1, :] * ext[t + j]
            conv = term if conv is None else conv + term
        put(t, 0, col(t, 2) * conv)
    for k in range(ka - 1):
        na_ref[k] = ext[steps + k]

    for t in range(steps):
        nb_ref[kb - 1 - steps + t] = col(t, 3) * jax.nn.sigmoid(col(t, 4))
    for k in range(kb - 1 - steps):
        nb_ref[k] = sb_ref[k + steps]

    def ext_b(k):
        return sb_ref[k] if k < kb - 1 else nb_ref[k - steps]

    for t in range(steps):
        acc = None
        for j in range(kb):
            term = cbw_ref[j:j + 1, :] * ext_b(t + j)
            acc = term if acc is None else acc + term
        put(t, 1, _silu(_layernorm(acc + cbb_ref[...], lbg_ref[...], lbb_ref[...])))

    def lane_groups(vals):
        out = jnp.full((1, w), vals[cg - 1], F32)
        for g in range(cg - 2, -1, -1):
            out = jnp.where(lane < (g + 1) * cgw, vals[g], out)
        return out

    for t in range(steps):
        cv_ref[t] = _layernorm(col(t, 6), lcg_ref[...], lcb_ref[...])
    for t in range(steps):
        s = lane_groups([spb_ref[g, t] for g in range(cg)])
        for sp in range(t + 1):
            s = s + lane_groups([spw_ref[g, t, sp] for g in range(cg)]) * cv_ref[sp]
        put(t, 2, col(t, 5) * s)

    for t in range(steps):
        nd_ref[nbuf_d - steps + t] = col(t, 7)
    for k in range(nbuf_d - steps):
        nd_ref[k] = sd_ref[k + steps]

    def ext_d(k):
        return sd_ref[k] if k < nbuf_d else nd_ref[k - steps]

    for t in range(steps):
        for gi, win in enumerate(POOL_WINDOWS):
            lanes = slice(gi * gw, (gi + 1) * gw)
            tot = None
            for k in range(win):
                term = ext_d(nbuf_d + t - k)[:, lanes]
                tot = term if tot is None else tot + term
            cnt = float(min(PAST_LEN + t + 1, win))
            d = tot / cnt - ext_d(nbuf_d + t)[:, lanes]
            yd = jnp.dot(d.astype(BF16), pw_ref[gi].astype(BF16), preferred_element_type=F32)
            y_ref[t * nseq:(t + 1) * nseq, 3 * w + gi * gw:3 * w + (gi + 1) * gw] = (
                yd * ps_ref[:, lanes]).astype(y_ref.dtype)


def _mix_sample(z, sa, sb, sd, lw, *, nseq, steps, m_prompt):
    w = lw["conv_a_w"].shape[1]
    ka = lw["conv_a_w"].shape[0]
    kb = lw["conv_b_w"].shape[0]
    nbuf_d = max(POOL_WINDOWS) - 1
    cg = lw["sp_w"].shape[0]
    ms = nseq * steps
    assert m_prompt % ms == 0 and steps <= nbuf_d and steps <= kb - 1
    rb = m_prompt // ms
    spw_small = lw["sp_w"][:, :steps, :steps]
    spb_small = lw["sp_b"][:, :steps]
    vec = lambda a: a.reshape(1, w)
    smem = pl.BlockSpec(memory_space=pltpu.SMEM)
    full2 = lambda a: pl.BlockSpec(a.shape, lambda i: (0, 0))
    full3 = lambda a: pl.BlockSpec(a.shape, lambda i: (0, 0, 0))
    small = [lw["conv_a_w"], lw["conv_b_w"], vec(lw["conv_b_b"]), vec(lw["ln_b_g"]), vec(lw["ln_b_b"]),
             vec(lw["ln_c_g"]), vec(lw["ln_c_b"])]
    in_specs = ([smem, smem,
                 pl.BlockSpec((ms, 8 * w), lambda i: (rb, 0)),
                 full3(sa), full3(sb), full3(sd)]
                + [full2(a) for a in small]
                + [full3(lw["pool_w"]), full2(vec(lw["pool_scale"]))])
    out_shapes = (jax.ShapeDtypeStruct((ms, 4 * w), BF16),
                  jax.ShapeDtypeStruct(sa.shape, F32), jax.ShapeDtypeStruct(sb.shape, F32),
                  jax.ShapeDtypeStruct(sd.shape, F32), jax.ShapeDtypeStruct((steps, nseq, w), F32))
    out_specs = (pl.BlockSpec((ms, 4 * w), lambda i: (0, 0)),
                 full3(sa), full3(sb), full3(sd),
                 pl.BlockSpec((steps, nseq, w), lambda i: (0, 0, 0)))
    return pl.pallas_call(
        functools.partial(_mix_sample_body, nseq=nseq, steps=steps, width=w, ka=ka, kb=kb,
                          nbuf_d=nbuf_d, cg=cg),
        grid=(1,),
        in_specs=in_specs,
        out_specs=out_specs,
        out_shape=out_shapes,
        compiler_params=_cparams(("arbitrary",)),
        name="mix_sample",
    )(spw_small, spb_small, z, sa, sb, sd, *small, lw["pool_w"], vec(lw["pool_scale"]))


def _merge_body(yp_ref, ys_ref, g0_ref, g1_ref, g2_ref, g3_ref, wa_ref, wb_ref, wc_ref, wd_ref, o_ref, wbf_ref,
                *, width, prompt_tiles):
    i = pl.program_id(0)

    @pl.when(i == 0)
    def _cast():
        for k, w_ref in enumerate((wa_ref, wb_ref, wc_ref, wd_ref)):
            wbf_ref[k] = w_ref[...].astype(BF16)

    def run(y_ref):
        acc = None
        for k, g_ref in enumerate((g0_ref, g1_ref, g2_ref, g3_ref)):
            br = jnp.dot(y_ref[:, k * width:(k + 1) * width], wbf_ref[k], preferred_element_type=F32)
            term = jax.nn.sigmoid(g_ref[...].astype(F32)) * br
            acc = term if acc is None else acc + term
        o_ref[...] = acc.astype(o_ref.dtype)

    @pl.when(i < prompt_tiles)
    def _prompt():
        run(yp_ref)

    @pl.when(i >= prompt_tiles)
    def _sample():
        run(ys_ref)


def _merge(y_p, y_s, z, w_brs, layer, *, tm):
    mp, yw = y_p.shape
    ms = y_s.shape[0]
    assert mp % tm == 0 and ms % tm == 0
    tp = mp // tm
    w = yw // 4
    d = w_brs[0].shape[2]
    assert (2 * yw) % d == 0
    goff = (2 * yw) // d
    gate_spec = lambda k: pl.BlockSpec((tm, d), lambda i, k=k: (i, goff + k))
    w_spec = pl.BlockSpec((None, w, d), lambda i: (layer, 0, 0), pipeline_mode=pl.Buffered(1))
    return pl.pallas_call(
        functools.partial(_merge_body, width=w, prompt_tiles=tp),
        grid=((mp + ms) // tm,),
        in_specs=[pl.BlockSpec((tm, yw), lambda i: (jnp.minimum(i, tp - 1), 0)),
                  pl.BlockSpec((tm, yw), lambda i: (jnp.maximum(i - tp, 0), 0))]
                 + [gate_spec(k) for k in range(4)] + [w_spec] * 4,
        out_specs=pl.BlockSpec((tm, d), lambda i: (i, 0)),
        out_shape=jax.ShapeDtypeStruct((mp + ms, d), BF16),
        scratch_shapes=[pltpu.VMEM((4, w, d), BF16)],
        compiler_params=_cparams(("arbitrary",)),
        name="merge",
    )(y_p, y_s, z, z, z, z, *w_brs)


def _outproj_body(m_ref, x_ref, w_ref, g_ref, *refs, with_h):
    x1_ref = refs[0]
    wbf_ref = refs[-1]

    @pl.when(pl.program_id(0) == 0)
    def _cast():
        wbf_ref[...] = w_ref[...].astype(BF16)

    x1 = x_ref[...] + jnp.dot(m_ref[...], wbf_ref[...], preferred_element_type=F32)
    x1_ref[...] = x1
    if with_h:
        refs[1][...] = _rms(x1, g_ref[...]).astype(BF16)


def _outproj(merged, x, w_out, layer, g, *, tm, with_h):
    m, d = x.shape
    row = pl.BlockSpec((tm, d), lambda i: (i, 0))
    out_shape = [jax.ShapeDtypeStruct((m, d), F32)]
    out_specs = [row]
    if with_h:
        out_shape.append(jax.ShapeDtypeStruct((m, d), BF16))
        out_specs.append(row)
    return pl.pallas_call(
        functools.partial(_outproj_body, with_h=with_h),
        grid=(m // tm,),
        in_specs=[row, row,
                  pl.BlockSpec((None, d, d), lambda i: (layer, 0, 0), pipeline_mode=pl.Buffered(1)),
                  pl.BlockSpec((1, d), lambda i: (0, 0))],
        out_specs=out_specs,
        out_shape=out_shape,
        scratch_shapes=[pltpu.VMEM((d, d), BF16)],
        compiler_params=_cparams(("arbitrary",)),
        name="outproj",
    )(merged, x, w_out, g.reshape(1, d))


def _route_body(x_ref, g_ref, rwt_ref, e_ref, r_ref, wt_ref, cnt_ref, carry_ref, tri_ref):
    i = pl.program_id(0)
    ne = rwt_ref.shape[0]
    tm = x_ref.shape[0]

    @pl.when(i == 0)
    def _init():
        carry_ref[...] = jnp.zeros_like(carry_ref)
        upper = lax.broadcasted_iota(I32, (tm, tm), 0) <= lax.broadcasted_iota(I32, (tm, tm), 1)
        tri_ref[...] = jnp.where(upper, 1.0, 0.0).astype(BF16)

    h = _rms(x_ref[...], g_ref[...])
    h_hi = h.astype(BF16)
    h_lo = (h - h_hi.astype(F32)).astype(BF16)
    rw = rwt_ref[...]
    rw_hi = rw.astype(BF16)
    rw_lo = (rw - rw_hi.astype(F32)).astype(BF16)
    dn = (((1,), (1,)), ((), ()))
    dg = lambda a, b: lax.dot_general(a, b, dn, preferred_element_type=F32)
    logits = dg(rw_hi, h_hi) + dg(rw_hi, h_lo) + dg(rw_lo, h_hi)
    ex = jnp.exp(logits - jnp.max(logits, axis=0, keepdims=True))
    p = ex / jnp.sum(ex, axis=0, keepdims=True)
    eid = lax.broadcasted_iota(I32, (ne, tm), 0)
    m1 = jnp.max(p, axis=0, keepdims=True)
    i1 = jnp.min(jnp.where(p == m1, eid, ne), axis=0, keepdims=True)
    oh1 = eid == i1
    p2 = jnp.where(oh1, -1.0, p)
    m2 = jnp.max(p2, axis=0, keepdims=True)
    i2 = jnp.min(jnp.where(p2 == m2, eid, ne), axis=0, keepdims=True)
    oh2 = eid == i2
    den = m1 + m2
    sel = jnp.logical_or(oh1, oh2)
    sel_f = jnp.where(sel, 1.0, 0.0)
    cum = jnp.dot(sel_f.astype(BF16), tri_ref[...], preferred_element_type=F32) + carry_ref[...]
    rank1 = jnp.sum(jnp.where(oh1, cum, 0.0), axis=0, keepdims=True) - 1.0
    rank2 = jnp.sum(jnp.where(oh2, cum, 0.0), axis=0, keepdims=True) - 1.0
    carry_ref[...] = carry_ref[...] + jnp.sum(sel_f, axis=1, keepdims=True)
    e_ref[0:1, :] = i1
    e_ref[1:2, :] = i2
    r_ref[0:1, :] = rank1.astype(I32)
    r_ref[1:2, :] = rank2.astype(I32)
    wt_ref[0:1, :] = m1 / den
    wt_ref[1:2, :] = m2 / den

    @pl.when(i == pl.num_programs(0) - 1)
    def _counts():
        cnt_ref[...] = jnp.broadcast_to(carry_ref[...], cnt_ref.shape).astype(I32)


def _route(x1, g, router_w, *, tm):
    m, d = x1.shape
    ne = router_w.shape[1]
    tok = lambda dt: jax.ShapeDtypeStruct((TOP_K, m), dt)
    tok_spec = pl.BlockSpec((TOP_K, tm), lambda i: (0, i))
    return pl.pallas_call(
        _route_body,
        grid=(m // tm,),
        in_specs=[pl.BlockSpec((tm, d), lambda i: (i, 0)),
                  pl.BlockSpec((1, d), lambda i: (0, 0)),
                  pl.BlockSpec((ne, d), lambda i: (0, 0))],
        out_specs=(tok_spec, tok_spec, tok_spec, pl.BlockSpec((ne, LANES), lambda i: (0, 0))),
        out_shape=(tok(I32), tok(I32), tok(F32), jax.ShapeDtypeStruct((ne, LANES), I32)),
        scratch_shapes=[pltpu.VMEM((ne, 1), F32), pltpu.VMEM((tm, tm), BF16)],
        compiler_params=_cparams(("arbitrary",)),
        name="route",
    )(x1, g.reshape(1, d), router_w.T)


def _plan_body(e_ref, r_ref, cnt_ref, pos_ref, tgrp_ref, tfirst_ref, misc_ref, *, tile_shift):
    ne = cnt_ref.shape[0]
    tile = 1 << tile_shift
    sub = lax.broadcasted_iota(I32, (ne, LANES), 0)
    lane = lax.broadcasted_iota(I32, (ne, LANES), 1)

    def as_row(col):
        return jnp.sum(jnp.where(sub == lane, col, 0), axis=0, keepdims=True)

    cnt = cnt_ref[:, 0:1]
    padded = lax.shift_left(lax.shift_right_logical(cnt + (tile - 1), tile_shift), tile_shift)
    gstart = jnp.sum(jnp.where(lane < sub, as_row(padded), 0), axis=1, keepdims=True)
    gend = gstart + padded
    m = e_ref.shape[1]
    eid = lax.broadcasted_iota(I32, (ne, m), 0)
    for k in range(TOP_K):
        base = jnp.sum(jnp.where(eid == e_ref[k:k + 1, :], gstart, 0), axis=0, keepdims=True)
        pos_ref[k:k + 1, :] = base + r_ref[k:k + 1, :]
    start = lax.broadcasted_iota(I32, (1, LANES), 1) * tile
    grp = jnp.sum((gend <= start).astype(I32), axis=0, keepdims=True)
    tgrp_ref[...] = jnp.minimum(grp, ne - 1)
    tfirst_ref[...] = jnp.sum(jnp.logical_and(gstart == start, padded > 0).astype(I32), axis=0, keepdims=True)
    n_tiles = lax.shift_right_logical(jnp.max(gend, axis=0, keepdims=True), tile_shift)
    misc = jnp.where(sub == 0, n_tiles, 0)
    misc = jnp.where(sub == 1, as_row(gstart + cnt), misc)
    misc = jnp.where(sub == 2, as_row(gend), misc)
    misc_ref[...] = misc


def _plan(e, r, cnt, *, tile):
    tile_shift = tile.bit_length() - 1
    assert 1 << tile_shift == tile
    ne = cnt.shape[0]
    m = e.shape[1]
    whole = lambda a: pl.BlockSpec(a.shape, lambda i: (0, 0))
    row = jax.ShapeDtypeStruct((1, LANES), I32)
    out_shape = (jax.ShapeDtypeStruct((TOP_K, m), I32), row, row, jax.ShapeDtypeStruct((ne, LANES), I32))
    return pl.pallas_call(
        functools.partial(_plan_body, tile_shift=tile_shift),
        grid=(1,),
        in_specs=[whole(e), whole(r), whole(cnt)],
        out_specs=tuple(pl.BlockSpec(s.shape, lambda i: (0, 0)) for s in out_shape),
        out_shape=out_shape,
        compiler_params=_cparams(("arbitrary",)),
        name="plan",
    )(e, r, cnt)


def _dispatch_body(pos0_ref, pos1_ref, nv_ref, fill_ref, end_ref, x_hbm, g_ref, o_ref, src_ref, buf_ref, sem_ref):
    j = pl.program_id(0)
    tm = o_ref.shape[0]
    n_tok = pos0_ref.shape[0]
    nv = nv_ref[0]

    def start_tile(tile, slot):
        def body(r, carry):
            tok = src_ref[tile * tm + r]
            pltpu.make_async_copy(x_hbm.at[pl.ds(tok, 1), :], buf_ref.at[slot, pl.ds(r, 1), :],
                                  sem_ref.at[slot]).start()
            return carry
        lax.fori_loop(0, tm, body, 0, unroll=ISSUE_UNROLL)

    @pl.when(j == 0)
    def _prologue():
        for g in range(fill_ref.shape[0]):
            def fill(p, carry):
                src_ref[p] = 0
                return carry
            lax.fori_loop(fill_ref[g], end_ref[g], fill, 0)

        def inv(t, carry):
            src_ref[pos0_ref[t]] = t
            src_ref[pos1_ref[t]] = t
            return carry
        lax.fori_loop(0, n_tok, inv, 0, unroll=ISSUE_UNROLL)
        start_tile(0, 0)

    slot = lax.rem(j, 2)

    @pl.when(j < nv)
    def _work():
        @pl.when(j + 1 < nv)
        def _prefetch():
            start_tile(j + 1, 1 - slot)

        pltpu.make_async_copy(x_hbm.at[pl.ds(0, tm), :], buf_ref.at[slot], sem_ref.at[slot]).wait()
        o_ref[...] = _rms(buf_ref[slot], g_ref[...]).astype(o_ref.dtype)

    @pl.when(j >= nv)
    def _skip():
        o_ref[...] = jnp.zeros_like(o_ref)


def _dispatch(x1, g, pos0, pos1, n_tiles, fill_row, end_row, *, tm, max_tiles):
    m, d = x1.shape
    grid_spec = pltpu.PrefetchScalarGridSpec(
        num_scalar_prefetch=5,
        grid=(max_tiles,),
        in_specs=[pl.BlockSpec(memory_space=pl.ANY),
                  pl.BlockSpec((1, d), lambda j, *_: (0, 0))],
        out_specs=pl.BlockSpec((tm, d), lambda j, *_: (j, 0)),
        scratch_shapes=[pltpu.SMEM((max_tiles * tm,), I32),
                        pltpu.VMEM((2, tm, d), F32),
                        pltpu.SemaphoreType.DMA((2,))],
    )
    return pl.pallas_call(
        _dispatch_body,
        grid_spec=grid_spec,
        out_shape=jax.ShapeDtypeStruct((max_tiles * tm, d), BF16),
        compiler_params=_cparams(("arbitrary",)),
        name="dispatch",
    )(pos0, pos1, n_tiles, fill_row, end_row, x1, g.reshape(1, d))


def _combine_body(pos0_ref, pos1_ref, ys_hbm, x_ref, wt_ref, g_ref, xn_ref, h_ref, buf_ref, sem_ref):
    i = pl.program_id(0)
    n = pl.num_programs(0)
    tm = x_ref.shape[0]

    def start_tile(tile, slot):
        def body(r, carry):
            t = tile * tm + r
            pltpu.make_async_copy(ys_hbm.at[pl.ds(pos0_ref[t], 1), :], buf_ref.at[slot, 0, pl.ds(r, 1), :],
                                  sem_ref.at[slot, 0]).start()
            pltpu.make_async_copy(ys_hbm.at[pl.ds(pos1_ref[t], 1), :], buf_ref.at[slot, 1, pl.ds(r, 1), :],
                                  sem_ref.at[slot, 1]).start()
            return carry
        lax.fori_loop(0, tm, body, 0, unroll=ISSUE_UNROLL)

    @pl.when(i == 0)
    def _prologue():
        start_tile(0, 0)

    slot = lax.rem(i, 2)

    @pl.when(i + 1 < n)
    def _prefetch():
        start_tile(i + 1, 1 - slot)

    for k in range(TOP_K):
        pltpu.make_async_copy(ys_hbm.at[pl.ds(0, tm), :], buf_ref.at[slot, k], sem_ref.at[slot, k]).wait()
    xn = x_ref[...] + wt_ref[:, 0:1] * buf_ref[slot, 0] + wt_ref[:, 1:2] * buf_ref[slot, 1]
    xn_ref[...] = xn
    h_ref[...] = _rms(xn, g_ref[...]).astype(h_ref.dtype)


def _combine(ys, x1, wt_cols, g, pos0, pos1, *, tm, h_dtype):
    m, d = x1.shape
    row = lambda: pl.BlockSpec((tm, d), lambda i, *_: (i, 0))
    grid_spec = pltpu.PrefetchScalarGridSpec(
        num_scalar_prefetch=2,
        grid=(m // tm,),
        in_specs=[pl.BlockSpec(memory_space=pl.ANY), row(),
                  pl.BlockSpec((tm, TOP_K), lambda i, *_: (i, 0)),
                  pl.BlockSpec((1, d), lambda i, *_: (0, 0))],
        out_specs=(row(), row()),
        scratch_shapes=[pltpu.VMEM((2, TOP_K, tm, d), F32), pltpu.SemaphoreType.DMA((2, TOP_K))],
    )
    return pl.pallas_call(
        _combine_body,
        grid_spec=grid_spec,
        out_shape=(jax.ShapeDtypeStruct((m, d), F32), jax.ShapeDtypeStruct((m, d), h_dtype)),
        compiler_params=_cparams(("arbitrary",)),
        name="combine",
    )(pos0, pos1, ys, x1, wt_cols, g.reshape(1, d))


def _pick(total, prefs):
    for p in prefs:
        if total % p == 0:
            return p
    raise ValueError(f"no tile for {total} in {prefs}")


def _row_tile(m, cap):
    best = None
    for t in range(16, cap + 1, 16):
        if m % t == 0:
            best = t
    assert best is not None
    return best


def kernel(x_prompt, x_sample, state_conv_a, state_conv_b, state_pool, norm_mix_g, w_in, conv_a_w, w_br_a,
           conv_b_w, conv_b_b, ln_b_g, ln_b_b, w_br_b, ln_c_g, ln_c_b, sp_w, sp_b, w_br_c, pool_w,
           pool_scale, w_br_d, w_out, norm_ffn_g, ffn_w1, ffn_w3, ffn_w2, router_w, moe_w1, moe_w3, moe_w2,
           final_norm_g):
    batch, seq, d = x_prompt.shape
    nseq, steps, _ = x_sample.shape
    depth = w_in.shape[0]
    n_exp = router_w.shape[2]
    mp, ms = batch * seq, nseq * steps
    m = mp + ms

    x = jnp.concatenate([x_prompt.reshape(mp, d), jnp.swapaxes(x_sample, 0, 1).reshape(ms, d)], axis=0)
    st_a = jnp.swapaxes(state_conv_a, 1, 2)
    st_b = jnp.swapaxes(state_conv_b, 1, 2)
    st_d = jnp.swapaxes(state_pool, 1, 2)

    tm_big = _row_tile(m, 1088)
    tm_mid = _row_tile(m, 544)
    tm_small = _row_tile(m, 256)
    tm_merge = min(_row_tile(mp, 256), _row_tile(ms, 256))
    assert mp % tm_merge == 0 and ms % tm_merge == 0
    tm_moe = 512
    max_tiles = -(-(m * TOP_K) // tm_moe) + n_exp
    assert max_tiles <= LANES
    mix_rows = _pick(seq, (256, 128))
    moe_w1f = moe_w1.reshape((-1,) + moe_w1.shape[2:])
    moe_w3f = moe_w3.reshape((-1,) + moe_w3.shape[2:])
    moe_w2f = moe_w2.reshape((-1,) + moe_w2.shape[2:])
    dffe = moe_w1.shape[3]
    dff = ffn_w1.shape[2]

    h = _rmsnorm(x, norm_mix_g[0], BF16, tm_mid)
    outs_p = ([], [], [])
    outs_s = ([], [], [], [])
    y_final = None
    for i in range(depth):
        lw = dict(conv_a_w=conv_a_w[i], conv_b_w=conv_b_w[i], conv_b_b=conv_b_b[i], ln_b_g=ln_b_g[i],
                  ln_b_b=ln_b_b[i], ln_c_g=ln_c_g[i], ln_c_b=ln_c_b[i], sp_w=sp_w[i], sp_b=sp_b[i],
                  pool_w=pool_w[i], pool_scale=pool_scale[i])
        z = _dense_matmul(h, w_in, i, tm=tm_big, tn=_pick(w_in.shape[2], (1024, 512, 256, 128)),
                          mode="plain", out_dtype=BF16, name="inproj")
        y_p, na_p, nb_p, nd_p = _mix_prompt(z, lw, batch=batch, seq=seq, rows=mix_rows)
        y_s, na_s, nb_s, nd_s, cv_s = _mix_sample(z, st_a[i], st_b[i], st_d[i], lw, nseq=nseq, steps=steps,
                                                  m_prompt=mp)
        for acc, val in zip(outs_p, (na_p, nb_p, nd_p)):
            acc.append(val)
        for acc, val in zip(outs_s, (na_s, nb_s, nd_s, cv_s)):
            acc.append(val)
        merged = _merge(y_p, y_s, z, (w_br_a, w_br_b, w_br_c, w_br_d), i, tm=tm_merge)
        last = i == depth - 1
        g_next = final_norm_g if last else norm_mix_g[i + 1]
        h_dtype = F32 if last else BF16
        j = i // 2
        if i % 2 == 0:
            x1, h2 = _outproj(merged, x, w_out, i, norm_ffn_g[i], tm=tm_small, with_h=True)
            a = _dense_matmul(h2, ffn_w1, j, w2=ffn_w3, tm=tm_big, tn=_pick(dff, (512, 256, 128)),
                              mode="gated", out_dtype=BF16, name="ffn_up")
            x = _dense_matmul(a, ffn_w2, j, tm=tm_mid, tn=_pick(d, (512, 256, 128)), mode="resid",
                              out_dtype=F32, res=x1, name="ffn_down")
            h = _rmsnorm(x, g_next, h_dtype, tm_mid)
        else:
            (x1,) = _outproj(merged, x, w_out, i, norm_ffn_g[i], tm=tm_small, with_h=False)
            e, r, wt, cnt = _route(x1, norm_ffn_g[i], router_w[j], tm=_pick(m, (512, 256, 128)))
            pos, tgrp, tfirst, misc = _plan(e, r, cnt, tile=tm_moe)
            n_tiles = misc[0, 0:1]
            hs = _dispatch(x1, norm_ffn_g[i], pos[0], pos[1], n_tiles, misc[1, :n_exp], misc[2, :n_exp],
                           tm=tm_moe, max_tiles=max_tiles)
            grp = tgrp[0, :max_tiles] + j * n_exp
            first = tfirst[0, :max_tiles]
            a = _grouped_matmul(hs, [moe_w1f, moe_w3f], grp, first, n_tiles, tm=tm_moe,
                                tn=_pick(dffe, (1408, 1024, 512, 256, 128)), mode="gated", out_dtype=BF16,
                                name="moe_up", weight_buffers=1)
            ys = _grouped_matmul(a, [moe_w2f], grp, first, n_tiles, tm=tm_moe,
                                 tn=_pick(d, (1024, 512, 256, 128)), mode="plain", out_dtype=F32, name="moe_down")
            x, h = _combine(ys, x1, wt.T, g_next, pos[0], pos[1], tm=tm_small, h_dtype=h_dtype)
        if last:
            y_final = h

    y_prompt = y_final[:mp].reshape(batch, seq, d)
    y_sample = jnp.swapaxes(y_final[mp:].reshape(steps, nseq, d), 0, 1)
    stack_s = lambda vals: jnp.swapaxes(jnp.stack(vals), 1, 2)
    return (y_prompt, y_sample, jnp.stack(outs_p[0]), jnp.stack(outs_p[1]), jnp.stack(outs_p[2]),
            stack_s(outs_s[0]), stack_s(outs_s[1]), stack_s(outs_s[2]), stack_s(outs_s[3]))
```

```python
import functools

import jax
import jax.numpy as jnp
from jax import lax
from jax.experimental import pallas as pl
from jax.experimental.pallas import tpu as pltpu

F32 = jnp.float32
BF16 = jnp.bfloat16
I32 = jnp.int32

EPS = 1e-6
PAST_LEN = 16384
POOL_WINDOWS = (2, 4, 8, 16)
TOP_K = 2
LANES = 128
SUBLANES = 8
HIST = 32
ISSUE_UNROLL = 8
VMEM_LIMIT = 56 * 1024 * 1024


def _cparams(sem, vmem=VMEM_LIMIT):
    return pltpu.CompilerParams(dimension_semantics=sem, vmem_limit_bytes=vmem)


def _rms(x, g):
    return x * lax.rsqrt(jnp.mean(x * x, axis=-1, keepdims=True) + EPS) * g


def _layernorm(x, g, b):
    mu = jnp.mean(x, axis=-1, keepdims=True)
    xc = x - mu
    var = jnp.mean(xc * xc, axis=-1, keepdims=True)
    return xc * lax.rsqrt(var + EPS) * g + b


def _silu(x):
    return x * jax.nn.sigmoid(x)


def _rmsnorm_body(x_ref, g_ref, o_ref):
    o_ref[...] = _rms(x_ref[...], g_ref[...]).astype(o_ref.dtype)


def _rmsnorm(x, g, out_dtype, tm):
    m, d = x.shape
    return pl.pallas_call(
        _rmsnorm_body,
        grid=(m // tm,),
        in_specs=[pl.BlockSpec((tm, d), lambda i: (i, 0)),
                  pl.BlockSpec((1, d), lambda i: (0, 0))],
        out_specs=pl.BlockSpec((tm, d), lambda i: (i, 0)),
        out_shape=jax.ShapeDtypeStruct((m, d), out_dtype),
        compiler_params=_cparams(("arbitrary",)),
        name="rmsnorm",
    )(x, g.reshape(1, d))


CAST_ROWS = 256
STAGE_ROWS = 256


def _mm_body(grp_ref, first_ref, nv_ref, *refs, n_w, mode):
    del grp_ref
    x_ref = refs[0]
    w_refs = refs[1:1 + n_w]
    pos = 1 + n_w
    res_ref = None
    if mode == "resid":
        res_ref = refs[pos]
        pos += 1
    o_ref = refs[pos]
    wb_refs = refs[pos + 1:pos + 1 + n_w]
    m = pl.program_id(1)
    valid = m < nv_ref[0]

    @pl.when(jnp.logical_and(valid, first_ref[m] == 1))
    def _cast():
        for w_ref, wb_ref in zip(w_refs, wb_refs):
            def body(c, carry, w_ref=w_ref, wb_ref=wb_ref):
                r = pl.multiple_of(c * CAST_ROWS, CAST_ROWS)
                wb_ref[pl.ds(r, CAST_ROWS), :] = w_ref[pl.ds(r, CAST_ROWS), :].astype(BF16)
                return carry
            lax.fori_loop(0, w_ref.shape[0] // CAST_ROWS, body, 0)

    @pl.when(valid)
    def _compute():
        x = x_ref[...]
        if mode == "gated":
            a = jnp.dot(x, wb_refs[0][...], preferred_element_type=F32)
            b = jnp.dot(x, wb_refs[1][...], preferred_element_type=F32)
            o_ref[...] = (_silu(a) * b).astype(o_ref.dtype)
        else:
            acc = jnp.dot(x, wb_refs[0][...], preferred_element_type=F32)
            if mode == "resid":
                acc = res_ref[...] + acc
            o_ref[...] = acc.astype(o_ref.dtype)

    @pl.when(jnp.logical_not(valid))
    def _zero():
        o_ref[...] = jnp.zeros_like(o_ref)


def _grouped_matmul(x, ws, tile_grp, tile_first, n_valid, *, tm, tn, mode, out_dtype, res=None, name,
                    weight_buffers=2):
    m, k = x.shape
    n = ws[0].shape[2]
    assert m % tm == 0 and n % tn == 0 and k % CAST_ROWS == 0
    n_w = len(ws)

    def row_of(mi, nv):
        return jnp.minimum(mi, nv[0] - 1)

    in_specs = [pl.BlockSpec((tm, k), lambda ni, mi, g, f, nv: (row_of(mi, nv), 0))]
    for _ in ws:
        in_specs.append(pl.BlockSpec((None, k, tn), lambda ni, mi, g, f, nv: (g[row_of(mi, nv)], 0, ni),
                                     pipeline_mode=pl.Buffered(weight_buffers)))
    args = [x] + list(ws)
    if mode == "resid":
        in_specs.append(pl.BlockSpec((tm, tn), lambda ni, mi, g, f, nv: (row_of(mi, nv), ni)))
        args.append(res)
    grid_spec = pltpu.PrefetchScalarGridSpec(
        num_scalar_prefetch=3,
        grid=(n // tn, m // tm),
        in_specs=in_specs,
        out_specs=pl.BlockSpec((tm, tn), lambda ni, mi, g, f, nv: (mi, ni)),
        scratch_shapes=[pltpu.VMEM((k, tn), BF16) for _ in ws],
    )
    return pl.pallas_call(
        functools.partial(_mm_body, n_w=n_w, mode=mode),
        grid_spec=grid_spec,
        out_shape=jax.ShapeDtypeStruct((m, n), out_dtype),
        compiler_params=_cparams(("arbitrary", "arbitrary")),
        name=name,
    )(tile_grp, tile_first, n_valid, *args)


def _dense_matmul(x, w, layer, *, tm, tn, mode, out_dtype, res=None, w2=None, name):
    tiles = x.shape[0] // tm
    grp = jnp.full((tiles,), layer, I32)
    first = jnp.zeros((tiles,), I32).at[0].set(1)
    nv = jnp.full((1,), tiles, I32)
    ws = [w] if w2 is None else [w, w2]
    return _grouped_matmul(x, ws, grp, first, nv, tm=tm, tn=tn, mode=mode, out_dtype=out_dtype,
                           res=res, name=name)


_SLOT, _ISSUED, _DONE, _NEXT_GRP, _NEXT_COL, _HAS_NEXT = range(6)


def _stream_body(grp_ref, first_ref, nextg_ref, wrap_ref, nv_ref, x_ref, *refs, n_w, mode):
    w_hbm = refs[:n_w]
    o_ref, wb_ref, stage_ref, st_ref, sem_ref = refs[n_w:n_w + 5]
    n = pl.program_id(0)
    m = pl.program_id(1)
    k, tn = stage_ref.shape
    valid = m < nv_ref[0]

    def copy(j):
        col = pl.multiple_of(st_ref[_NEXT_COL] * tn, LANES)
        return pltpu.make_async_copy(w_hbm[j].at[st_ref[_NEXT_GRP], :, pl.ds(col, tn)], stage_ref, sem_ref.at[0])

    def cast(slot, j):
        for r in range(0, k, CAST_ROWS):
            wb_ref[slot, j, r:r + CAST_ROWS, :] = stage_ref[r:r + CAST_ROWS, :].astype(BF16)

    def compute(slot):
        x = x_ref[...]
        if mode == "gated":
            a = jnp.dot(x, wb_ref[slot, 0], preferred_element_type=F32)
            b = jnp.dot(x, wb_ref[slot, 1], preferred_element_type=F32)
            o_ref[...] = (_silu(a) * b).astype(o_ref.dtype)
        else:
            o_ref[...] = jnp.dot(x, wb_ref[slot, 0], preferred_element_type=F32).astype(o_ref.dtype)

    @pl.when(jnp.logical_and(valid, first_ref[m] == 1))
    def _switch():
        @pl.when(jnp.logical_and(n == 0, m == 0))
        def _boot():
            st_ref[_SLOT] = 1
            st_ref[_ISSUED] = 0
            st_ref[_DONE] = 0
            st_ref[_NEXT_GRP] = grp_ref[0]
            st_ref[_NEXT_COL] = 0

        for j in range(n_w):
            @pl.when(st_ref[_DONE] <= j)
            def _finish(j=j):
                @pl.when(st_ref[_ISSUED] <= j)
                def _issue():
                    copy(j).start()
                copy(j).wait()
                cast(1 - st_ref[_SLOT], j)

        st_ref[_SLOT] = 1 - st_ref[_SLOT]
        st_ref[_ISSUED] = 0
        st_ref[_DONE] = 0
        st_ref[_NEXT_GRP] = nextg_ref[m]
        st_ref[_NEXT_COL] = n + wrap_ref[m]
        last_col = n == pl.num_programs(0) - 1
        st_ref[_HAS_NEXT] = jnp.where(jnp.logical_and(wrap_ref[m] == 1, last_col), 0, 1)

    slot = st_ref[_SLOT]
    issued = st_ref[_ISSUED]
    done = st_ref[_DONE]
    in_flight = done < issued
    can_start = jnp.logical_and(st_ref[_HAS_NEXT] == 1, issued == 0)

    @pl.when(jnp.logical_and(valid, jnp.logical_and(jnp.logical_not(in_flight), jnp.logical_not(can_start))))
    def _plain():
        compute(slot)

    @pl.when(jnp.logical_and(valid, jnp.logical_and(jnp.logical_not(in_flight), can_start)))
    def _start0():
        compute(slot)
        copy(0).start()
        st_ref[_ISSUED] = 1

    for j in range(n_w):
        @pl.when(jnp.logical_and(valid, jnp.logical_and(in_flight, done == j)))
        def _cast_step(j=j):
            copy(j).wait()
            cast(1 - slot, j)
            compute(slot)
            st_ref[_DONE] = j + 1
            if j + 1 < n_w:
                copy(j + 1).start()
                st_ref[_ISSUED] = j + 2

    @pl.when(jnp.logical_not(valid))
    def _zero():
        o_ref[...] = jnp.zeros_like(o_ref)


def _stream_matmul(x, ws, tile_grp, tile_first, tile_next, tile_wrap, n_valid, *, tm, tn, mode, out_dtype, name):
    m, k = x.shape
    n = ws[0].shape[2]
    assert m % tm == 0 and n % tn == 0 and k % CAST_ROWS == 0 and tn % LANES == 0
    n_w = len(ws)
    grid_spec = pltpu.PrefetchScalarGridSpec(
        num_scalar_prefetch=5,
        grid=(n // tn, m // tm),
        in_specs=[pl.BlockSpec((tm, k), lambda ni, mi, g, f, nx, wr, nv: (jnp.minimum(mi, nv[0] - 1), 0))]
                 + [pl.BlockSpec(memory_space=pl.ANY)] * n_w,
        out_specs=pl.BlockSpec((tm, tn), lambda ni, mi, g, f, nx, wr, nv: (mi, ni)),
        scratch_shapes=[pltpu.VMEM((2, n_w, k, tn), BF16), pltpu.VMEM((k, tn), F32),
                        pltpu.SMEM((8,), I32), pltpu.SemaphoreType.DMA((1,))],
    )
    return pl.pallas_call(
        functools.partial(_stream_body, n_w=n_w, mode=mode),
        grid_spec=grid_spec,
        out_shape=jax.ShapeDtypeStruct((m, n), out_dtype),
        compiler_params=_cparams(("arbitrary", "arbitrary"), vmem=60 * 1024 * 1024),
        name=name,
    )(tile_grp, tile_first, tile_next, tile_wrap, n_valid, x, *ws)


def _mix_prompt_body(z_ref, caw_ref, cbw_ref, cbb_ref, lbg_ref, lbb_ref, lcg_ref, lcb_ref, spw_ref,
                     spbt_ref, pw_ref, ps_ref, y_ref, na_ref, nb_ref, nd_ref,
                     qa_ext, gb_ext, pd_ext, vn_buf, wm_ref, gsh_ref, *, rows, width, ka, kb, nbuf_d):
    i = pl.program_id(1)
    last = pl.num_programs(1) - 1
    w = width
    gw = w // len(POOL_WINDOWS)
    chunk = wm_ref.shape[1]
    cg = wm_ref.shape[0]
    cgw = w // cg
    sub = 32

    @pl.when(i == 0)
    def _init():
        zeros = jnp.zeros((HIST, w), F32)
        qa_ext[0:HIST, :] = zeros
        gb_ext[0:HIST, :] = zeros
        pd_ext[0:HIST, :] = zeros
        tri = lax.broadcasted_iota(I32, (chunk, chunk), 0) >= lax.broadcasted_iota(I32, (chunk, chunk), 1)
        for g in range(cg):
            wm_ref[g] = jnp.where(tri, spw_ref[g], 0.0).astype(BF16)

    def col(k):
        return z_ref[:, k * w:(k + 1) * w].astype(F32)

    qa_ext[HIST:HIST + rows, :] = col(0) * col(1)
    gb_ext[HIST:HIST + rows, :] = col(3) * jax.nn.sigmoid(col(4))
    pd_ext[HIST:HIST + rows, :] = col(7)
    vn_buf[...] = _layernorm(col(6), lcg_ref[...], lcb_ref[...]).astype(BF16)

    conv_a = None
    for j in range(ka):
        term = caw_ref[j:j + 1, :] * qa_ext[HIST - (ka - 1) + j:HIST - (ka - 1) + j + rows, :]
        conv_a = term if conv_a is None else conv_a + term
    y_ref[:, 0:w] = (col(2) * conv_a).astype(y_ref.dtype)

    for f in range(1, SUBLANES):
        gsh_ref[f - 1, SUBLANES:HIST + rows, :] = gb_ext[SUBLANES - f:HIST + rows - f, :]
    for r0 in range(0, rows, sub):
        acc = None
        for j in range(kb):
            back = kb - 1 - j
            f, s = back % SUBLANES, HIST + r0 - (back - back % SUBLANES)
            src = gb_ext[s:s + sub, :] if f == 0 else gsh_ref[f - 1, s:s + sub, :]
            term = cbw_ref[j:j + 1, :] * src
            acc = term if acc is None else acc + term
        yb = _silu(_layernorm(acc + cbb_ref[...], lbg_ref[...], lbb_ref[...]))
        y_ref[r0:r0 + sub, w:2 * w] = yb.astype(y_ref.dtype)

    for c0 in range(0, rows, chunk):
        for g in range(cg):
            lanes = slice(g * cgw, (g + 1) * cgw)
            s = jnp.dot(wm_ref[g], vn_buf[c0:c0 + chunk, lanes], preferred_element_type=F32)
            s = s + spbt_ref[:, g:g + 1]
            u = z_ref[c0:c0 + chunk, 5 * w + g * cgw:5 * w + (g + 1) * cgw].astype(F32)
            y_ref[c0:c0 + chunk, 2 * w + g * cgw:2 * w + (g + 1) * cgw] = (u * s).astype(y_ref.dtype)

    pos = i * rows + lax.broadcasted_iota(I32, (rows, 1), 0)
    for gi, win in enumerate(POOL_WINDOWS):
        lanes = slice(gi * gw, (gi + 1) * gw)
        tot = None
        for k in range(win):
            term = pd_ext[HIST - k:HIST - k + rows, lanes]
            tot = term if tot is None else tot + term
        cnt = jnp.minimum(pos + 1, win).astype(F32)
        d = tot / cnt - pd_ext[HIST:HIST + rows, lanes]
        yd = jnp.dot(d.astype(BF16), pw_ref[gi].astype(BF16), preferred_element_type=F32)
        y_ref[:, 3 * w + gi * gw:3 * w + (gi + 1) * gw] = (yd * ps_ref[:, lanes]).astype(y_ref.dtype)

    @pl.when(i == last)
    def _state():
        na_ref[...] = qa_ext[HIST + rows - (ka - 1):HIST + rows, :]
        nb_ref[...] = gb_ext[HIST + rows - (kb - 1):HIST + rows, :]
        nd_ref[...] = pd_ext[HIST + rows - nbuf_d:HIST + rows, :]

    qa_ext[0:HIST, :] = qa_ext[rows:rows + HIST, :]
    gb_ext[0:HIST, :] = gb_ext[rows:rows + HIST, :]
    pd_ext[0:HIST, :] = pd_ext[rows:rows + HIST, :]


def _mix_prompt(z, lw, *, batch, seq, rows):
    w = lw["conv_a_w"].shape[1]
    ka = lw["conv_a_w"].shape[0]
    kb = lw["conv_b_w"].shape[0]
    nbuf_d = max(POOL_WINDOWS) - 1
    cg, chunk = lw["sp_w"].shape[0], lw["sp_w"].shape[1]
    assert seq % rows == 0 and rows % chunk == 0 and kb - 1 <= HIST and nbuf_d <= HIST
    nt = seq // rows
    full2 = lambda a: pl.BlockSpec(a.shape, lambda b, i: (0, 0))
    full3 = lambda a: pl.BlockSpec(a.shape, lambda b, i: (0, 0, 0))
    vec = lambda a: a.reshape(1, w)
    small = [lw["conv_a_w"], lw["conv_b_w"], vec(lw["conv_b_b"]), vec(lw["ln_b_g"]), vec(lw["ln_b_b"]),
             vec(lw["ln_c_g"]), vec(lw["ln_c_b"])]
    spbt = lw["sp_b"].T
    in_specs = ([pl.BlockSpec((rows, 8 * w), lambda b, i: (b * nt + i, 0))]
                + [full2(a) for a in small]
                + [full3(lw["sp_w"]), full2(spbt), full3(lw["pool_w"]), full2(vec(lw["pool_scale"]))])
    out_shapes = (jax.ShapeDtypeStruct((batch * seq, 4 * w), BF16),
                  jax.ShapeDtypeStruct((batch, ka - 1, w), F32),
                  jax.ShapeDtypeStruct((batch, kb - 1, w), F32),
                  jax.ShapeDtypeStruct((batch, nbuf_d, w), F32))
    out_specs = (pl.BlockSpec((rows, 4 * w), lambda b, i: (b * nt + i, 0)),
                 pl.BlockSpec((None, ka - 1, w), lambda b, i: (b, 0, 0)),
                 pl.BlockSpec((None, kb - 1, w), lambda b, i: (b, 0, 0)),
                 pl.BlockSpec((None, nbuf_d, w), lambda b, i: (b, 0, 0)))
    scratch = [pltpu.VMEM((HIST + rows, w), F32), pltpu.VMEM((HIST + rows, w), F32),
               pltpu.VMEM((HIST + rows, w), F32), pltpu.VMEM((rows, w), BF16),
               pltpu.VMEM((cg, chunk, chunk), BF16),
               pltpu.VMEM((SUBLANES - 1, HIST + rows, w), F32)]
    return pl.pallas_call(
        functools.partial(_mix_prompt_body, rows=rows, width=w, ka=ka, kb=kb, nbuf_d=nbuf_d),
        grid=(batch, nt),
        in_specs=in_specs,
        out_specs=out_specs,
        out_shape=out_shapes,
        scratch_shapes=scratch,
        compiler_params=_cparams(("arbitrary", "arbitrary")),
        name="mix_prompt",
    )(z, *small, lw["sp_w"], spbt, lw["pool_w"], vec(lw["pool_scale"]))


def _mix_sample_body(spw_ref, spb_ref, z_ref, sa_ref, sb_ref, sd_ref, caw_ref, cbw_ref, cbb_ref,
                     lbg_ref, lbb_ref, lcg_ref, lcb_ref, pw_ref, ps_ref,
                     y_ref, na_ref, nb_ref, nd_ref, cv_ref, *, nseq, steps, width, ka, kb, nbuf_d, cg):
    w = width
    gw = w // len(POOL_WINDOWS)
    cgw = w // cg
    lane = lax.broadcasted_iota(I32, (1, w), 1)

    def col(t, k):
        return z_ref[t * nseq:(t + 1) * nseq, k * w:(k + 1) * w].astype(F32)

    def put(t, k, val):
        y_ref[t * nseq:(t + 1) * nseq, k * w:(k + 1) * w] = val.astype(y_ref.dtype)

    ext = [sa_ref[k] for k in range(ka - 1)] + [col(t, 0) * col(t, 1) for t in range(steps)]
    for t in range(steps):
        conv = None
        for j in range(ka):
            term = caw_ref[j:j + 1, :] * ext[t + j]
            conv = term if conv is None else conv + term
        put(t, 0, col(t, 2) * conv)
    for k in range(ka - 1):
        na_ref[k] = ext[steps + k]

    for t in range(steps):
        nb_ref[kb - 1 - steps + t] = col(t, 3) * jax.nn.sigmoid(col(t, 4))
    for k in range(kb - 1 - steps):
        nb_ref[k] = sb_ref[k + steps]

    def ext_b(k):
        return sb_ref[k] if k < kb - 1 else nb_ref[k - steps]

    for t in range(steps):
        acc = None
        for j in range(kb):
            term = cbw_ref[j:j + 1, :] * ext_b(t + j)
            acc = term if acc is None else acc + term
        put(t, 1, _silu(_layernorm(acc + cbb_ref[...], lbg_ref[...], lbb_ref[...])))

    def lane_groups(vals):
        out = jnp.full((1, w), vals[cg - 1], F32)
        for g in range(cg - 2, -1, -1):
            out = jnp.where(lane < (g + 1) * cgw, vals[g], out)
        return out

    for t in range(steps):
        cv_ref[t] = _layernorm(col(t, 6), lcg_ref[...], lcb_ref[...])
    for t in range(steps):
        s = lane_groups([spb_ref[g, t] for g in range(cg)])
        for sp in range(t + 1):
            s = s + lane_groups([spw_ref[g, t, sp] for g in range(cg)]) * cv_ref[sp]
        put(t, 2, col(t, 5) * s)

    for t in range(steps):
        nd_ref[nbuf_d - steps + t] = col(t, 7)
    for k in range(nbuf_d - steps):
        nd_ref[k] = sd_ref[k + steps]

    def ext_d(k):
        return sd_ref[k] if k < nbuf_d else nd_ref[k - steps]

    for t in range(steps):
        for gi, win in enumerate(POOL_WINDOWS):
            lanes = slice(gi * gw, (gi + 1) * gw)
            tot = None
            for k in range(win):
                term = ext_d(nbuf_d + t - k)[:, lanes]
                tot = term if tot is None else tot + term
            cnt = float(min(PAST_LEN + t + 1, win))
            d = tot / cnt - ext_d(nbuf_d + t)[:, lanes]
            yd = jnp.dot(d.astype(BF16), pw_ref[gi].astype(BF16), preferred_element_type=F32)
            y_ref[t * nseq:(t + 1) * nseq, 3 * w + gi * gw:3 * w + (gi + 1) * gw] = (
                yd * ps_ref[:, lanes]).astype(y_ref.dtype)


def _mix_sample(z, sa, sb, sd, lw, *, nseq, steps, m_prompt):
    w = lw["conv_a_w"].shape[1]
    ka = lw["conv_a_w"].shape[0]
    kb = lw["conv_b_w"].shape[0]
    nbuf_d = max(POOL_WINDOWS) - 1
    cg = lw["sp_w"].shape[0]
    ms = nseq * steps
    assert m_prompt % ms == 0 and steps <= nbuf_d and steps <= kb - 1
    rb = m_prompt // ms
    spw_small = lw["sp_w"][:, :steps, :steps]
    spb_small = lw["sp_b"][:, :steps]
    vec = lambda a: a.reshape(1, w)
    smem = pl.BlockSpec(memory_space=pltpu.SMEM)
    full2 = lambda a: pl.BlockSpec(a.shape, lambda i: (0, 0))
    full3 = lambda a: pl.BlockSpec(a.shape, lambda i: (0, 0, 0))
    small = [lw["conv_a_w"], lw["conv_b_w"], vec(lw["conv_b_b"]), vec(lw["ln_b_g"]), vec(lw["ln_b_b"]),
             vec(lw["ln_c_g"]), vec(lw["ln_c_b"])]
    in_specs = ([smem, smem,
                 pl.BlockSpec((ms, 8 * w), lambda i: (rb, 0)),
                 full3(sa), full3(sb), full3(sd)]
                + [full2(a) for a in small]
                + [full3(lw["pool_w"]), full2(vec(lw["pool_scale"]))])
    out_shapes = (jax.ShapeDtypeStruct((ms, 4 * w), BF16),
                  jax.ShapeDtypeStruct(sa.shape, F32), jax.ShapeDtypeStruct(sb.shape, F32),
                  jax.ShapeDtypeStruct(sd.shape, F32), jax.ShapeDtypeStruct((steps, nseq, w), F32))
    out_specs = (pl.BlockSpec((ms, 4 * w), lambda i: (0, 0)),
                 full3(sa), full3(sb), full3(sd),
                 pl.BlockSpec((steps, nseq, w), lambda i: (0, 0, 0)))
    return pl.pallas_call(
        functools.partial(_mix_sample_body, nseq=nseq, steps=steps, width=w, ka=ka, kb=kb,
                          nbuf_d=nbuf_d, cg=cg),
        grid=(1,),
        in_specs=in_specs,
        out_specs=out_specs,
        out_shape=out_shapes,
        compiler_params=_cparams(("arbitrary",)),
        name="mix_sample",
    )(spw_small, spb_small, z, sa, sb, sd, *small, lw["pool_w"], vec(lw["pool_scale"]))


def _merge_out_body(yp_ref, ys_ref, g0_ref, g1_ref, g2_ref, g3_ref, x_ref, gn_ref, wa_hbm, wb_hbm, wc_hbm, wd_hbm,
                    wo_hbm, *refs, width, prompt_tiles, layer, with_h):
    n_out = 2 if with_h else 1
    x1_ref = refs[0]
    wbr_ref, wout_ref, stage_ref, sem_ref = refs[n_out:n_out + 4]
    i = pl.program_id(0)
    rows = stage_ref.shape[1]
    d = wout_ref.shape[0]

    @pl.when(i == 0)
    def _load_weights():
        chunks = []
        for k, w_hbm in enumerate((wa_hbm, wb_hbm, wc_hbm, wd_hbm)):
            for r in range(0, width, rows):
                chunks.append((w_hbm.at[layer, pl.ds(r, rows), :], wbr_ref.at[k, pl.ds(r, rows), :]))
        for r in range(0, d, rows):
            chunks.append((wo_hbm.at[layer, pl.ds(r, rows), :], wout_ref.at[pl.ds(r, rows), :]))

        def fetch(c):
            return pltpu.make_async_copy(chunks[c][0], stage_ref.at[c % 2], sem_ref.at[c % 2])

        fetch(0).start()
        for c in range(len(chunks)):
            if c + 1 < len(chunks):
                fetch(c + 1).start()
            fetch(c).wait()
            chunks[c][1][...] = stage_ref[c % 2].astype(BF16)

    def run(y_ref):
        acc = None
        for k, g_ref in enumerate((g0_ref, g1_ref, g2_ref, g3_ref)):
            br = jnp.dot(y_ref[:, k * width:(k + 1) * width], wbr_ref[k], preferred_element_type=F32)
            term = jax.nn.sigmoid(g_ref[...].astype(F32)) * br
            acc = term if acc is None else acc + term
        x1 = x_ref[...] + jnp.dot(acc.astype(BF16), wout_ref[...], preferred_element_type=F32)
        x1_ref[...] = x1
        if with_h:
            refs[1][...] = _rms(x1, gn_ref[...]).astype(BF16)

    @pl.when(i < prompt_tiles)
    def _prompt():
        run(yp_ref)

    @pl.when(i >= prompt_tiles)
    def _sample():
        run(ys_ref)


def _merge_out(y_p, y_s, z, x, w_brs, w_out, layer, g, *, tm, with_h):
    mp, yw = y_p.shape
    ms = y_s.shape[0]
    assert mp % tm == 0 and ms % tm == 0
    tp = mp // tm
    w = yw // 4
    d = w_out.shape[2]
    assert (2 * yw) % d == 0 and w % STAGE_ROWS == 0 and d % STAGE_ROWS == 0
    goff = (2 * yw) // d
    gate_spec = lambda k: pl.BlockSpec((tm, d), lambda i, k=k: (i, goff + k))
    row = pl.BlockSpec((tm, d), lambda i: (i, 0))
    hbm = pl.BlockSpec(memory_space=pl.ANY)
    out_shape = [jax.ShapeDtypeStruct((mp + ms, d), F32)]
    out_specs = [row]
    if with_h:
        out_shape.append(jax.ShapeDtypeStruct((mp + ms, d), BF16))
        out_specs.append(row)
    return pl.pallas_call(
        functools.partial(_merge_out_body, width=w, prompt_tiles=tp, layer=layer, with_h=with_h),
        grid=((mp + ms) // tm,),
        in_specs=[pl.BlockSpec((tm, yw), lambda i: (jnp.minimum(i, tp - 1), 0)),
                  pl.BlockSpec((tm, yw), lambda i: (jnp.maximum(i - tp, 0), 0))]
                 + [gate_spec(k) for k in range(4)]
                 + [row, pl.BlockSpec((1, d), lambda i: (0, 0))] + [hbm] * 5,
        out_specs=out_specs,
        out_shape=out_shape,
        scratch_shapes=[pltpu.VMEM((4, w, d), BF16), pltpu.VMEM((d, d), BF16),
                        pltpu.VMEM((2, STAGE_ROWS, d), F32), pltpu.SemaphoreType.DMA((2,))],
        compiler_params=_cparams(("arbitrary",)),
        name="merge_out",
    )(y_p, y_s, z, z, z, z, x, g.reshape(1, d), *w_brs, w_out)


def _route_body(x_ref, g_ref, rwt_ref, e_ref, r_ref, wt_ref, cnt_ref, carry_ref, tri_ref):
    i = pl.program_id(0)
    ne = rwt_ref.shape[0]
    tm = x_ref.shape[0]

    @pl.when(i == 0)
    def _init():
        carry_ref[...] = jnp.zeros_like(carry_ref)
        upper = lax.broadcasted_iota(I32, (tm, tm), 0) <= lax.broadcasted_iota(I32, (tm, tm), 1)
        tri_ref[...] = jnp.where(upper, 1.0, 0.0).astype(BF16)

    h = _rms(x_ref[...], g_ref[...])
    h_hi = h.astype(BF16)
    h_lo = (h - h_hi.astype(F32)).astype(BF16)
    rw = rwt_ref[...]
    rw_hi = rw.astype(BF16)
    rw_lo = (rw - rw_hi.astype(F32)).astype(BF16)
    dn = (((1,), (1,)), ((), ()))
    dg = lambda a, b: lax.dot_general(a, b, dn, preferred_element_type=F32)
    logits = dg(rw_hi, h_hi) + dg(rw_hi, h_lo) + dg(rw_lo, h_hi)
    ex = jnp.exp(logits - jnp.max(logits, axis=0, keepdims=True))
    p = ex / jnp.sum(ex, axis=0, keepdims=True)
    eid = lax.broadcasted_iota(I32, (ne, tm), 0)
    m1 = jnp.max(p, axis=0, keepdims=True)
    i1 = jnp.min(jnp.where(p == m1, eid, ne), axis=0, keepdims=True)
    oh1 = eid == i1
    p2 = jnp.where(oh1, -1.0, p)
    m2 = jnp.max(p2, axis=0, keepdims=True)
    i2 = jnp.min(jnp.where(p2 == m2, eid, ne), axis=0, keepdims=True)
    oh2 = eid == i2
    den = m1 + m2
    sel = jnp.logical_or(oh1, oh2)
    sel_f = jnp.where(sel, 1.0, 0.0)
    cum = jnp.dot(sel_f.astype(BF16), tri_ref[...], preferred_element_type=F32) + carry_ref[...]
    rank1 = jnp.sum(jnp.where(oh1, cum, 0.0), axis=0, keepdims=True) - 1.0
    rank2 = jnp.sum(jnp.where(oh2, cum, 0.0), axis=0, keepdims=True) - 1.0
    carry_ref[...] = carry_ref[...] + jnp.sum(sel_f, axis=1, keepdims=True)
    e_ref[0:1, :] = i1
    e_ref[1:2, :] = i2
    r_ref[0:1, :] = rank1.astype(I32)
    r_ref[1:2, :] = rank2.astype(I32)
    wt_ref[0:1, :] = m1 / den
    wt_ref[1:2, :] = m2 / den

    @pl.when(i == pl.num_programs(0) - 1)
    def _counts():
        cnt_ref[...] = jnp.broadcast_to(carry_ref[...], cnt_ref.shape).astype(I32)


def _route(x1, g, router_w, *, tm):
    m, d = x1.shape
    ne = router_w.shape[1]
    tok = lambda dt: jax.ShapeDtypeStruct((TOP_K, m), dt)
    tok_spec = pl.BlockSpec((TOP_K, tm), lambda i: (0, i))
    return pl.pallas_call(
        _route_body,
        grid=(m // tm,),
        in_specs=[pl.BlockSpec((tm, d), lambda i: (i, 0)),
                  pl.BlockSpec((1, d), lambda i: (0, 0)),
                  pl.BlockSpec((ne, d), lambda i: (0, 0))],
        out_specs=(tok_spec, tok_spec, tok_spec, pl.BlockSpec((ne, LANES), lambda i: (0, 0))),
        out_shape=(tok(I32), tok(I32), tok(F32), jax.ShapeDtypeStruct((ne, LANES), I32)),
        scratch_shapes=[pltpu.VMEM((ne, 1), F32), pltpu.VMEM((tm, tm), BF16)],
        compiler_params=_cparams(("arbitrary",)),
        name="route",
    )(x1, g.reshape(1, d), router_w.T)


def _plan_body(e_ref, r_ref, cnt_ref, pos_ref, tgrp_ref, tfirst_ref, misc_ref, *, tile_shift):
    ne = cnt_ref.shape[0]
    tile = 1 << tile_shift
    sub = lax.broadcasted_iota(I32, (ne, LANES), 0)
    lane = lax.broadcasted_iota(I32, (ne, LANES), 1)

    def as_row(col):
        return jnp.sum(jnp.where(sub == lane, col, 0), axis=0, keepdims=True)

    cnt = cnt_ref[:, 0:1]
    padded = lax.shift_left(lax.shift_right_logical(cnt + (tile - 1), tile_shift), tile_shift)
    gstart = jnp.sum(jnp.where(lane < sub, as_row(padded), 0), axis=1, keepdims=True)
    gend = gstart + padded
    m = e_ref.shape[1]
    eid = lax.broadcasted_iota(I32, (ne, m), 0)
    for k in range(TOP_K):
        base = jnp.sum(jnp.where(eid == e_ref[k:k + 1, :], gstart, 0), axis=0, keepdims=True)
        pos_ref[k:k + 1, :] = base + r_ref[k:k + 1, :]
    start = lax.broadcasted_iota(I32, (1, LANES), 1) * tile
    grp = jnp.sum((gend <= start).astype(I32), axis=0, keepdims=True)
    tgrp_ref[...] = jnp.minimum(grp, ne - 1)
    tfirst_ref[...] = jnp.sum(jnp.logical_and(gstart == start, padded > 0).astype(I32), axis=0, keepdims=True)
    n_tiles = lax.shift_right_logical(jnp.max(gend, axis=0, keepdims=True), tile_shift)
    misc = jnp.where(sub == 0, n_tiles, 0)
    misc = jnp.where(sub == 1, as_row(gstart + cnt), misc)
    misc = jnp.where(sub == 2, as_row(gend), misc)
    lane_f = lane.astype(F32)
    nonempty = jnp.logical_and(as_row(padded) > 0, lane < ne)
    later = jnp.min(jnp.where(jnp.logical_and(nonempty, lane > sub), lane_f, float(ne)), axis=1, keepdims=True)
    firstg = jnp.min(jnp.where(nonempty, lane_f, float(ne)), axis=1, keepdims=True)
    nxt = jnp.where(later < ne, later, firstg).astype(I32)
    wraps = (later >= ne).astype(I32)
    mine = sub == jnp.minimum(grp, ne - 1)
    misc = jnp.where(sub == 3, jnp.sum(jnp.where(mine, nxt, 0), axis=0, keepdims=True), misc)
    misc = jnp.where(sub == 4, jnp.sum(jnp.where(mine, wraps, 0), axis=0, keepdims=True), misc)
    misc_ref[...] = misc


def _plan(e, r, cnt, *, tile):
    tile_shift = tile.bit_length() - 1
    assert 1 << tile_shift == tile
    ne = cnt.shape[0]
    m = e.shape[1]
    whole = lambda a: pl.BlockSpec(a.shape, lambda i: (0, 0))
    row = jax.ShapeDtypeStruct((1, LANES), I32)
    out_shape = (jax.ShapeDtypeStruct((TOP_K, m), I32), row, row, jax.ShapeDtypeStruct((ne, LANES), I32))
    return pl.pallas_call(
        functools.partial(_plan_body, tile_shift=tile_shift),
        grid=(1,),
        in_specs=[whole(e), whole(r), whole(cnt)],
        out_specs=tuple(pl.BlockSpec(s.shape, lambda i: (0, 0)) for s in out_shape),
        out_shape=out_shape,
        compiler_params=_cparams(("arbitrary",)),
        name="plan",
    )(e, r, cnt)


def _dispatch_body(pos0_ref, pos1_ref, nv_ref, fill_ref, end_ref, x_hbm, g_ref, o_ref, src_ref, buf_ref, sem_ref):
    j = pl.program_id(0)
    tm = o_ref.shape[0]
    n_tok = pos0_ref.shape[0]
    nv = nv_ref[0]

    def start_tile(tile, slot):
        def body(r, carry):
            tok = src_ref[tile * tm + r]
            pltpu.make_async_copy(x_hbm.at[pl.ds(tok, 1), :], buf_ref.at[slot, pl.ds(r, 1), :],
                                  sem_ref.at[slot]).start()
            return carry
        lax.fori_loop(0, tm, body, 0, unroll=ISSUE_UNROLL)

    @pl.when(j == 0)
    def _prologue():
        for g in range(fill_ref.shape[0]):
            def fill(p, carry):
                src_ref[p] = 0
                return carry
            lax.fori_loop(fill_ref[g], end_ref[g], fill, 0)

        def inv(t, carry):
            src_ref[pos0_ref[t]] = t
            src_ref[pos1_ref[t]] = t
            return carry
        lax.fori_loop(0, n_tok, inv, 0, unroll=ISSUE_UNROLL)
        start_tile(0, 0)

    slot = lax.rem(j, 2)

    @pl.when(j < nv)
    def _work():
        @pl.when(j + 1 < nv)
        def _prefetch():
            start_tile(j + 1, 1 - slot)

        pltpu.make_async_copy(x_hbm.at[pl.ds(0, tm), :], buf_ref.at[slot], sem_ref.at[slot]).wait()
        o_ref[...] = _rms(buf_ref[slot], g_ref[...]).astype(o_ref.dtype)

    @pl.when(j >= nv)
    def _skip():
        o_ref[...] = jnp.zeros_like(o_ref)


def _dispatch(x1, g, pos0, pos1, n_tiles, fill_row, end_row, *, tm, max_tiles):
    m, d = x1.shape
    grid_spec = pltpu.PrefetchScalarGridSpec(
        num_scalar_prefetch=5,
        grid=(max_tiles,),
        in_specs=[pl.BlockSpec(memory_space=pl.ANY),
                  pl.BlockSpec((1, d), lambda j, *_: (0, 0))],
        out_specs=pl.BlockSpec((tm, d), lambda j, *_: (j, 0)),
        scratch_shapes=[pltpu.SMEM((max_tiles * tm,), I32),
                        pltpu.VMEM((2, tm, d), F32),
                        pltpu.SemaphoreType.DMA((2,))],
    )
    return pl.pallas_call(
        _dispatch_body,
        grid_spec=grid_spec,
        out_shape=jax.ShapeDtypeStruct((max_tiles * tm, d), BF16),
        compiler_params=_cparams(("arbitrary",)),
        name="dispatch",
    )(pos0, pos1, n_tiles, fill_row, end_row, x1, g.reshape(1, d))


def _combine_body(pos0_ref, pos1_ref, ys_hbm, x_ref, wt_ref, g_ref, xn_ref, h_ref, buf_ref, sem_ref):
    i = pl.program_id(0)
    n = pl.num_programs(0)
    tm = x_ref.shape[0]

    def start_tile(tile, slot):
        def body(r, carry):
            t = tile * tm + r
            pltpu.make_async_copy(ys_hbm.at[pl.ds(pos0_ref[t], 1), :], buf_ref.at[slot, 0, pl.ds(r, 1), :],
                                  sem_ref.at[slot, 0]).start()
            pltpu.make_async_copy(ys_hbm.at[pl.ds(pos1_ref[t], 1), :], buf_ref.at[slot, 1, pl.ds(r, 1), :],
                                  sem_ref.at[slot, 1]).start()
            return carry
        lax.fori_loop(0, tm, body, 0, unroll=ISSUE_UNROLL)

    @pl.when(i == 0)
    def _prologue():
        start_tile(0, 0)

    slot = lax.rem(i, 2)

    @pl.when(i + 1 < n)
    def _prefetch():
        start_tile(i + 1, 1 - slot)

    for k in range(TOP_K):
        pltpu.make_async_copy(ys_hbm.at[pl.ds(0, tm), :], buf_ref.at[slot, k], sem_ref.at[slot, k]).wait()
    xn = x_ref[...] + wt_ref[:, 0:1] * buf_ref[slot, 0] + wt_ref[:, 1:2] * buf_ref[slot, 1]
    xn_ref[...] = xn
    h_ref[...] = _rms(xn, g_ref[...]).astype(h_ref.dtype)


def _combine(ys, x1, wt_cols, g, pos0, pos1, *, tm, h_dtype):
    m, d = x1.shape
    row = lambda: pl.BlockSpec((tm, d), lambda i, *_: (i, 0))
    grid_spec = pltpu.PrefetchScalarGridSpec(
        num_scalar_prefetch=2,
        grid=(m // tm,),
        in_specs=[pl.BlockSpec(memory_space=pl.ANY), row(),
                  pl.BlockSpec((tm, TOP_K), lambda i, *_: (i, 0)),
                  pl.BlockSpec((1, d), lambda i, *_: (0, 0))],
        out_specs=(row(), row()),
        scratch_shapes=[pltpu.VMEM((2, TOP_K, tm, d), F32), pltpu.SemaphoreType.DMA((2, TOP_K))],
    )
    return pl.pallas_call(
        _combine_body,
        grid_spec=grid_spec,
        out_shape=(jax.ShapeDtypeStruct((m, d), F32), jax.ShapeDtypeStruct((m, d), h_dtype)),
        compiler_params=_cparams(("arbitrary",)),
        name="combine",
    )(pos0, pos1, ys, x1, wt_cols, g.reshape(1, d))


def _pick(total, prefs):
    for p in prefs:
        if total % p == 0:
            return p
    raise ValueError(f"no tile for {total} in {prefs}")


def _row_tile(m, cap):
    best = None
    for t in range(16, cap + 1, 16):
        if m % t == 0:
            best = t
    assert best is not None
    return best


def kernel(x_prompt, x_sample, state_conv_a, state_conv_b, state_pool, norm_mix_g, w_in, conv_a_w, w_br_a,
           conv_b_w, conv_b_b, ln_b_g, ln_b_b, w_br_b, ln_c_g, ln_c_b, sp_w, sp_b, w_br_c, pool_w,
           pool_scale, w_br_d, w_out, norm_ffn_g, ffn_w1, ffn_w3, ffn_w2, router_w, moe_w1, moe_w3, moe_w2,
           final_norm_g):
    batch, seq, d = x_prompt.shape
    nseq, steps, _ = x_sample.shape
    depth = w_in.shape[0]
    n_exp = router_w.shape[2]
    mp, ms = batch * seq, nseq * steps
    m = mp + ms

    x = jnp.concatenate([x_prompt.reshape(mp, d), jnp.swapaxes(x_sample, 0, 1).reshape(ms, d)], axis=0)
    st_a = jnp.swapaxes(state_conv_a, 1, 2)
    st_b = jnp.swapaxes(state_conv_b, 1, 2)
    st_d = jnp.swapaxes(state_pool, 1, 2)

    tm_big = _row_tile(m, 1088)
    tm_mid = _row_tile(m, 544)
    tm_small = _row_tile(m, 256)
    tm_merge = min(_row_tile(mp, 256), _row_tile(ms, 256))
    assert mp % tm_merge == 0 and ms % tm_merge == 0
    tm_moe = 512
    max_tiles = -(-(m * TOP_K) // tm_moe) + n_exp
    assert max_tiles <= LANES
    mix_rows = _pick(seq, (256, 128))
    moe_w1f = moe_w1.reshape((-1,) + moe_w1.shape[2:])
    moe_w3f = moe_w3.reshape((-1,) + moe_w3.shape[2:])
    moe_w2f = moe_w2.reshape((-1,) + moe_w2.shape[2:])
    dffe = moe_w1.shape[3]
    dff = ffn_w1.shape[2]

    h = _rmsnorm(x, norm_mix_g[0], BF16, tm_mid)
    outs_p = ([], [], [])
    outs_s = ([], [], [], [])
    y_final = None
    for i in range(depth):
        lw = dict(conv_a_w=conv_a_w[i], conv_b_w=conv_b_w[i], conv_b_b=conv_b_b[i], ln_b_g=ln_b_g[i],
                  ln_b_b=ln_b_b[i], ln_c_g=ln_c_g[i], ln_c_b=ln_c_b[i], sp_w=sp_w[i], sp_b=sp_b[i],
                  pool_w=pool_w[i], pool_scale=pool_scale[i])
        z = _dense_matmul(h, w_in, i, tm=tm_big, tn=_pick(w_in.shape[2], (1024, 512, 256, 128)),
                          mode="plain", out_dtype=BF16, name="inproj")
        y_p, na_p, nb_p, nd_p = _mix_prompt(z, lw, batch=batch, seq=seq, rows=mix_rows)
        y_s, na_s, nb_s, nd_s, cv_s = _mix_sample(z, st_a[i], st_b[i], st_d[i], lw, nseq=nseq, steps=steps,
                                                  m_prompt=mp)
        for acc, val in zip(outs_p, (na_p, nb_p, nd_p)):
            acc.append(val)
        for acc, val in zip(outs_s, (na_s, nb_s, nd_s, cv_s)):
            acc.append(val)
        merge_out = functools.partial(_merge_out, y_p, y_s, z, x, (w_br_a, w_br_b, w_br_c, w_br_d), w_out, i,
                                      norm_ffn_g[i], tm=tm_merge)
        last = i == depth - 1
        g_next = final_norm_g if last else norm_mix_g[i + 1]
        h_dtype = F32 if last else BF16
        j = i // 2
        if i % 2 == 0:
            x1, h2 = merge_out(with_h=True)
            a = _dense_matmul(h2, ffn_w1, j, w2=ffn_w3, tm=tm_big, tn=_pick(dff, (512, 256, 128)),
                              mode="gated", out_dtype=BF16, name="ffn_up")
            x = _dense_matmul(a, ffn_w2, j, tm=tm_mid, tn=_pick(d, (512, 256, 128)), mode="resid",
                              out_dtype=F32, res=x1, name="ffn_down")
            h = _rmsnorm(x, g_next, h_dtype, tm_mid)
        else:
            (x1,) = merge_out(with_h=False)
            e, r, wt, cnt = _route(x1, norm_ffn_g[i], router_w[j], tm=_pick(m, (512, 256, 128)))
            pos, tgrp, tfirst, misc = _plan(e, r, cnt, tile=tm_moe)
            n_tiles = misc[0, 0:1]
            hs = _dispatch(x1, norm_ffn_g[i], pos[0], pos[1], n_tiles, misc[1, :n_exp], misc[2, :n_exp],
                           tm=tm_moe, max_tiles=max_tiles)
            grp = tgrp[0, :max_tiles] + j * n_exp
            first = tfirst[0, :max_tiles]
            nextg = misc[3, :max_tiles] + j * n_exp
            wrap = misc[4, :max_tiles]
            a = _stream_matmul(hs, [moe_w1f, moe_w3f], grp, first, nextg, wrap, n_tiles, tm=tm_moe,
                               tn=_pick(dffe, (1408, 1024, 512, 256, 128)), mode="gated", out_dtype=BF16,
                               name="moe_up")
            ys = _stream_matmul(a, [moe_w2f], grp, first, nextg, wrap, n_tiles, tm=tm_moe,
                                tn=_pick(d, (1024, 512, 256, 128)), mode="plain", out_dtype=F32, name="moe_down")
            x, h = _combine(ys, x1, wt.T, g_next, pos[0], pos[1], tm=tm_small, h_dtype=h_dtype)
        if last:
            y_final = h

    y_prompt = y_final[:mp].reshape(batch, seq, d)
    y_sample = jnp.swapaxes(y_final[mp:].reshape(steps, nseq, d), 0, 1)
    stack_s = lambda vals: jnp.swapaxes(jnp.stack(vals), 1, 2)
    return (y_prompt, y_sample, jnp.stack(outs_p[0]), jnp.stack(outs_p[1]), jnp.stack(outs_p[2]),
            stack_s(outs_s[0]), stack_s(outs_s[1]), stack_s(outs_s[2]), stack_s(outs_s[3]))
```

```python
import functools

import jax
import jax.numpy as jnp
from jax import lax
from jax.experimental import pallas as pl
from jax.experimental.pallas import tpu as pltpu

F32 = jnp.float32
BF16 = jnp.bfloat16
I32 = jnp.int32

EPS = 1e-6
PAST_LEN = 16384
POOL_WINDOWS = (2, 4, 8, 16)
TOP_K = 2
LANES = 128
SUBLANES = 8
HIST = 32
ISSUE_UNROLL = 8
VMEM_LIMIT = 56 * 1024 * 1024


def _cparams(sem, vmem=VMEM_LIMIT):
    return pltpu.CompilerParams(dimension_semantics=sem, vmem_limit_bytes=vmem)


def _rms(x, g):
    return x * lax.rsqrt(jnp.mean(x * x, axis=-1, keepdims=True) + EPS) * g


def _layernorm(x, g, b):
    mu = jnp.mean(x, axis=-1, keepdims=True)
    xc = x - mu
    var = jnp.mean(xc * xc, axis=-1, keepdims=True)
    return xc * lax.rsqrt(var + EPS) * g + b


def _silu(x):
    return x * jax.nn.sigmoid(x)


def _rmsnorm_body(x_ref, g_ref, o_ref):
    o_ref[...] = _rms(x_ref[...], g_ref[...]).astype(o_ref.dtype)


def _rmsnorm(x, g, out_dtype, tm):
    m, d = x.shape
    return pl.pallas_call(
        _rmsnorm_body,
        grid=(m // tm,),
        in_specs=[pl.BlockSpec((tm, d), lambda i: (i, 0)),
                  pl.BlockSpec((1, d), lambda i: (0, 0))],
        out_specs=pl.BlockSpec((tm, d), lambda i: (i, 0)),
        out_shape=jax.ShapeDtypeStruct((m, d), out_dtype),
        compiler_params=_cparams(("arbitrary",)),
        name="rmsnorm",
    )(x, g.reshape(1, d))


CAST_ROWS = 256
STAGE_ROWS = 256


def _mm_body(grp_ref, first_ref, nv_ref, *refs, n_w, mode):
    del grp_ref
    x_ref = refs[0]
    w_refs = refs[1:1 + n_w]
    pos = 1 + n_w
    res_ref = None
    if mode == "resid":
        res_ref = refs[pos]
        pos += 1
    o_ref = refs[pos]
    wb_refs = refs[pos + 1:pos + 1 + n_w]
    m = pl.program_id(1)
    valid = m < nv_ref[0]

    @pl.when(jnp.logical_and(valid, first_ref[m] == 1))
    def _cast():
        for w_ref, wb_ref in zip(w_refs, wb_refs):
            def body(c, carry, w_ref=w_ref, wb_ref=wb_ref):
                r = pl.multiple_of(c * CAST_ROWS, CAST_ROWS)
                wb_ref[pl.ds(r, CAST_ROWS), :] = w_ref[pl.ds(r, CAST_ROWS), :].astype(BF16)
                return carry
            lax.fori_loop(0, w_ref.shape[0] // CAST_ROWS, body, 0)

    @pl.when(valid)
    def _compute():
        x = x_ref[...]
        if mode == "gated":
            a = jnp.dot(x, wb_refs[0][...], preferred_element_type=F32)
            b = jnp.dot(x, wb_refs[1][...], preferred_element_type=F32)
            o_ref[...] = (_silu(a) * b).astype(o_ref.dtype)
        else:
            acc = jnp.dot(x, wb_refs[0][...], preferred_element_type=F32)
            if mode == "resid":
                acc = res_ref[...] + acc
            o_ref[...] = acc.astype(o_ref.dtype)

    @pl.when(jnp.logical_not(valid))
    def _zero():
        o_ref[...] = jnp.zeros_like(o_ref)


def _grouped_matmul(x, ws, tile_grp, tile_first, n_valid, *, tm, tn, mode, out_dtype, res=None, name,
                    weight_buffers=2):
    m, k = x.shape
    n = ws[0].shape[2]
    assert m % tm == 0 and n % tn == 0 and k % CAST_ROWS == 0
    n_w = len(ws)

    def row_of(mi, nv):
        return jnp.minimum(mi, nv[0] - 1)

    in_specs = [pl.BlockSpec((tm, k), lambda ni, mi, g, f, nv: (row_of(mi, nv), 0))]
    for _ in ws:
        in_specs.append(pl.BlockSpec((None, k, tn), lambda ni, mi, g, f, nv: (g[row_of(mi, nv)], 0, ni),
                                     pipeline_mode=pl.Buffered(weight_buffers)))
    args = [x] + list(ws)
    if mode == "resid":
        in_specs.append(pl.BlockSpec((tm, tn), lambda ni, mi, g, f, nv: (row_of(mi, nv), ni)))
        args.append(res)
    grid_spec = pltpu.PrefetchScalarGridSpec(
        num_scalar_prefetch=3,
        grid=(n // tn, m // tm),
        in_specs=in_specs,
        out_specs=pl.BlockSpec((tm, tn), lambda ni, mi, g, f, nv: (mi, ni)),
        scratch_shapes=[pltpu.VMEM((k, tn), BF16) for _ in ws],
    )
    return pl.pallas_call(
        functools.partial(_mm_body, n_w=n_w, mode=mode),
        grid_spec=grid_spec,
        out_shape=jax.ShapeDtypeStruct((m, n), out_dtype),
        compiler_params=_cparams(("arbitrary", "arbitrary")),
        name=name,
    )(tile_grp, tile_first, n_valid, *args)


def _dense_matmul(x, w, layer, *, tm, tn, mode, out_dtype, res=None, w2=None, name):
    tiles = x.shape[0] // tm
    grp = jnp.full((tiles,), layer, I32)
    first = jnp.zeros((tiles,), I32).at[0].set(1)
    nv = jnp.full((1,), tiles, I32)
    ws = [w] if w2 is None else [w, w2]
    return _grouped_matmul(x, ws, grp, first, nv, tm=tm, tn=tn, mode=mode, out_dtype=out_dtype,
                           res=res, name=name)


_SLOT, _ISSUED, _DONE, _NEXT_GRP, _NEXT_COL, _HAS_NEXT = range(6)


def _stream_body(grp_ref, first_ref, nextg_ref, wrap_ref, nv_ref, x_ref, *refs, n_w, mode):
    w_hbm = refs[:n_w]
    o_ref, wb_ref, stage_ref, st_ref, sem_ref = refs[n_w:n_w + 5]
    n = pl.program_id(0)
    m = pl.program_id(1)
    k, tn = stage_ref.shape
    valid = m < nv_ref[0]

    def copy(j):
        col = pl.multiple_of(st_ref[_NEXT_COL] * tn, LANES)
        return pltpu.make_async_copy(w_hbm[j].at[st_ref[_NEXT_GRP], :, pl.ds(col, tn)], stage_ref, sem_ref.at[0])

    def cast(slot, j):
        for r in range(0, k, CAST_ROWS):
            wb_ref[slot, j, r:r + CAST_ROWS, :] = stage_ref[r:r + CAST_ROWS, :].astype(BF16)

    def compute(slot):
        x = x_ref[...]
        if mode == "gated":
            a = jnp.dot(x, wb_ref[slot, 0], preferred_element_type=F32)
            b = jnp.dot(x, wb_ref[slot, 1], preferred_element_type=F32)
            o_ref[...] = (_silu(a) * b).astype(o_ref.dtype)
        else:
            o_ref[...] = jnp.dot(x, wb_ref[slot, 0], preferred_element_type=F32).astype(o_ref.dtype)

    @pl.when(jnp.logical_and(valid, first_ref[m] == 1))
    def _switch():
        @pl.when(jnp.logical_and(n == 0, m == 0))
        def _boot():
            st_ref[_SLOT] = 1
            st_ref[_ISSUED] = 0
            st_ref[_DONE] = 0
            st_ref[_NEXT_GRP] = grp_ref[0]
            st_ref[_NEXT_COL] = 0

        for j in range(n_w):
            @pl.when(st_ref[_DONE] <= j)
            def _finish(j=j):
                @pl.when(st_ref[_ISSUED] <= j)
                def _issue():
                    copy(j).start()
                copy(j).wait()
                cast(1 - st_ref[_SLOT], j)

        st_ref[_SLOT] = 1 - st_ref[_SLOT]
        st_ref[_ISSUED] = 0
        st_ref[_DONE] = 0
        st_ref[_NEXT_GRP] = nextg_ref[m]
        st_ref[_NEXT_COL] = n + wrap_ref[m]
        last_col = n == pl.num_programs(0) - 1
        st_ref[_HAS_NEXT] = jnp.where(jnp.logical_and(wrap_ref[m] == 1, last_col), 0, 1)

    slot = st_ref[_SLOT]
    issued = st_ref[_ISSUED]
    done = st_ref[_DONE]
    in_flight = done < issued
    can_start = jnp.logical_and(st_ref[_HAS_NEXT] == 1, issued == 0)

    @pl.when(jnp.logical_and(valid, jnp.logical_and(jnp.logical_not(in_flight), jnp.logical_not(can_start))))
    def _plain():
        compute(slot)

    @pl.when(jnp.logical_and(valid, jnp.logical_and(jnp.logical_not(in_flight), can_start)))
    def _start0():
        compute(slot)
        copy(0).start()
        st_ref[_ISSUED] = 1

    for j in range(n_w):
        @pl.when(jnp.logical_and(valid, jnp.logical_and(in_flight, done == j)))
        def _cast_step(j=j):
            copy(j).wait()
            cast(1 - slot, j)
            compute(slot)
            st_ref[_DONE] = j + 1
            if j + 1 < n_w:
                copy(j + 1).start()
                st_ref[_ISSUED] = j + 2

    @pl.when(jnp.logical_not(valid))
    def _zero():
        o_ref[...] = jnp.zeros_like(o_ref)


def _stream_matmul(x, ws, tile_grp, tile_first, tile_next, tile_wrap, n_valid, *, tm, tn, mode, out_dtype, name):
    m, k = x.shape
    n = ws[0].shape[2]
    assert m % tm == 0 and n % tn == 0 and k % CAST_ROWS == 0 and tn % LANES == 0
    n_w = len(ws)
    grid_spec = pltpu.PrefetchScalarGridSpec(
        num_scalar_prefetch=5,
        grid=(n // tn, m // tm),
        in_specs=[pl.BlockSpec((tm, k), lambda ni, mi, g, f, nx, wr, nv: (jnp.minimum(mi, nv[0] - 1), 0))]
                 + [pl.BlockSpec(memory_space=pl.ANY)] * n_w,
        out_specs=pl.BlockSpec((tm, tn), lambda ni, mi, g, f, nx, wr, nv: (mi, ni)),
        scratch_shapes=[pltpu.VMEM((2, n_w, k, tn), BF16), pltpu.VMEM((k, tn), F32),
                        pltpu.SMEM((8,), I32), pltpu.SemaphoreType.DMA((1,))],
    )
    return pl.pallas_call(
        functools.partial(_stream_body, n_w=n_w, mode=mode),
        grid_spec=grid_spec,
        out_shape=jax.ShapeDtypeStruct((m, n), out_dtype),
        compiler_params=_cparams(("arbitrary", "arbitrary"), vmem=60 * 1024 * 1024),
        name=name,
    )(tile_grp, tile_first, tile_next, tile_wrap, n_valid, x, *ws)


def _mix_prompt_body(z_ref, caw_ref, cbw_ref, cbb_ref, lbg_ref, lbb_ref, lcg_ref, lcb_ref, spw_ref,
                     spbt_ref, pw_ref, ps_ref, y_ref, na_ref, nb_ref, nd_ref,
                     qa_ext, gb_ext, pd_ext, vn_buf, wm_ref, gsh_ref, *, rows, width, ka, kb, nbuf_d):
    i = pl.program_id(1)
    last = pl.num_programs(1) - 1
    w = width
    gw = w // len(POOL_WINDOWS)
    chunk = wm_ref.shape[1]
    cg = wm_ref.shape[0]
    cgw = w // cg
    sub = 32

    @pl.when(i == 0)
    def _init():
        zeros = jnp.zeros((HIST, w), F32)
        qa_ext[0:HIST, :] = zeros
        gb_ext[0:HIST, :] = zeros
        pd_ext[0:HIST, :] = zeros
        tri = lax.broadcasted_iota(I32, (chunk, chunk), 0) >= lax.broadcasted_iota(I32, (chunk, chunk), 1)
        for g in range(cg):
            wm_ref[g] = jnp.where(tri, spw_ref[g], 0.0).astype(BF16)

    def col(k):
        return z_ref[:, k * w:(k + 1) * w].astype(F32)

    qa_ext[HIST:HIST + rows, :] = col(0) * col(1)
    gb_ext[HIST:HIST + rows, :] = col(3) * jax.nn.sigmoid(col(4))
    pd_ext[HIST:HIST + rows, :] = col(7)
    vn_buf[...] = _layernorm(col(6), lcg_ref[...], lcb_ref[...]).astype(BF16)

    conv_a = None
    for j in range(ka):
        term = caw_ref[j:j + 1, :] * qa_ext[HIST - (ka - 1) + j:HIST - (ka - 1) + j + rows, :]
        conv_a = term if conv_a is None else conv_a + term
    y_ref[:, 0:w] = (col(2) * conv_a).astype(y_ref.dtype)

    for f in range(1, SUBLANES):
        gsh_ref[f - 1, SUBLANES:HIST + rows, :] = gb_ext[SUBLANES - f:HIST + rows - f, :]
    for r0 in range(0, rows, sub):
        acc = None
        for j in range(kb):
            back = kb - 1 - j
            f, s = back % SUBLANES, HIST + r0 - (back - back % SUBLANES)
            src = gb_ext[s:s + sub, :] if f == 0 else gsh_ref[f - 1, s:s + sub, :]
            term = cbw_ref[j:j + 1, :] * src
            acc = term if acc is None else acc + term
        yb = _silu(_layernorm(acc + cbb_ref[...], lbg_ref[...], lbb_ref[...]))
        y_ref[r0:r0 + sub, w:2 * w] = yb.astype(y_ref.dtype)

    for c0 in range(0, rows, chunk):
        for g in range(cg):
            lanes = slice(g * cgw, (g + 1) * cgw)
            s = jnp.dot(wm_ref[g], vn_buf[c0:c0 + chunk, lanes], preferred_element_type=F32)
            s = s + spbt_ref[:, g:g + 1]
            u = z_ref[c0:c0 + chunk, 5 * w + g * cgw:5 * w + (g + 1) * cgw].astype(F32)
            y_ref[c0:c0 + chunk, 2 * w + g * cgw:2 * w + (g + 1) * cgw] = (u * s).astype(y_ref.dtype)

    pos = i * rows + lax.broadcasted_iota(I32, (rows, 1), 0)
    for gi, win in enumerate(POOL_WINDOWS):
        lanes = slice(gi * gw, (gi + 1) * gw)
        tot = None
        for k in range(win):
            term = pd_ext[HIST - k:HIST - k + rows, lanes]
            tot = term if tot is None else tot + term
        cnt = jnp.minimum(pos + 1, win).astype(F32)
        d = tot / cnt - pd_ext[HIST:HIST + rows, lanes]
        yd = jnp.dot(d.astype(BF16), pw_ref[gi].astype(BF16), preferred_element_type=F32)
        y_ref[:, 3 * w + gi * gw:3 * w + (gi + 1) * gw] = (yd * ps_ref[:, lanes]).astype(y_ref.dtype)

    @pl.when(i == last)
    def _state():
        na_ref[...] = qa_ext[HIST + rows - (ka - 1):HIST + rows, :]
        nb_ref[...] = gb_ext[HIST + rows - (kb - 1):HIST + rows, :]
        nd_ref[...] = pd_ext[HIST + rows - nbuf_d:HIST + rows, :]

    qa_ext[0:HIST, :] = qa_ext[rows:rows + HIST, :]
    gb_ext[0:HIST, :] = gb_ext[rows:rows + HIST, :]
    pd_ext[0:HIST, :] = pd_ext[rows:rows + HIST, :]


def _mix_prompt(z, lw, *, batch, seq, rows):
    w = lw["conv_a_w"].shape[1]
    ka = lw["conv_a_w"].shape[0]
    kb = lw["conv_b_w"].shape[0]
    nbuf_d = max(POOL_WINDOWS) - 1
    cg, chunk = lw["sp_w"].shape[0], lw["sp_w"].shape[1]
    assert seq % rows == 0 and rows % chunk == 0 and kb - 1 <= HIST and nbuf_d <= HIST
    nt = seq // rows
    full2 = lambda a: pl.BlockSpec(a.shape, lambda b, i: (0, 0))
    full3 = lambda a: pl.BlockSpec(a.shape, lambda b, i: (0, 0, 0))
    vec = lambda a: a.reshape(1, w)
    small = [lw["conv_a_w"], lw["conv_b_w"], vec(lw["conv_b_b"]), vec(lw["ln_b_g"]), vec(lw["ln_b_b"]),
             vec(lw["ln_c_g"]), vec(lw["ln_c_b"])]
    spbt = lw["sp_b"].T
    in_specs = ([pl.BlockSpec((rows, 8 * w), lambda b, i: (b * nt + i, 0))]
                + [full2(a) for a in small]
                + [full3(lw["sp_w"]), full2(spbt), full3(lw["pool_w"]), full2(vec(lw["pool_scale"]))])
    out_shapes = (jax.ShapeDtypeStruct((batch * seq, 4 * w), BF16),
                  jax.ShapeDtypeStruct((batch, ka - 1, w), F32),
                  jax.ShapeDtypeStruct((batch, kb - 1, w), F32),
                  jax.ShapeDtypeStruct((batch, nbuf_d, w), F32))
    out_specs = (pl.BlockSpec((rows, 4 * w), lambda b, i: (b * nt + i, 0)),
                 pl.BlockSpec((None, ka - 1, w), lambda b, i: (b, 0, 0)),
                 pl.BlockSpec((None, kb - 1, w), lambda b, i: (b, 0, 0)),
                 pl.BlockSpec((None, nbuf_d, w), lambda b, i: (b, 0, 0)))
    scratch = [pltpu.VMEM((HIST + rows, w), F32), pltpu.VMEM((HIST + rows, w), F32),
               pltpu.VMEM((HIST + rows, w), F32), pltpu.VMEM((rows, w), BF16),
               pltpu.VMEM((cg, chunk, chunk), BF16),
               pltpu.VMEM((SUBLANES - 1, HIST + rows, w), F32)]
    return pl.pallas_call(
        functools.partial(_mix_prompt_body, rows=rows, width=w, ka=ka, kb=kb, nbuf_d=nbuf_d),
        grid=(batch, nt),
        in_specs=in_specs,
        out_specs=out_specs,
        out_shape=out_shapes,
        scratch_shapes=scratch,
        compiler_params=_cparams(("arbitrary", "arbitrary")),
        name="mix_prompt",
    )(z, *small, lw["sp_w"], spbt, lw["pool_w"], vec(lw["pool_scale"]))


def _mix_sample_body(spw_ref, spb_ref, z_ref, sa_ref, sb_ref, sd_ref, caw_ref, cbw_ref, cbb_ref,
                     lbg_ref, lbb_ref, lcg_ref, lcb_ref, pw_ref, ps_ref,
                     y_ref, na_ref, nb_ref, nd_ref, cv_ref, *, nseq, steps, width, ka, kb, nbuf_d, cg):
    w = width
    gw = w // len(POOL_WINDOWS)
    cgw = w // cg
    lane = lax.broadcasted_iota(I32, (1, w), 1)

    def col(t, k):
        return z_ref[t * nseq:(t + 1) * nseq, k * w:(k + 1) * w].astype(F32)

    def put(t, k, val):
        y_ref[t * nseq:(t + 1) * nseq, k * w:(k + 1) * w] = val.astype(y_ref.dtype)

    ext = [sa_ref[k] for k in range(ka - 1)] + [col(t, 0) * col(t, 1) for t in range(steps)]
    for t in range(steps):
        conv = None
        for j in range(ka):
            term = caw_ref[j:j + 1, :] * ext[t + j]
            conv = term if conv is None else conv + term
        put(t, 0, col(t, 2) * conv)
    for k in range(ka - 1):
        na_ref[k] = ext[steps + k]

    for t in range(steps):
        nb_ref[kb - 1 - steps + t] = col(t, 3) * jax.nn.sigmoid(col(t, 4))
    for k in range(kb - 1 - steps):
        nb_ref[k] = sb_ref[k + steps]

    def ext_b(k):
        return sb_ref[k] if k < kb - 1 else nb_ref[k - steps]

    for t in range(steps):
        acc = None
        for j in range(kb):
            term = cbw_ref[j:j + 1, :] * ext_b(t + j)
            acc = term if acc is None else acc + term
        put(t, 1, _silu(_layernorm(acc + cbb_ref[...], lbg_ref[...], lbb_ref[...])))

    def lane_groups(vals):
        out = jnp.full((1, w), vals[cg - 1], F32)
        for g in range(cg - 2, -1, -1):
            out = jnp.where(lane < (g + 1) * cgw, vals[g], out)
        return out

    for t in range(steps):
        cv_ref[t] = _layernorm(col(t, 6), lcg_ref[...], lcb_ref[...])
    for t in range(steps):
        s = lane_groups([spb_ref[g, t] for g in range(cg)])
        for sp in range(t + 1):
            s = s + lane_groups([spw_ref[g, t, sp] for g in range(cg)]) * cv_ref[sp]
        put(t, 2, col(t, 5) * s)

    for t in range(steps):
        nd_ref[nbuf_d - steps + t] = col(t, 7)
    for k in range(nbuf_d - steps):
        nd_ref[k] = sd_ref[k + steps]

    def ext_d(k):
        return sd_ref[k] if k < nbuf_d else nd_ref[k - steps]

    for t in range(steps):
        for gi, win in enumerate(POOL_WINDOWS):
            lanes = slice(gi * gw, (gi + 1) * gw)
            tot = None
            for k in range(win):
                term = ext_d(nbuf_d + t - k)[:, lanes]
                tot = term if tot is None else tot + term
            cnt = float(min(PAST_LEN + t + 1, win))
            d = tot / cnt - ext_d(nbuf_d + t)[:, lanes]
            yd = jnp.dot(d.astype(BF16), pw_ref[gi].astype(BF16), preferred_element_type=F32)
            y_ref[t * nseq:(t + 1) * nseq, 3 * w + gi * gw:3 * w + (gi + 1) * gw] = (
                yd * ps_ref[:, lanes]).astype(y_ref.dtype)


def _mix_sample(z, sa, sb, sd, lw, *, nseq, steps, m_prompt):
    w = lw["conv_a_w"].shape[1]
    ka = lw["conv_a_w"].shape[0]
    kb = lw["conv_b_w"].shape[0]
    nbuf_d = max(POOL_WINDOWS) - 1
    cg = lw["sp_w"].shape[0]
    ms = nseq * steps
    assert m_prompt % ms == 0 and steps <= nbuf_d and steps <= kb - 1
    rb = m_prompt // ms
    spw_small = lw["sp_w"][:, :steps, :steps]
    spb_small = lw["sp_b"][:, :steps]
    vec = lambda a: a.reshape(1, w)
    smem = pl.BlockSpec(memory_space=pltpu.SMEM)
    full2 = lambda a: pl.BlockSpec(a.shape, lambda i: (0, 0))
    full3 = lambda a: pl.BlockSpec(a.shape, lambda i: (0, 0, 0))
    small = [lw["conv_a_w"], lw["conv_b_w"], vec(lw["conv_b_b"]), vec(lw["ln_b_g"]), vec(lw["ln_b_b"]),
             vec(lw["ln_c_g"]), vec(lw["ln_c_b"])]
    in_specs = ([smem, smem,
                 pl.BlockSpec((ms, 8 * w), lambda i: (rb, 0)),
                 full3(sa), full3(sb), full3(sd)]
                + [full2(a) for a in small]
                + [full3(lw["pool_w"]), full2(vec(lw["pool_scale"]))])
    out_shapes = (jax.ShapeDtypeStruct((ms, 4 * w), BF16),
                  jax.ShapeDtypeStruct(sa.shape, F32), jax.ShapeDtypeStruct(sb.shape, F32),
                  jax.ShapeDtypeStruct(sd.shape, F32), jax.ShapeDtypeStruct((steps, nseq, w), F32))
    out_specs = (pl.BlockSpec((ms, 4 * w), lambda i: (0, 0)),
                 full3(sa), full3(sb), full3(sd),
                 pl.BlockSpec((steps, nseq, w), lambda i: (0, 0, 0)))
    return pl.pallas_call(
        functools.partial(_mix_sample_body, nseq=nseq, steps=steps, width=w, ka=ka, kb=kb,
                          nbuf_d=nbuf_d, cg=cg),
        grid=(1,),
        in_specs=in_specs,
        out_specs=out_specs,
        out_shape=out_shapes,
        compiler_params=_cparams(("arbitrary",)),
        name="mix_sample",
    )(spw_small, spb_small, z, sa, sb, sd, *small, lw["pool_w"], vec(lw["pool_scale"]))


def _merge_out_body(yp_ref, ys_ref, g0_ref, g1_ref, g2_ref, g3_ref, x_ref, gn_ref, wa_hbm, wb_hbm, wc_hbm, wd_hbm,
                    wo_hbm, *refs, width, prompt_tiles, layer, with_h):
    n_out = 2 if with_h else 1
    x1_ref = refs[0]
    wbr_ref, wout_ref, stage_ref, sem_ref = refs[n_out:n_out + 4]
    i = pl.program_id(0)
    rows = stage_ref.shape[1]
    d = wout_ref.shape[0]

    @pl.when(i == 0)
    def _load_weights():
        chunks = []
        for k, w_hbm in enumerate((wa_hbm, wb_hbm, wc_hbm, wd_hbm)):
            for r in range(0, width, rows):
                chunks.append((w_hbm.at[layer, pl.ds(r, rows), :], wbr_ref.at[k, pl.ds(r, rows), :]))
        for r in range(0, d, rows):
            chunks.append((wo_hbm.at[layer, pl.ds(r, rows), :], wout_ref.at[pl.ds(r, rows), :]))

        def fetch(c):
            return pltpu.make_async_copy(chunks[c][0], stage_ref.at[c % 2], sem_ref.at[c % 2])

        fetch(0).start()
        for c in range(len(chunks)):
            if c + 1 < len(chunks):
                fetch(c + 1).start()
            fetch(c).wait()
            chunks[c][1][...] = stage_ref[c % 2].astype(BF16)

    def run(y_ref):
        acc = None
        for k, g_ref in enumerate((g0_ref, g1_ref, g2_ref, g3_ref)):
            br = jnp.dot(y_ref[:, k * width:(k + 1) * width], wbr_ref[k], preferred_element_type=F32)
            term = jax.nn.sigmoid(g_ref[...].astype(F32)) * br
            acc = term if acc is None else acc + term
        x1 = x_ref[...] + jnp.dot(acc.astype(BF16), wout_ref[...], preferred_element_type=F32)
        x1_ref[...] = x1
        if with_h:
            refs[1][...] = _rms(x1, gn_ref[...]).astype(BF16)

    @pl.when(i < prompt_tiles)
    def _prompt():
        run(yp_ref)

    @pl.when(i >= prompt_tiles)
    def _sample():
        run(ys_ref)


def _merge_out(y_p, y_s, z, x, w_brs, w_out, layer, g, *, tm, with_h):
    mp, yw = y_p.shape
    ms = y_s.shape[0]
    assert mp % tm == 0 and ms % tm == 0
    tp = mp // tm
    w = yw // 4
    d = w_out.shape[2]
    assert (2 * yw) % d == 0 and w % STAGE_ROWS == 0 and d % STAGE_ROWS == 0
    goff = (2 * yw) // d
    gate_spec = lambda k: pl.BlockSpec((tm, d), lambda i, k=k: (i, goff + k))
    row = pl.BlockSpec((tm, d), lambda i: (i, 0))
    hbm = pl.BlockSpec(memory_space=pl.ANY)
    out_shape = [jax.ShapeDtypeStruct((mp + ms, d), F32)]
    out_specs = [row]
    if with_h:
        out_shape.append(jax.ShapeDtypeStruct((mp + ms, d), BF16))
        out_specs.append(row)
    return pl.pallas_call(
        functools.partial(_merge_out_body, width=w, prompt_tiles=tp, layer=layer, with_h=with_h),
        grid=((mp + ms) // tm,),
        in_specs=[pl.BlockSpec((tm, yw), lambda i: (jnp.minimum(i, tp - 1), 0)),
                  pl.BlockSpec((tm, yw), lambda i: (jnp.maximum(i - tp, 0), 0))]
                 + [gate_spec(k) for k in range(4)]
                 + [row, pl.BlockSpec((1, d), lambda i: (0, 0))] + [hbm] * 5,
        out_specs=out_specs,
        out_shape=out_shape,
        scratch_shapes=[pltpu.VMEM((4, w, d), BF16), pltpu.VMEM((d, d), BF16),
                        pltpu.VMEM((2, STAGE_ROWS, d), F32), pltpu.SemaphoreType.DMA((2,))],
        compiler_params=_cparams(("arbitrary",)),
        name="merge_out",
    )(y_p, y_s, z, z, z, z, x, g.reshape(1, d), *w_brs, w_out)


def _route_body(x_ref, g_ref, rwt_ref, e_ref, r_ref, wt_ref, cnt_ref, carry_ref, tri_ref):
    i = pl.program_id(0)
    ne = rwt_ref.shape[0]
    tm = x_ref.shape[0]

    @pl.when(i == 0)
    def _init():
        carry_ref[...] = jnp.zeros_like(carry_ref)
        upper = lax.broadcasted_iota(I32, (tm, tm), 0) <= lax.broadcasted_iota(I32, (tm, tm), 1)
        tri_ref[...] = jnp.where(upper, 1.0, 0.0).astype(BF16)

    h = _rms(x_ref[...], g_ref[...])
    h_hi = h.astype(BF16)
    h_lo = (h - h_hi.astype(F32)).astype(BF16)
    rw = rwt_ref[...]
    rw_hi = rw.astype(BF16)
    rw_lo = (rw - rw_hi.astype(F32)).astype(BF16)
    dn = (((1,), (1,)), ((), ()))
    dg = lambda a, b: lax.dot_general(a, b, dn, preferred_element_type=F32)
    logits = dg(rw_hi, h_hi) + dg(rw_hi, h_lo) + dg(rw_lo, h_hi)
    ex = jnp.exp(logits - jnp.max(logits, axis=0, keepdims=True))
    p = ex / jnp.sum(ex, axis=0, keepdims=True)
    eid = lax.broadcasted_iota(I32, (ne, tm), 0)
    m1 = jnp.max(p, axis=0, keepdims=True)
    i1 = jnp.min(jnp.where(p == m1, eid, ne), axis=0, keepdims=True)
    oh1 = eid == i1
    p2 = jnp.where(oh1, -1.0, p)
    m2 = jnp.max(p2, axis=0, keepdims=True)
    i2 = jnp.min(jnp.where(p2 == m2, eid, ne), axis=0, keepdims=True)
    oh2 = eid == i2
    den = m1 + m2
    sel = jnp.logical_or(oh1, oh2)
    sel_f = jnp.where(sel, 1.0, 0.0)
    cum = jnp.dot(sel_f.astype(BF16), tri_ref[...], preferred_element_type=F32) + carry_ref[...]
    rank1 = jnp.sum(jnp.where(oh1, cum, 0.0), axis=0, keepdims=True) - 1.0
    rank2 = jnp.sum(jnp.where(oh2, cum, 0.0), axis=0, keepdims=True) - 1.0
    carry_ref[...] = carry_ref[...] + jnp.sum(sel_f, axis=1, keepdims=True)
    e_ref[0:1, :] = i1
    e_ref[1:2, :] = i2
    r_ref[0:1, :] = rank1.astype(I32)
    r_ref[1:2, :] = rank2.astype(I32)
    wt_ref[0:1, :] = m1 / den
    wt_ref[1:2, :] = m2 / den

    @pl.when(i == pl.num_programs(0) - 1)
    def _counts():
        cnt_ref[...] = jnp.broadcast_to(carry_ref[...], cnt_ref.shape).astype(I32)


def _route(x1, g, router_w, *, tm):
    m, d = x1.shape
    ne = router_w.shape[1]
    tok = lambda dt: jax.ShapeDtypeStruct((TOP_K, m), dt)
    tok_spec = pl.BlockSpec((TOP_K, tm), lambda i: (0, i))
    return pl.pallas_call(
        _route_body,
        grid=(m // tm,),
        in_specs=[pl.BlockSpec((tm, d), lambda i: (i, 0)),
                  pl.BlockSpec((1, d), lambda i: (0, 0)),
                  pl.BlockSpec((ne, d), lambda i: (0, 0))],
        out_specs=(tok_spec, tok_spec, tok_spec, pl.BlockSpec((ne, LANES), lambda i: (0, 0))),
        out_shape=(tok(I32), tok(I32), tok(F32), jax.ShapeDtypeStruct((ne, LANES), I32)),
        scratch_shapes=[pltpu.VMEM((ne, 1), F32), pltpu.VMEM((tm, tm), BF16)],
        compiler_params=_cparams(("arbitrary",)),
        name="route",
    )(x1, g.reshape(1, d), router_w.T)


def _plan_body(e_ref, r_ref, cnt_ref, pos_ref, tgrp_ref, tfirst_ref, misc_ref, *, tile_shift):
    ne = cnt_ref.shape[0]
    tile = 1 << tile_shift
    sub = lax.broadcasted_iota(I32, (ne, LANES), 0)
    lane = lax.broadcasted_iota(I32, (ne, LANES), 1)

    def as_row(col):
        return jnp.sum(jnp.where(sub == lane, col, 0), axis=0, keepdims=True)

    cnt = cnt_ref[:, 0:1]
    padded = lax.shift_left(lax.shift_right_logical(cnt + (tile - 1), tile_shift), tile_shift)
    gstart = jnp.sum(jnp.where(lane < sub, as_row(padded), 0), axis=1, keepdims=True)
    gend = gstart + padded
    m = e_ref.shape[1]
    eid = lax.broadcasted_iota(I32, (ne, m), 0)
    for k in range(TOP_K):
        base = jnp.sum(jnp.where(eid == e_ref[k:k + 1, :], gstart, 0), axis=0, keepdims=True)
        pos_ref[k:k + 1, :] = base + r_ref[k:k + 1, :]
    start = lax.broadcasted_iota(I32, (1, LANES), 1) * tile
    grp = jnp.sum((gend <= start).astype(I32), axis=0, keepdims=True)
    tgrp_ref[...] = jnp.minimum(grp, ne - 1)
    tfirst_ref[...] = jnp.sum(jnp.logical_and(gstart == start, padded > 0).astype(I32), axis=0, keepdims=True)
    n_tiles = lax.shift_right_logical(jnp.max(gend, axis=0, keepdims=True), tile_shift)
    misc = jnp.where(sub == 0, n_tiles, 0)
    misc = jnp.where(sub == 1, as_row(gstart + cnt), misc)
    misc = jnp.where(sub == 2, as_row(gend), misc)
    lane_f = lane.astype(F32)
    nonempty = jnp.logical_and(as_row(padded) > 0, lane < ne)
    later = jnp.min(jnp.where(jnp.logical_and(nonempty, lane > sub), lane_f, float(ne)), axis=1, keepdims=True)
    firstg = jnp.min(jnp.where(nonempty, lane_f, float(ne)), axis=1, keepdims=True)
    nxt = jnp.where(later < ne, later, firstg).astype(I32)
    wraps = (later >= ne).astype(I32)
    mine = sub == jnp.minimum(grp, ne - 1)
    misc = jnp.where(sub == 3, jnp.sum(jnp.where(mine, nxt, 0), axis=0, keepdims=True), misc)
    misc = jnp.where(sub == 4, jnp.sum(jnp.where(mine, wraps, 0), axis=0, keepdims=True), misc)
    misc_ref[...] = misc


def _plan(e, r, cnt, *, tile):
    tile_shift = tile.bit_length() - 1
    assert 1 << tile_shift == tile
    ne = cnt.shape[0]
    m = e.shape[1]
    whole = lambda a: pl.BlockSpec(a.shape, lambda i: (0, 0))
    row = jax.ShapeDtypeStruct((1, LANES), I32)
    out_shape = (jax.ShapeDtypeStruct((TOP_K, m), I32), row, row, jax.ShapeDtypeStruct((ne, LANES), I32))
    return pl.pallas_call(
        functools.partial(_plan_body, tile_shift=tile_shift),
        grid=(1,),
        in_specs=[whole(e), whole(r), whole(cnt)],
        out_specs=tuple(pl.BlockSpec(s.shape, lambda i: (0, 0)) for s in out_shape),
        out_shape=out_shape,
        compiler_params=_cparams(("arbitrary",)),
        name="plan",
    )(e, r, cnt)


def _dispatch_body(pos0_ref, pos1_ref, nv_ref, fill_ref, end_ref, x_hbm, g_ref, o_ref, src_ref, buf_ref, sem_ref):
    j = pl.program_id(0)
    tm = o_ref.shape[0]
    n_tok = pos0_ref.shape[0]
    nv = nv_ref[0]

    def start_tile(tile, slot):
        def body(r, carry):
            tok = src_ref[tile * tm + r]
            pltpu.make_async_copy(x_hbm.at[pl.ds(tok, 1), :], buf_ref.at[slot, pl.ds(r, 1), :],
                                  sem_ref.at[slot]).start()
            return carry
        lax.fori_loop(0, tm, body, 0, unroll=ISSUE_UNROLL)

    @pl.when(j == 0)
    def _prologue():
        for g in range(fill_ref.shape[0]):
            def fill(p, carry):
                src_ref[p] = 0
                return carry
            lax.fori_loop(fill_ref[g], end_ref[g], fill, 0)

        def inv(t, carry):
            src_ref[pos0_ref[t]] = t
            src_ref[pos1_ref[t]] = t
            return carry
        lax.fori_loop(0, n_tok, inv, 0, unroll=ISSUE_UNROLL)
        start_tile(0, 0)

    slot = lax.rem(j, 2)

    @pl.when(j < nv)
    def _work():
        @pl.when(j + 1 < nv)
        def _prefetch():
            start_tile(j + 1, 1 - slot)

        pltpu.make_async_copy(x_hbm.at[pl.ds(0, tm), :], buf_ref.at[slot], sem_ref.at[slot]).wait()
        o_ref[...] = _rms(buf_ref[slot], g_ref[...]).astype(o_ref.dtype)

    @pl.when(j >= nv)
    def _skip():
        o_ref[...] = jnp.zeros_like(o_ref)


def _dispatch(x1, g, pos0, pos1, n_tiles, fill_row, end_row, *, tm, max_tiles):
    m, d = x1.shape
    grid_spec = pltpu.PrefetchScalarGridSpec(
        num_scalar_prefetch=5,
        grid=(max_tiles,),
        in_specs=[pl.BlockSpec(memory_space=pl.ANY),
                  pl.BlockSpec((1, d), lambda j, *_: (0, 0))],
        out_specs=pl.BlockSpec((tm, d), lambda j, *_: (j, 0)),
        scratch_shapes=[pltpu.SMEM((max_tiles * tm,), I32),
                        pltpu.VMEM((2, tm, d), F32),
                        pltpu.SemaphoreType.DMA((2,))],
    )
    return pl.pallas_call(
        _dispatch_body,
        grid_spec=grid_spec,
        out_shape=jax.ShapeDtypeStruct((max_tiles * tm, d), BF16),
        compiler_params=_cparams(("arbitrary",)),
        name="dispatch",
    )(pos0, pos1, n_tiles, fill_row, end_row, x1, g.reshape(1, d))


def _combine_body(pos0_ref, pos1_ref, ys_hbm, x_ref, wt_ref, g_ref, xn_ref, h_ref, buf_ref, sem_ref):
    i = pl.program_id(0)
    n = pl.num_programs(0)
    tm = x_ref.shape[0]

    def start_tile(tile, slot):
        def body(r, carry):
            t = tile * tm + r
            pltpu.make_async_copy(ys_hbm.at[pl.ds(pos0_ref[t], 1), :], buf_ref.at[slot, 0, pl.ds(r, 1), :],
                                  sem_ref.at[slot, 0]).start()
            pltpu.make_async_copy(ys_hbm.at[pl.ds(pos1_ref[t], 1), :], buf_ref.at[slot, 1, pl.ds(r, 1), :],
                                  sem_ref.at[slot, 1]).start()
            return carry
        lax.fori_loop(0, tm, body, 0, unroll=ISSUE_UNROLL)

    @pl.when(i == 0)
    def _prologue():
        start_tile(0, 0)

    slot = lax.rem(i, 2)

    @pl.when(i + 1 < n)
    def _prefetch():
        start_tile(i + 1, 1 - slot)

    for k in range(TOP_K):
        pltpu.make_async_copy(ys_hbm.at[pl.ds(0, tm), :], buf_ref.at[slot, k], sem_ref.at[slot, k]).wait()
    xn = x_ref[...] + wt_ref[:, 0:1] * buf_ref[slot, 0] + wt_ref[:, 1:2] * buf_ref[slot, 1]
    xn_ref[...] = xn
    h_ref[...] = _rms(xn, g_ref[...]).astype(h_ref.dtype)


def _combine(ys, x1, wt_cols, g, pos0, pos1, *, tm, h_dtype):
    m, d = x1.shape
    row = lambda: pl.BlockSpec((tm, d), lambda i, *_: (i, 0))
    grid_spec = pltpu.PrefetchScalarGridSpec(
        num_scalar_prefetch=2,
        grid=(m // tm,),
        in_specs=[pl.BlockSpec(memory_space=pl.ANY), row(),
                  pl.BlockSpec((tm, TOP_K), lambda i, *_: (i, 0)),
                  pl.BlockSpec((1, d), lambda i, *_: (0, 0))],
        out_specs=(row(), row()),
        scratch_shapes=[pltpu.VMEM((2, TOP_K, tm, d), F32), pltpu.SemaphoreType.DMA((2, TOP_K))],
    )
    return pl.pallas_call(
        _combine_body,
        grid_spec=grid_spec,
        out_shape=(jax.ShapeDtypeStruct((m, d), F32), jax.ShapeDtypeStruct((m, d), h_dtype)),
        compiler_params=_cparams(("arbitrary",)),
        name="combine",
    )(pos0, pos1, ys, x1, wt_cols, g.reshape(1, d))


def _pick(total, prefs):
    for p in prefs:
        if total % p == 0:
            return p
    raise ValueError(f"no tile for {total} in {prefs}")


def _row_tile(m, cap):
    best = None
    for t in range(16, cap + 1, 16):
        if m % t == 0:
            best = t
    assert best is not None
    return best


def kernel(x_prompt, x_sample, state_conv_a, state_conv_b, state_pool, norm_mix_g, w_in, conv_a_w, w_br_a,
           conv_b_w, conv_b_b, ln_b_g, ln_b_b, w_br_b, ln_c_g, ln_c_b, sp_w, sp_b, w_br_c, pool_w,
           pool_scale, w_br_d, w_out, norm_ffn_g, ffn_w1, ffn_w3, ffn_w2, router_w, moe_w1, moe_w3, moe_w2,
           final_norm_g):
    batch, seq, d = x_prompt.shape
    nseq, steps, _ = x_sample.shape
    depth = w_in.shape[0]
    n_exp = router_w.shape[2]
    mp, ms = batch * seq, nseq * steps
    m = mp + ms

    x = jnp.concatenate([x_prompt.reshape(mp, d), jnp.swapaxes(x_sample, 0, 1).reshape(ms, d)], axis=0)
    st_a = jnp.swapaxes(state_conv_a, 1, 2)
    st_b = jnp.swapaxes(state_conv_b, 1, 2)
    st_d = jnp.swapaxes(state_pool, 1, 2)

    tm_big = _row_tile(m, 1088)
    tm_mid = _row_tile(m, 544)
    tm_small = _row_tile(m, 256)
    tm_merge = min(_row_tile(mp, 256), _row_tile(ms, 256))
    assert mp % tm_merge == 0 and ms % tm_merge == 0
    tm_moe = 512
    max_tiles = -(-(m * TOP_K) // tm_moe) + n_exp
    assert max_tiles <= LANES
    mix_rows = _pick(seq, (256, 128))
    moe_w1f = moe_w1.reshape((-1,) + moe_w1.shape[2:])
    moe_w3f = moe_w3.reshape((-1,) + moe_w3.shape[2:])
    moe_w2f = moe_w2.reshape((-1,) + moe_w2.shape[2:])
    dffe = moe_w1.shape[3]
    dff = ffn_w1.shape[2]

    h = _rmsnorm(x, norm_mix_g[0], BF16, tm_mid)
    outs_p = ([], [], [])
    outs_s = ([], [], [], [])
    y_final = None
    for i in range(depth):
        lw = dict(conv_a_w=conv_a_w[i], conv_b_w=conv_b_w[i], conv_b_b=conv_b_b[i], ln_b_g=ln_b_g[i],
                  ln_b_b=ln_b_b[i], ln_c_g=ln_c_g[i], ln_c_b=ln_c_b[i], sp_w=sp_w[i], sp_b=sp_b[i],
                  pool_w=pool_w[i], pool_scale=pool_scale[i])
        z = _dense_matmul(h, w_in, i, tm=tm_mid if i == 1 else tm_big,
                          tn=_pick(w_in.shape[2], (1024, 512, 256, 128)),
                          mode="plain", out_dtype=BF16, name="inproj")
        y_p, na_p, nb_p, nd_p = _mix_prompt(z, lw, batch=batch, seq=seq, rows=mix_rows)
        y_s, na_s, nb_s, nd_s, cv_s = _mix_sample(z, st_a[i], st_b[i], st_d[i], lw, nseq=nseq, steps=steps,
                                                  m_prompt=mp)
        for acc, val in zip(outs_p, (na_p, nb_p, nd_p)):
            acc.append(val)
        for acc, val in zip(outs_s, (na_s, nb_s, nd_s, cv_s)):
            acc.append(val)
        merge_out = functools.partial(_merge_out, y_p, y_s, z, x, (w_br_a, w_br_b, w_br_c, w_br_d), w_out, i,
                                      norm_ffn_g[i], tm=tm_merge)
        last = i == depth - 1
        g_next = final_norm_g if last else norm_mix_g[i + 1]
        h_dtype = F32 if last else BF16
        j = i // 2
        if i % 2 == 0:
            x1, h2 = merge_out(with_h=True)
            a = _dense_matmul(h2, ffn_w1, j, w2=ffn_w3, tm=tm_big, tn=_pick(dff, (512, 256, 128)),
                              mode="gated", out_dtype=BF16, name="ffn_up")
            x = _dense_matmul(a, ffn_w2, j, tm=tm_mid, tn=_pick(d, (512, 256, 128)), mode="resid",
                              out_dtype=F32, res=x1, name="ffn_down")
            h = _rmsnorm(x, g_next, h_dtype, tm_mid)
        else:
            (x1,) = merge_out(with_h=False)
            e, r, wt, cnt = _route(x1, norm_ffn_g[i], router_w[j], tm=_pick(m, (512, 256, 128)))
            big = 2 if j == 1 else 1
            tile_moe = tm_moe * big
            tiles_moe = -(-(m * TOP_K) // tile_moe) + n_exp
            pos, tgrp, tfirst, misc = _plan(e, r, cnt, tile=tile_moe)
            n_tiles = misc[0, 0:1]
            grp = tgrp[0, :tiles_moe] + j * n_exp
            first = tfirst[0, :tiles_moe]
            nextg = misc[3, :tiles_moe] + j * n_exp
            wrap = misc[4, :tiles_moe]
            rep = lambda v: jnp.repeat(v, big)
            first_s = (rep(first).reshape(-1, big) * (jnp.arange(big) == 0)).reshape(-1)
            n_tiles_s = n_tiles * big
            hs = _dispatch(x1, norm_ffn_g[i], pos[0], pos[1], n_tiles_s, misc[1, :n_exp], misc[2, :n_exp],
                           tm=tm_moe, max_tiles=tiles_moe * big)
            a = _stream_matmul(hs, [moe_w1f, moe_w3f], rep(grp), first_s, rep(nextg), rep(wrap), n_tiles_s,
                               tm=tm_moe, tn=_pick(dffe, (1408, 1024, 512, 256, 128)), mode="gated",
                               out_dtype=BF16, name="moe_up")
            ys = _stream_matmul(a, [moe_w2f], grp, first, nextg, wrap, n_tiles, tm=tile_moe,
                                tn=_pick(d, (1024, 512, 256, 128)), mode="plain", out_dtype=F32, name="moe_down")
            x, h = _combine(ys, x1, wt.T, g_next, pos[0], pos[1], tm=tm_small, h_dtype=h_dtype)
        if last:
            y_final = h

    y_prompt = y_final[:mp].reshape(batch, seq, d)
    y_sample = jnp.swapaxes(y_final[mp:].reshape(steps, nseq, d), 0, 1)
    stack_s = lambda vals: jnp.swapaxes(jnp.stack(vals), 1, 2)
    return (y_prompt, y_sample, jnp.stack(outs_p[0]), jnp.stack(outs_p[1]), jnp.stack(outs_p[2]),
            stack_s(outs_s[0]), stack_s(outs_s[1]), stack_s(outs_s[2]), stack_s(outs_s[3]))
```

```python
import functools

import jax
import jax.numpy as jnp
from jax import lax
from jax.experimental import pallas as pl
from jax.experimental.pallas import tpu as pltpu

F32 = jnp.float32
BF16 = jnp.bfloat16
I32 = jnp.int32

EPS = 1e-6
PAST_LEN = 16384
POOL_WINDOWS = (2, 4, 8, 16)
TOP_K = 2
LANES = 128
SUBLANES = 8
HIST = 32
ISSUE_UNROLL = 8
VMEM_LIMIT = 56 * 1024 * 1024


def _cparams(sem, vmem=VMEM_LIMIT):
    return pltpu.CompilerParams(dimension_semantics=sem, vmem_limit_bytes=vmem)


def _rms(x, g):
    return x * lax.rsqrt(jnp.mean(x * x, axis=-1, keepdims=True) + EPS) * g


def _layernorm(x, g, b):
    mu = jnp.mean(x, axis=-1, keepdims=True)
    xc = x - mu
    var = jnp.mean(xc * xc, axis=-1, keepdims=True)
    return xc * lax.rsqrt(var + EPS) * g + b


def _silu(x):
    return x * jax.nn.sigmoid(x)


def _rmsnorm_body(x_ref, g_ref, o_ref):
    o_ref[...] = _rms(x_ref[...], g_ref[...]).astype(o_ref.dtype)


def _rmsnorm(x, g, out_dtype, tm):
    m, d = x.shape
    return pl.pallas_call(
        _rmsnorm_body,
        grid=(m // tm,),
        in_specs=[pl.BlockSpec((tm, d), lambda i: (i, 0)),
                  pl.BlockSpec((1, d), lambda i: (0, 0))],
        out_specs=pl.BlockSpec((tm, d), lambda i: (i, 0)),
        out_shape=jax.ShapeDtypeStruct((m, d), out_dtype),
        compiler_params=_cparams(("arbitrary",)),
        name="rmsnorm",
    )(x, g.reshape(1, d))


CAST_ROWS = 256
STAGE_ROWS = 256
WEIGHT_DMA_PIECES = 4


def _mm_body(grp_ref, first_ref, nv_ref, *refs, n_w, mode):
    del grp_ref
    x_ref = refs[0]
    w_refs = refs[1:1 + n_w]
    pos = 1 + n_w
    res_ref = None
    if mode == "resid":
        res_ref = refs[pos]
        pos += 1
    o_ref = refs[pos]
    wb_refs = refs[pos + 1:pos + 1 + n_w]
    m = pl.program_id(1)
    valid = m < nv_ref[0]

    @pl.when(jnp.logical_and(valid, first_ref[m] == 1))
    def _cast():
        for w_ref, wb_ref in zip(w_refs, wb_refs):
            def body(c, carry, w_ref=w_ref, wb_ref=wb_ref):
                r = pl.multiple_of(c * CAST_ROWS, CAST_ROWS)
                wb_ref[pl.ds(r, CAST_ROWS), :] = w_ref[pl.ds(r, CAST_ROWS), :].astype(BF16)
                return carry
            lax.fori_loop(0, w_ref.shape[0] // CAST_ROWS, body, 0)

    @pl.when(valid)
    def _compute():
        x = x_ref[...]
        if mode == "gated":
            a = jnp.dot(x, wb_refs[0][...], preferred_element_type=F32)
            b = jnp.dot(x, wb_refs[1][...], preferred_element_type=F32)
            o_ref[...] = (_silu(a) * b).astype(o_ref.dtype)
        else:
            acc = jnp.dot(x, wb_refs[0][...], preferred_element_type=F32)
            if mode == "resid":
                acc = res_ref[...] + acc
            o_ref[...] = acc.astype(o_ref.dtype)

    @pl.when(jnp.logical_not(valid))
    def _zero():
        o_ref[...] = jnp.zeros_like(o_ref)


def _grouped_matmul(x, ws, tile_grp, tile_first, n_valid, *, tm, tn, mode, out_dtype, res=None, name,
                    weight_buffers=2):
    m, k = x.shape
    n = ws[0].shape[2]
    assert m % tm == 0 and n % tn == 0 and k % CAST_ROWS == 0
    n_w = len(ws)

    def row_of(mi, nv):
        return jnp.minimum(mi, nv[0] - 1)

    in_specs = [pl.BlockSpec((tm, k), lambda ni, mi, g, f, nv: (row_of(mi, nv), 0))]
    for _ in ws:
        in_specs.append(pl.BlockSpec((None, k, tn), lambda ni, mi, g, f, nv: (g[row_of(mi, nv)], 0, ni),
                                     pipeline_mode=pl.Buffered(weight_buffers)))
    args = [x] + list(ws)
    if mode == "resid":
        in_specs.append(pl.BlockSpec((tm, tn), lambda ni, mi, g, f, nv: (row_of(mi, nv), ni)))
        args.append(res)
    grid_spec = pltpu.PrefetchScalarGridSpec(
        num_scalar_prefetch=3,
        grid=(n // tn, m // tm),
        in_specs=in_specs,
        out_specs=pl.BlockSpec((tm, tn), lambda ni, mi, g, f, nv: (mi, ni)),
        scratch_shapes=[pltpu.VMEM((k, tn), BF16) for _ in ws],
    )
    return pl.pallas_call(
        functools.partial(_mm_body, n_w=n_w, mode=mode),
        grid_spec=grid_spec,
        out_shape=jax.ShapeDtypeStruct((m, n), out_dtype),
        compiler_params=_cparams(("arbitrary", "arbitrary")),
        name=name,
    )(tile_grp, tile_first, n_valid, *args)


def _dense_matmul(x, w, layer, *, tm, tn, mode, out_dtype, res=None, w2=None, name):
    tiles = x.shape[0] // tm
    grp = jnp.full((tiles,), layer, I32)
    first = jnp.zeros((tiles,), I32).at[0].set(1)
    nv = jnp.full((1,), tiles, I32)
    ws = [w] if w2 is None else [w, w2]
    return _grouped_matmul(x, ws, grp, first, nv, tm=tm, tn=tn, mode=mode, out_dtype=out_dtype,
                           res=res, name=name)


_SLOT, _ISSUED, _DONE, _NEXT_GRP, _NEXT_COL, _HAS_NEXT = range(6)


def _stream_body(grp_ref, first_ref, nextg_ref, wrap_ref, nv_ref, x_ref, *refs, n_w, mode):
    res_ref = refs[0] if mode == "resid" else None
    refs = refs[1:] if mode == "resid" else refs
    w_hbm = refs[:n_w]
    o_ref, wb_ref, stage_ref, st_ref, sem_ref = refs[n_w:n_w + 5]
    n = pl.program_id(0)
    m = pl.program_id(1)
    k, tn = stage_ref.shape
    valid = m < nv_ref[0]
    piece = k // WEIGHT_DMA_PIECES

    class copy:
        def __init__(self, j):
            col = pl.multiple_of(st_ref[_NEXT_COL] * tn, LANES)
            self.src = w_hbm[j].at[st_ref[_NEXT_GRP], :, pl.ds(col, tn)]

        def start(self):
            for p in range(WEIGHT_DMA_PIECES):
                rows = pl.ds(p * piece, piece)
                pltpu.make_async_copy(self.src.at[rows, :], stage_ref.at[rows, :], sem_ref.at[0]).start(priority=1)

        def wait(self):
            pltpu.make_async_copy(self.src, stage_ref, sem_ref.at[0]).wait()

    def cast(slot, j):
        for r in range(0, k, CAST_ROWS):
            wb_ref[slot, j, r:r + CAST_ROWS, :] = stage_ref[r:r + CAST_ROWS, :].astype(BF16)

    def compute(slot):
        x = x_ref[...]
        if mode == "gated":
            a = jnp.dot(x, wb_ref[slot, 0], preferred_element_type=F32)
            b = jnp.dot(x, wb_ref[slot, 1], preferred_element_type=F32)
            o_ref[...] = (_silu(a) * b).astype(o_ref.dtype)
        else:
            acc = jnp.dot(x, wb_ref[slot, 0], preferred_element_type=F32)
            if mode == "resid":
                acc = res_ref[...] + acc
            o_ref[...] = acc.astype(o_ref.dtype)

    @pl.when(jnp.logical_and(valid, first_ref[m] == 1))
    def _switch():
        @pl.when(jnp.logical_and(n == 0, m == 0))
        def _boot():
            st_ref[_SLOT] = 1
            st_ref[_ISSUED] = 0
            st_ref[_DONE] = 0
            st_ref[_NEXT_GRP] = grp_ref[0]
            st_ref[_NEXT_COL] = 0

        for j in range(n_w):
            @pl.when(st_ref[_DONE] <= j)
            def _finish(j=j):
                @pl.when(st_ref[_ISSUED] <= j)
                def _issue():
                    copy(j).start()
                copy(j).wait()
                cast(1 - st_ref[_SLOT], j)

        st_ref[_SLOT] = 1 - st_ref[_SLOT]
        st_ref[_ISSUED] = 0
        st_ref[_DONE] = 0
        st_ref[_NEXT_GRP] = nextg_ref[m]
        st_ref[_NEXT_COL] = n + wrap_ref[m]
        last_col = n == pl.num_programs(0) - 1
        st_ref[_HAS_NEXT] = jnp.where(jnp.logical_and(wrap_ref[m] == 1, last_col), 0, 1)

    slot = st_ref[_SLOT]
    issued = st_ref[_ISSUED]
    done = st_ref[_DONE]
    in_flight = done < issued
    can_start = jnp.logical_and(st_ref[_HAS_NEXT] == 1, issued == 0)

    @pl.when(jnp.logical_and(valid, jnp.logical_and(jnp.logical_not(in_flight), jnp.logical_not(can_start))))
    def _plain():
        compute(slot)

    @pl.when(jnp.logical_and(valid, jnp.logical_and(jnp.logical_not(in_flight), can_start)))
    def _start0():
        compute(slot)
        copy(0).start()
        st_ref[_ISSUED] = 1

    for j in range(n_w):
        @pl.when(jnp.logical_and(valid, jnp.logical_and(in_flight, done == j)))
        def _cast_step(j=j):
            copy(j).wait()
            cast(1 - slot, j)
            compute(slot)
            st_ref[_DONE] = j + 1
            if j + 1 < n_w:
                copy(j + 1).start()
                st_ref[_ISSUED] = j + 2

    @pl.when(jnp.logical_not(valid))
    def _zero():
        o_ref[...] = jnp.zeros_like(o_ref)


def _stream_matmul(x, ws, tile_grp, tile_first, tile_next, tile_wrap, n_valid, *, tm, tn, mode, out_dtype, name,
                   res=None):
    m, k = x.shape
    n = ws[0].shape[2]
    assert m % tm == 0 and n % tn == 0 and k % CAST_ROWS == 0 and tn % LANES == 0
    assert k % (WEIGHT_DMA_PIECES * SUBLANES) == 0
    n_w = len(ws)
    row_of = lambda mi, nv: jnp.minimum(mi, nv[0] - 1)
    extra_specs, extra_args = [], []
    if mode == "resid":
        extra_specs = [pl.BlockSpec((tm, tn), lambda ni, mi, g, f, nx, wr, nv: (row_of(mi, nv), ni))]
        extra_args = [res]
    grid_spec = pltpu.PrefetchScalarGridSpec(
        num_scalar_prefetch=5,
        grid=(n // tn, m // tm),
        in_specs=[pl.BlockSpec((tm, k), lambda ni, mi, g, f, nx, wr, nv: (row_of(mi, nv), 0))]
                 + extra_specs + [pl.BlockSpec(memory_space=pl.ANY)] * n_w,
        out_specs=pl.BlockSpec((tm, tn), lambda ni, mi, g, f, nx, wr, nv: (mi, ni)),
        scratch_shapes=[pltpu.VMEM((2, n_w, k, tn), BF16), pltpu.VMEM((k, tn), F32),
                        pltpu.SMEM((8,), I32), pltpu.SemaphoreType.DMA((1,))],
    )
    return pl.pallas_call(
        functools.partial(_stream_body, n_w=n_w, mode=mode),
        grid_spec=grid_spec,
        out_shape=jax.ShapeDtypeStruct((m, n), out_dtype),
        compiler_params=_cparams(("arbitrary", "arbitrary"), vmem=60 * 1024 * 1024),
        name=name,
    )(tile_grp, tile_first, tile_next, tile_wrap, n_valid, x, *extra_args, *ws)


def _dense_stream_matmul(x, w, layer, *, tm, tn, mode, out_dtype, res=None, w2=None, name):
    tiles = x.shape[0] // tm
    grp = jnp.full((tiles,), layer, I32)
    first = jnp.zeros((tiles,), I32).at[0].set(1)
    wrap = jnp.ones((tiles,), I32)
    nv = jnp.full((1,), tiles, I32)
    ws = [w] if w2 is None else [w, w2]
    return _stream_matmul(x, ws, grp, first, grp, wrap, nv, tm=tm, tn=tn, mode=mode, out_dtype=out_dtype,
                          res=res, name=name)


def _mix_prompt_body(z_ref, caw_ref, cbw_ref, cbb_ref, lbg_ref, lbb_ref, lcg_ref, lcb_ref, spw_ref,
                     spbt_ref, pw_ref, ps_ref, y_ref, na_ref, nb_ref, nd_ref,
                     qa_ext, gb_ext, pd_ext, vn_buf, wm_ref, gsh_ref, *, rows, width, ka, kb, nbuf_d):
    i = pl.program_id(1)
    last = pl.num_programs(1) - 1
    w = width
    gw = w // len(POOL_WINDOWS)
    chunk = wm_ref.shape[1]
    cg = wm_ref.shape[0]
    cgw = w // cg
    sub = 32

    @pl.when(i == 0)
    def _init():
        zeros = jnp.zeros((HIST, w), F32)
        qa_ext[0:HIST, :] = zeros
        gb_ext[0:HIST, :] = zeros
        pd_ext[0:HIST, :] = zeros
        tri = lax.broadcasted_iota(I32, (chunk, chunk), 0) >= lax.broadcasted_iota(I32, (chunk, chunk), 1)
        for g in range(cg):
            wm_ref[g] = jnp.where(tri, spw_ref[g], 0.0).astype(BF16)

    def col(k):
        return z_ref[:, k * w:(k + 1) * w].astype(F32)

    qa_ext[HIST:HIST + rows, :] = col(0) * col(1)
    gb_ext[HIST:HIST + rows, :] = col(3) * jax.nn.sigmoid(col(4))
    pd_ext[HIST:HIST + rows, :] = col(7)
    vn_buf[...] = _layernorm(col(6), lcg_ref[...], lcb_ref[...]).astype(BF16)

    conv_a = None
    for j in range(ka):
        term = caw_ref[j:j + 1, :] * qa_ext[HIST - (ka - 1) + j:HIST - (ka - 1) + j + rows, :]
        conv_a = term if conv_a is None else conv_a + term
    y_ref[:, 0:w] = (col(2) * conv_a).astype(y_ref.dtype)

    for f in range(1, SUBLANES):
        gsh_ref[f - 1, SUBLANES:HIST + rows, :] = gb_ext[SUBLANES - f:HIST + rows - f, :]
    for r0 in range(0, rows, sub):
        acc = None
        for j in range(kb):
            back = kb - 1 - j
            f, s = back % SUBLANES, HIST + r0 - (back - back % SUBLANES)
            src = gb_ext[s:s + sub, :] if f == 0 else gsh_ref[f - 1, s:s + sub, :]
            term = cbw_ref[j:j + 1, :] * src
            acc = term if acc is None else acc + term
        yb = _silu(_layernorm(acc + cbb_ref[...], lbg_ref[...], lbb_ref[...]))
        y_ref[r0:r0 + sub, w:2 * w] = yb.astype(y_ref.dtype)

    for c0 in range(0, rows, chunk):
        for g in range(cg):
            lanes = slice(g * cgw, (g + 1) * cgw)
            s = jnp.dot(wm_ref[g], vn_buf[c0:c0 + chunk, lanes], preferred_element_type=F32)
            s = s + spbt_ref[:, g:g + 1]
            u = z_ref[c0:c0 + chunk, 5 * w + g * cgw:5 * w + (g + 1) * cgw].astype(F32)
            y_ref[c0:c0 + chunk, 2 * w + g * cgw:2 * w + (g + 1) * cgw] = (u * s).astype(y_ref.dtype)

    pos = i * rows + lax.broadcasted_iota(I32, (rows, 1), 0)
    for gi, win in enumerate(POOL_WINDOWS):
        lanes = slice(gi * gw, (gi + 1) * gw)
        tot = None
        for k in range(win):
            term = pd_ext[HIST - k:HIST - k + rows, lanes]
            tot = term if tot is None else tot + term
        cnt = jnp.minimum(pos + 1, win).astype(F32)
        d = tot / cnt - pd_ext[HIST:HIST + rows, lanes]
        yd = jnp.dot(d.astype(BF16), pw_ref[gi].astype(BF16), preferred_element_type=F32)
        y_ref[:, 3 * w + gi * gw:3 * w + (gi + 1) * gw] = (yd * ps_ref[:, lanes]).astype(y_ref.dtype)

    @pl.when(i == last)
    def _state():
        na_ref[...] = qa_ext[HIST + rows - (ka - 1):HIST + rows, :]
        nb_ref[...] = gb_ext[HIST + rows - (kb - 1):HIST + rows, :]
        nd_ref[...] = pd_ext[HIST + rows - nbuf_d:HIST + rows, :]

    qa_ext[0:HIST, :] = qa_ext[rows:rows + HIST, :]
    gb_ext[0:HIST, :] = gb_ext[rows:rows + HIST, :]
    pd_ext[0:HIST, :] = pd_ext[rows:rows + HIST, :]


def _mix_prompt(z, lw, *, batch, seq, rows):
    w = lw["conv_a_w"].shape[1]
    ka = lw["conv_a_w"].shape[0]
    kb = lw["conv_b_w"].shape[0]
    nbuf_d = max(POOL_WINDOWS) - 1
    cg, chunk = lw["sp_w"].shape[0], lw["sp_w"].shape[1]
    assert seq % rows == 0 and rows % chunk == 0 and kb - 1 <= HIST and nbuf_d <= HIST
    nt = seq // rows
    full2 = lambda a: pl.BlockSpec(a.shape, lambda b, i: (0, 0))
    full3 = lambda a: pl.BlockSpec(a.shape, lambda b, i: (0, 0, 0))
    vec = lambda a: a.reshape(1, w)
    small = [lw["conv_a_w"], lw["conv_b_w"], vec(lw["conv_b_b"]), vec(lw["ln_b_g"]), vec(lw["ln_b_b"]),
             vec(lw["ln_c_g"]), vec(lw["ln_c_b"])]
    spbt = lw["sp_b"].T
    in_specs = ([pl.BlockSpec((rows, 8 * w), lambda b, i: (b * nt + i, 0))]
                + [full2(a) for a in small]
                + [full3(lw["sp_w"]), full2(spbt), full3(lw["pool_w"]), full2(vec(lw["pool_scale"]))])
    out_shapes = (jax.ShapeDtypeStruct((batch * seq, 4 * w), BF16),
                  jax.ShapeDtypeStruct((batch, ka - 1, w), F32),
                  jax.ShapeDtypeStruct((batch, kb - 1, w), F32),
                  jax.ShapeDtypeStruct((batch, nbuf_d, w), F32))
    out_specs = (pl.BlockSpec((rows, 4 * w), lambda b, i: (b * nt + i, 0)),
                 pl.BlockSpec((None, ka - 1, w), lambda b, i: (b, 0, 0)),
                 pl.BlockSpec((None, kb - 1, w), lambda b, i: (b, 0, 0)),
                 pl.BlockSpec((None, nbuf_d, w), lambda b, i: (b, 0, 0)))
    scratch = [pltpu.VMEM((HIST + rows, w), F32), pltpu.VMEM((HIST + rows, w), F32),
               pltpu.VMEM((HIST + rows, w), F32), pltpu.VMEM((rows, w), BF16),
               pltpu.VMEM((cg, chunk, chunk), BF16),
               pltpu.VMEM((SUBLANES - 1, HIST + rows, w), F32)]
    return pl.pallas_call(
        functools.partial(_mix_prompt_body, rows=rows, width=w, ka=ka, kb=kb, nbuf_d=nbuf_d),
        grid=(batch, nt),
        in_specs=in_specs,
        out_specs=out_specs,
        out_shape=out_shapes,
        scratch_shapes=scratch,
        compiler_params=_cparams(("arbitrary", "arbitrary")),
        name="mix_prompt",
    )(z, *small, lw["sp_w"], spbt, lw["pool_w"], vec(lw["pool_scale"]))


def _mix_sample_body(spw_ref, spb_ref, z_ref, sa_ref, sb_ref, sd_ref, caw_ref, cbw_ref, cbb_ref,
                     lbg_ref, lbb_ref, lcg_ref, lcb_ref, pw_ref, ps_ref,
                     y_ref, na_ref, nb_ref, nd_ref, cv_ref, *, nseq, steps, width, ka, kb, nbuf_d, cg):
    w = width
    gw = w // len(POOL_WINDOWS)
    cgw = w // cg
    lane = lax.broadcasted_iota(I32, (1, w), 1)

    def col(t, k):
        return z_ref[t * nseq:(t + 1) * nseq, k * w:(k + 1) * w].astype(F32)

    def put(t, k, val):
        y_ref[t * nseq:(t + 1) * nseq, k * w:(k + 1) * w] = val.astype(y_ref.dtype)

    ext = [sa_ref[k] for k in range(ka - 1)] + [col(t, 0) * col(t, 1) for t in range(steps)]
    for t in range(steps):
        conv = None
        for j in range(ka):
            term = caw_ref[j:j + 1, :] * ext[t + j]
            conv = term if conv is None else conv + term
        put(t, 0, col(t, 2) * conv)
    for k in range(ka - 1):
        na_ref[k] = ext[steps + k]

    for t in range(steps):
        nb_ref[kb - 1 - steps + t] = col(t, 3) * jax.nn.sigmoid(col(t, 4))
    for k in range(kb - 1 - steps):
        nb_ref[k] = sb_ref[k + steps]

    def ext_b(k):
        return sb_ref[k] if k < kb - 1 else nb_ref[k - steps]

    for t in range(steps):
        acc = None
        for j in range(kb):
            term = cbw_ref[j:j + 1, :] * ext_b(t + j)
            acc = term if acc is None else acc + term
        put(t, 1, _silu(_layernorm(acc + cbb_ref[...], lbg_ref[...], lbb_ref[...])))

    def lane_groups(vals):
        out = jnp.full((1, w), vals[cg - 1], F32)
        for g in range(cg - 2, -1, -1):
            out = jnp.where(lane < (g + 1) * cgw, vals[g], out)
        return out

    for t in range(steps):
        cv_ref[t] = _layernorm(col(t, 6), lcg_ref[...], lcb_ref[...])
    for t in range(steps):
        s = lane_groups([spb_ref[g, t] for g in range(cg)])
        for sp in range(t + 1):
            s = s + lane_groups([spw_ref[g, t, sp] for g in range(cg)]) * cv_ref[sp]
        put(t, 2, col(t, 5) * s)

    for t in range(steps):
        nd_ref[nbuf_d - steps + t] = col(t, 7)
    for k in range(nbuf_d - steps):
        nd_ref[k] = sd_ref[k + steps]

    def ext_d(k):
        return sd_ref[k] if k < nbuf_d else nd_ref[k - steps]

    for t in range(steps):
        for gi, win in enumerate(POOL_WINDOWS):
            lanes = slice(gi * gw, (gi + 1) * gw)
            tot = None
            for k in range(win):
                term = ext_d(nbuf_d + t - k)[:, lanes]
                tot = term if tot is None else tot + term
            cnt = float(min(PAST_LEN + t + 1, win))
            d = tot / cnt - ext_d(nbuf_d + t)[:, lanes]
            yd = jnp.dot(d.astype(BF16), pw_ref[gi].astype(BF16), preferred_element_type=F32)
            y_ref[t * nseq:(t + 1) * nseq, 3 * w + gi * gw:3 * w + (gi + 1) * gw] = (
                yd * ps_ref[:, lanes]).astype(y_ref.dtype)


def _mix_sample(z, sa, sb, sd, lw, *, nseq, steps, m_prompt):
    w = lw["conv_a_w"].shape[1]
    ka = lw["conv_a_w"].shape[0]
    kb = lw["conv_b_w"].shape[0]
    nbuf_d = max(POOL_WINDOWS) - 1
    cg = lw["sp_w"].shape[0]
    ms = nseq * steps
    assert m_prompt % ms == 0 and steps <= nbuf_d and steps <= kb - 1
    rb = m_prompt // ms
    spw_small = lw["sp_w"][:, :steps, :steps]
    spb_small = lw["sp_b"][:, :steps]
    vec = lambda a: a.reshape(1, w)
    smem = pl.BlockSpec(memory_space=pltpu.SMEM)
    full2 = lambda a: pl.BlockSpec(a.shape, lambda i: (0, 0))
    full3 = lambda a: pl.BlockSpec(a.shape, lambda i: (0, 0, 0))
    small = [lw["conv_a_w"], lw["conv_b_w"], vec(lw["conv_b_b"]), vec(lw["ln_b_g"]), vec(lw["ln_b_b"]),
             vec(lw["ln_c_g"]), vec(lw["ln_c_b"])]
    in_specs = ([smem, smem,
                 pl.BlockSpec((ms, 8 * w), lambda i: (rb, 0)),
                 full3(sa), full3(sb), full3(sd)]
                + [full2(a) for a in small]
                + [full3(lw["pool_w"]), full2(vec(lw["pool_scale"]))])
    out_shapes = (jax.ShapeDtypeStruct((ms, 4 * w), BF16),
                  jax.ShapeDtypeStruct(sa.shape, F32), jax.ShapeDtypeStruct(sb.shape, F32),
                  jax.ShapeDtypeStruct(sd.shape, F32), jax.ShapeDtypeStruct((steps, nseq, w), F32))
    out_specs = (pl.BlockSpec((ms, 4 * w), lambda i: (0, 0)),
                 full3(sa), full3(sb), full3(sd),
                 pl.BlockSpec((steps, nseq, w), lambda i: (0, 0, 0)))
    return pl.pallas_call(
        functools.partial(_mix_sample_body, nseq=nseq, steps=steps, width=w, ka=ka, kb=kb,
                          nbuf_d=nbuf_d, cg=cg),
        grid=(1,),
        in_specs=in_specs,
        out_specs=out_specs,
        out_shape=out_shapes,
        compiler_params=_cparams(("arbitrary",)),
        name="mix_sample",
    )(spw_small, spb_small, z, sa, sb, sd, *small, lw["pool_w"], vec(lw["pool_scale"]))


def _merge_out_body(yp_ref, ys_ref, g0_ref, g1_ref, g2_ref, g3_ref, x_ref, gn_ref, wa_hbm, wb_hbm, wc_hbm, wd_hbm,
                    wo_hbm, *refs, width, prompt_tiles, layer, with_h):
    n_out = 2 if with_h else 1
    x1_ref = refs[0]
    wbr_ref, wout_ref, stage_ref, sem_ref = refs[n_out:n_out + 4]
    i = pl.program_id(0)
    rows = stage_ref.shape[1]
    d = wout_ref.shape[0]

    @pl.when(i == 0)
    def _load_weights():
        chunks = []
        for k, w_hbm in enumerate((wa_hbm, wb_hbm, wc_hbm, wd_hbm)):
            for r in range(0, width, rows):
                chunks.append((w_hbm.at[layer, pl.ds(r, rows), :], wbr_ref.at[k, pl.ds(r, rows), :]))
        for r in range(0, d, rows):
            chunks.append((wo_hbm.at[layer, pl.ds(r, rows), :], wout_ref.at[pl.ds(r, rows), :]))

        def fetch(c):
            return pltpu.make_async_copy(chunks[c][0], stage_ref.at[c % 2], sem_ref.at[c % 2])

        fetch(0).start(priority=0)
        for c in range(len(chunks)):
            if c + 1 < len(chunks):
                fetch(c + 1).start(priority=(c + 1) % 2)
            fetch(c).wait()
            chunks[c][1][...] = stage_ref[c % 2].astype(BF16)

    def run(y_ref):
        acc = None
        for k, g_ref in enumerate((g0_ref, g1_ref, g2_ref, g3_ref)):
            br = jnp.dot(y_ref[:, k * width:(k + 1) * width], wbr_ref[k], preferred_element_type=F32)
            term = jax.nn.sigmoid(g_ref[...].astype(F32)) * br
            acc = term if acc is None else acc + term
        x1 = x_ref[...] + jnp.dot(acc.astype(BF16), wout_ref[...], preferred_element_type=F32)
        x1_ref[...] = x1
        if with_h:
            refs[1][...] = _rms(x1, gn_ref[...]).astype(BF16)

    @pl.when(i < prompt_tiles)
    def _prompt():
        run(yp_ref)

    @pl.when(i >= prompt_tiles)
    def _sample():
        run(ys_ref)


def _merge_out(y_p, y_s, z, x, w_brs, w_out, layer, g, *, tm, with_h):
    mp, yw = y_p.shape
    ms = y_s.shape[0]
    assert mp % tm == 0 and ms % tm == 0
    tp = mp // tm
    w = yw // 4
    d = w_out.shape[2]
    assert (2 * yw) % d == 0 and w % STAGE_ROWS == 0 and d % STAGE_ROWS == 0
    goff = (2 * yw) // d
    gate_spec = lambda k: pl.BlockSpec((tm, d), lambda i, k=k: (i, goff + k))
    row = pl.BlockSpec((tm, d), lambda i: (i, 0))
    hbm = pl.BlockSpec(memory_space=pl.ANY)
    out_shape = [jax.ShapeDtypeStruct((mp + ms, d), F32)]
    out_specs = [row]
    if with_h:
        out_shape.append(jax.ShapeDtypeStruct((mp + ms, d), BF16))
        out_specs.append(row)
    return pl.pallas_call(
        functools.partial(_merge_out_body, width=w, prompt_tiles=tp, layer=layer, with_h=with_h),
        grid=((mp + ms) // tm,),
        in_specs=[pl.BlockSpec((tm, yw), lambda i: (jnp.minimum(i, tp - 1), 0)),
                  pl.BlockSpec((tm, yw), lambda i: (jnp.maximum(i - tp, 0), 0))]
                 + [gate_spec(k) for k in range(4)]
                 + [row, pl.BlockSpec((1, d), lambda i: (0, 0))] + [hbm] * 5,
        out_specs=out_specs,
        out_shape=out_shape,
        scratch_shapes=[pltpu.VMEM((4, w, d), BF16), pltpu.VMEM((d, d), BF16),
                        pltpu.VMEM((2, STAGE_ROWS, d), F32), pltpu.SemaphoreType.DMA((2,))],
        compiler_params=_cparams(("arbitrary",)),
        name="merge_out",
    )(y_p, y_s, z, z, z, z, x, g.reshape(1, d), *w_brs, w_out)


def _route_body(x_ref, g_ref, rwt_ref, e_ref, r_ref, wt_ref, cnt_ref, carry_ref, tri_ref):
    i = pl.program_id(0)
    ne = rwt_ref.shape[0]
    tm = x_ref.shape[0]

    @pl.when(i == 0)
    def _init():
        carry_ref[...] = jnp.zeros_like(carry_ref)
        upper = lax.broadcasted_iota(I32, (tm, tm), 0) <= lax.broadcasted_iota(I32, (tm, tm), 1)
        tri_ref[...] = jnp.where(upper, 1.0, 0.0).astype(BF16)

    h = _rms(x_ref[...], g_ref[...])
    h_hi = h.astype(BF16)
    h_lo = (h - h_hi.astype(F32)).astype(BF16)
    rw = rwt_ref[...]
    rw_hi = rw.astype(BF16)
    rw_lo = (rw - rw_hi.astype(F32)).astype(BF16)
    dn = (((1,), (1,)), ((), ()))
    dg = lambda a, b: lax.dot_general(a, b, dn, preferred_element_type=F32)
    logits = dg(rw_hi, h_hi) + dg(rw_hi, h_lo) + dg(rw_lo, h_hi)
    ex = jnp.exp(logits - jnp.max(logits, axis=0, keepdims=True))
    p = ex / jnp.sum(ex, axis=0, keepdims=True)
    eid = lax.broadcasted_iota(I32, (ne, tm), 0)
    m1 = jnp.max(p, axis=0, keepdims=True)
    i1 = jnp.min(jnp.where(p == m1, eid, ne), axis=0, keepdims=True)
    oh1 = eid == i1
    p2 = jnp.where(oh1, -1.0, p)
    m2 = jnp.max(p2, axis=0, keepdims=True)
    i2 = jnp.min(jnp.where(p2 == m2, eid, ne), axis=0, keepdims=True)
    oh2 = eid == i2
    den = m1 + m2
    sel = jnp.logical_or(oh1, oh2)
    sel_f = jnp.where(sel, 1.0, 0.0)
    cum = jnp.dot(sel_f.astype(BF16), tri_ref[...], preferred_element_type=F32) + carry_ref[...]
    rank1 = jnp.sum(jnp.where(oh1, cum, 0.0), axis=0, keepdims=True) - 1.0
    rank2 = jnp.sum(jnp.where(oh2, cum, 0.0), axis=0, keepdims=True) - 1.0
    carry_ref[...] = carry_ref[...] + jnp.sum(sel_f, axis=1, keepdims=True)
    e_ref[0:1, :] = i1
    e_ref[1:2, :] = i2
    r_ref[0:1, :] = rank1.astype(I32)
    r_ref[1:2, :] = rank2.astype(I32)
    wt_ref[0:1, :] = m1 / den
    wt_ref[1:2, :] = m2 / den

    @pl.when(i == pl.num_programs(0) - 1)
    def _counts():
        cnt_ref[...] = jnp.broadcast_to(carry_ref[...], cnt_ref.shape).astype(I32)


def _route(x1, g, router_w, *, tm):
    m, d = x1.shape
    ne = router_w.shape[1]
    tok = lambda dt: jax.ShapeDtypeStruct((TOP_K, m), dt)
    tok_spec = pl.BlockSpec((TOP_K, tm), lambda i: (0, i))
    return pl.pallas_call(
        _route_body,
        grid=(m // tm,),
        in_specs=[pl.BlockSpec((tm, d), lambda i: (i, 0)),
                  pl.BlockSpec((1, d), lambda i: (0, 0)),
                  pl.BlockSpec((ne, d), lambda i: (0, 0))],
        out_specs=(tok_spec, tok_spec, tok_spec, pl.BlockSpec((ne, LANES), lambda i: (0, 0))),
        out_shape=(tok(I32), tok(I32), tok(F32), jax.ShapeDtypeStruct((ne, LANES), I32)),
        scratch_shapes=[pltpu.VMEM((ne, 1), F32), pltpu.VMEM((tm, tm), BF16)],
        compiler_params=_cparams(("arbitrary",)),
        name="route",
    )(x1, g.reshape(1, d), router_w.T)


def _plan_body(e_ref, r_ref, cnt_ref, pos_ref, tgrp_ref, tfirst_ref, misc_ref, *, tile_shift):
    ne = cnt_ref.shape[0]
    tile = 1 << tile_shift
    sub = lax.broadcasted_iota(I32, (ne, LANES), 0)
    lane = lax.broadcasted_iota(I32, (ne, LANES), 1)

    def as_row(col):
        return jnp.sum(jnp.where(sub == lane, col, 0), axis=0, keepdims=True)

    cnt = cnt_ref[:, 0:1]
    padded = lax.shift_left(lax.shift_right_logical(cnt + (tile - 1), tile_shift), tile_shift)
    gstart = jnp.sum(jnp.where(lane < sub, as_row(padded), 0), axis=1, keepdims=True)
    gend = gstart + padded
    m = e_ref.shape[1]
    eid = lax.broadcasted_iota(I32, (ne, m), 0)
    for k in range(TOP_K):
        base = jnp.sum(jnp.where(eid == e_ref[k:k + 1, :], gstart, 0), axis=0, keepdims=True)
        pos_ref[k:k + 1, :] = base + r_ref[k:k + 1, :]
    start = lax.broadcasted_iota(I32, (1, LANES), 1) * tile
    grp = jnp.sum((gend <= start).astype(I32), axis=0, keepdims=True)
    tgrp_ref[...] = jnp.minimum(grp, ne - 1)
    tfirst_ref[...] = jnp.sum(jnp.logical_and(gstart == start, padded > 0).astype(I32), axis=0, keepdims=True)
    n_tiles = lax.shift_right_logical(jnp.max(gend, axis=0, keepdims=True), tile_shift)
    misc = jnp.where(sub == 0, n_tiles, 0)
    misc = jnp.where(sub == 1, as_row(gstart + cnt), misc)
    misc = jnp.where(sub == 2, as_row(gend), misc)
    lane_f = lane.astype(F32)
    nonempty = jnp.logical_and(as_row(padded) > 0, lane < ne)
    later = jnp.min(jnp.where(jnp.logical_and(nonempty, lane > sub), lane_f, float(ne)), axis=1, keepdims=True)
    firstg = jnp.min(jnp.where(nonempty, lane_f, float(ne)), axis=1, keepdims=True)
    nxt = jnp.where(later < ne, later, firstg).astype(I32)
    wraps = (later >= ne).astype(I32)
    mine = sub == jnp.minimum(grp, ne - 1)
    misc = jnp.where(sub == 3, jnp.sum(jnp.where(mine, nxt, 0), axis=0, keepdims=True), misc)
    misc = jnp.where(sub == 4, jnp.sum(jnp.where(mine, wraps, 0), axis=0, keepdims=True), misc)
    misc_ref[...] = misc


def _plan(e, r, cnt, *, tile):
    tile_shift = tile.bit_length() - 1
    assert 1 << tile_shift == tile
    ne = cnt.shape[0]
    m = e.shape[1]
    whole = lambda a: pl.BlockSpec(a.shape, lambda i: (0, 0))
    row = jax.ShapeDtypeStruct((1, LANES), I32)
    out_shape = (jax.ShapeDtypeStruct((TOP_K, m), I32), row, row, jax.ShapeDtypeStruct((ne, LANES), I32))
    return pl.pallas_call(
        functools.partial(_plan_body, tile_shift=tile_shift),
        grid=(1,),
        in_specs=[whole(e), whole(r), whole(cnt)],
        out_specs=tuple(pl.BlockSpec(s.shape, lambda i: (0, 0)) for s in out_shape),
        out_shape=out_shape,
        compiler_params=_cparams(("arbitrary",)),
        name="plan",
    )(e, r, cnt)


def _dispatch_body(pos0_ref, pos1_ref, nv_ref, fill_ref, end_ref, x_hbm, g_ref, o_ref, src_ref, buf_ref, sem_ref):
    j = pl.program_id(0)
    tm = o_ref.shape[0]
    n_tok = pos0_ref.shape[0]
    nv = nv_ref[0]

    def start_tile(tile, slot):
        def body(pair, carry):
            for p in range(2):
                r = pair * 2 + p
                tok = src_ref[tile * tm + r]
                pltpu.make_async_copy(x_hbm.at[pl.ds(tok, 1), :], buf_ref.at[slot, pl.ds(r, 1), :],
                                      sem_ref.at[slot]).start(priority=p)
            return carry
        lax.fori_loop(0, tm // 2, body, 0, unroll=ISSUE_UNROLL // 2)

    @pl.when(j == 0)
    def _prologue():
        for g in range(fill_ref.shape[0]):
            def fill(p, carry):
                src_ref[p] = 0
                return carry
            lax.fori_loop(fill_ref[g], end_ref[g], fill, 0)

        def inv(t, carry):
            src_ref[pos0_ref[t]] = t
            src_ref[pos1_ref[t]] = t
            return carry
        lax.fori_loop(0, n_tok, inv, 0, unroll=ISSUE_UNROLL)
        start_tile(0, 0)

    slot = lax.rem(j, 2)

    @pl.when(j < nv)
    def _work():
        @pl.when(j + 1 < nv)
        def _prefetch():
            start_tile(j + 1, 1 - slot)

        pltpu.make_async_copy(x_hbm.at[pl.ds(0, tm), :], buf_ref.at[slot], sem_ref.at[slot]).wait()
        o_ref[...] = _rms(buf_ref[slot], g_ref[...]).astype(o_ref.dtype)

    @pl.when(j >= nv)
    def _skip():
        o_ref[...] = jnp.zeros_like(o_ref)


def _dispatch(x1, g, pos0, pos1, n_tiles, fill_row, end_row, *, tm, max_tiles):
    m, d = x1.shape
    grid_spec = pltpu.PrefetchScalarGridSpec(
        num_scalar_prefetch=5,
        grid=(max_tiles,),
        in_specs=[pl.BlockSpec(memory_space=pl.ANY),
                  pl.BlockSpec((1, d), lambda j, *_: (0, 0))],
        out_specs=pl.BlockSpec((tm, d), lambda j, *_: (j, 0)),
        scratch_shapes=[pltpu.SMEM((max_tiles * tm,), I32),
                        pltpu.VMEM((2, tm, d), F32),
                        pltpu.SemaphoreType.DMA((2,))],
    )
    return pl.pallas_call(
        _dispatch_body,
        grid_spec=grid_spec,
        out_shape=jax.ShapeDtypeStruct((max_tiles * tm, d), BF16),
        compiler_params=_cparams(("arbitrary",)),
        name="dispatch",
    )(pos0, pos1, n_tiles, fill_row, end_row, x1, g.reshape(1, d))


def _combine_body(pos0_ref, pos1_ref, ys_hbm, x_ref, wt_ref, g_ref, xn_ref, h_ref, buf_ref, sem_ref):
    i = pl.program_id(0)
    n = pl.num_programs(0)
    tm = x_ref.shape[0]

    def start_tile(tile, slot):
        def body(r, carry):
            t = tile * tm + r
            pltpu.make_async_copy(ys_hbm.at[pl.ds(pos0_ref[t], 1), :], buf_ref.at[slot, 0, pl.ds(r, 1), :],
                                  sem_ref.at[slot, 0]).start(priority=0)
            pltpu.make_async_copy(ys_hbm.at[pl.ds(pos1_ref[t], 1), :], buf_ref.at[slot, 1, pl.ds(r, 1), :],
                                  sem_ref.at[slot, 1]).start(priority=1)
            return carry
        lax.fori_loop(0, tm, body, 0, unroll=ISSUE_UNROLL)

    @pl.when(i == 0)
    def _prologue():
        start_tile(0, 0)

    slot = lax.rem(i, 2)

    @pl.when(i + 1 < n)
    def _prefetch():
        start_tile(i + 1, 1 - slot)

    for k in range(TOP_K):
        pltpu.make_async_copy(ys_hbm.at[pl.ds(0, tm), :], buf_ref.at[slot, k], sem_ref.at[slot, k]).wait()
    xn = x_ref[...] + wt_ref[:, 0:1] * buf_ref[slot, 0] + wt_ref[:, 1:2] * buf_ref[slot, 1]
    xn_ref[...] = xn
    h_ref[...] = _rms(xn, g_ref[...]).astype(h_ref.dtype)


def _combine(ys, x1, wt_cols, g, pos0, pos1, *, tm, h_dtype):
    m, d = x1.shape
    row = lambda: pl.BlockSpec((tm, d), lambda i, *_: (i, 0))
    grid_spec = pltpu.PrefetchScalarGridSpec(
        num_scalar_prefetch=2,
        grid=(m // tm,),
        in_specs=[pl.BlockSpec(memory_space=pl.ANY), row(),
                  pl.BlockSpec((tm, TOP_K), lambda i, *_: (i, 0)),
                  pl.BlockSpec((1, d), lambda i, *_: (0, 0))],
        out_specs=(row(), row()),
        scratch_shapes=[pltpu.VMEM((2, TOP_K, tm, d), F32), pltpu.SemaphoreType.DMA((2, TOP_K))],
    )
    return pl.pallas_call(
        _combine_body,
        grid_spec=grid_spec,
        out_shape=(jax.ShapeDtypeStruct((m, d), F32), jax.ShapeDtypeStruct((m, d), h_dtype)),
        compiler_params=_cparams(("arbitrary",)),
        name="combine",
    )(pos0, pos1, ys, x1, wt_cols, g.reshape(1, d))


def _pick(total, prefs):
    for p in prefs:
        if total % p == 0:
            return p
    raise ValueError(f"no tile for {total} in {prefs}")


def _row_tile(m, cap):
    best = None
    for t in range(16, cap + 1, 16):
        if m % t == 0:
            best = t
    assert best is not None
    return best


def kernel(x_prompt, x_sample, state_conv_a, state_conv_b, state_pool, norm_mix_g, w_in, conv_a_w, w_br_a,
           conv_b_w, conv_b_b, ln_b_g, ln_b_b, w_br_b, ln_c_g, ln_c_b, sp_w, sp_b, w_br_c, pool_w,
           pool_scale, w_br_d, w_out, norm_ffn_g, ffn_w1, ffn_w3, ffn_w2, router_w, moe_w1, moe_w3, moe_w2,
           final_norm_g):
    batch, seq, d = x_prompt.shape
    nseq, steps, _ = x_sample.shape
    depth = w_in.shape[0]
    n_exp = router_w.shape[2]
    mp, ms = batch * seq, nseq * steps
    m = mp + ms

    x = jnp.concatenate([x_prompt.reshape(mp, d), jnp.swapaxes(x_sample, 0, 1).reshape(ms, d)], axis=0)
    st_a = jnp.swapaxes(state_conv_a, 1, 2)
    st_b = jnp.swapaxes(state_conv_b, 1, 2)
    st_d = jnp.swapaxes(state_pool, 1, 2)

    tm_big = _row_tile(m, 1088)
    tm_mid = _row_tile(m, 544)
    tm_small = _row_tile(m, 256)
    tm_merge = min(_row_tile(mp, 256), _row_tile(ms, 256))
    assert mp % tm_merge == 0 and ms % tm_merge == 0
    tm_moe = 512
    max_tiles = -(-(m * TOP_K) // tm_moe) + n_exp
    assert max_tiles <= LANES
    mix_rows = _pick(seq, (256, 128))
    moe_w1f = moe_w1.reshape((-1,) + moe_w1.shape[2:])
    moe_w3f = moe_w3.reshape((-1,) + moe_w3.shape[2:])
    moe_w2f = moe_w2.reshape((-1,) + moe_w2.shape[2:])
    dffe = moe_w1.shape[3]
    dff = ffn_w1.shape[2]

    h = _rmsnorm(x, norm_mix_g[0], BF16, tm_mid)
    outs_p = ([], [], [])
    outs_s = ([], [], [], [])
    y_final = None
    for i in range(depth):
        lw = dict(conv_a_w=conv_a_w[i], conv_b_w=conv_b_w[i], conv_b_b=conv_b_b[i], ln_b_g=ln_b_g[i],
                  ln_b_b=ln_b_b[i], ln_c_g=ln_c_g[i], ln_c_b=ln_c_b[i], sp_w=sp_w[i], sp_b=sp_b[i],
                  pool_w=pool_w[i], pool_scale=pool_scale[i])
        z = _dense_stream_matmul(h, w_in, i, tm=tm_big, tn=_pick(w_in.shape[2], (1024, 512, 256, 128)),
                                 mode="plain", out_dtype=BF16, name="inproj")
        y_p, na_p, nb_p, nd_p = _mix_prompt(z, lw, batch=batch, seq=seq, rows=mix_rows)
        y_s, na_s, nb_s, nd_s, cv_s = _mix_sample(z, st_a[i], st_b[i], st_d[i], lw, nseq=nseq, steps=steps,
                                                  m_prompt=mp)
        for acc, val in zip(outs_p, (na_p, nb_p, nd_p)):
            acc.append(val)
        for acc, val in zip(outs_s, (na_s, nb_s, nd_s, cv_s)):
            acc.append(val)
        merge_out = functools.partial(_merge_out, y_p, y_s, z, x, (w_br_a, w_br_b, w_br_c, w_br_d), w_out, i,
                                      norm_ffn_g[i], tm=tm_merge)
        last = i == depth - 1
        g_next = final_norm_g if last else norm_mix_g[i + 1]
        h_dtype = F32 if last else BF16
        j = i // 2
        if i % 2 == 0:
            x1, h2 = merge_out(with_h=True)
            a = _dense_stream_matmul(h2, ffn_w1, j, w2=ffn_w3, tm=tm_big, tn=_pick(dff, (512, 256, 128)),
                                     mode="gated", out_dtype=BF16, name="ffn_up")
            x = _dense_stream_matmul(a, ffn_w2, j, tm=tm_mid, tn=_pick(d, (512, 256, 128)), mode="resid",
                                     out_dtype=F32, res=x1, name="ffn_down")
            h = _rmsnorm(x, g_next, h_dtype, tm_mid)
        else:
            (x1,) = merge_out(with_h=False)
            e, r, wt, cnt = _route(x1, norm_ffn_g[i], router_w[j], tm=_pick(m, (512, 256, 128)))
            pos, tgrp, tfirst, misc = _plan(e, r, cnt, tile=tm_moe)
            n_tiles = misc[0, 0:1]
            hs = _dispatch(x1, norm_ffn_g[i], pos[0], pos[1], n_tiles, misc[1, :n_exp], misc[2, :n_exp],
                           tm=tm_moe, max_tiles=max_tiles)
            grp = tgrp[0, :max_tiles] + j * n_exp
            first = tfirst[0, :max_tiles]
            nextg = misc[3, :max_tiles] + j * n_exp
            wrap = misc[4, :max_tiles]
            a = _stream_matmul(hs, [moe_w1f, moe_w3f], grp, first, nextg, wrap, n_tiles, tm=tm_moe,
                               tn=_pick(dffe, (1408, 1024, 512, 256, 128)), mode="gated", out_dtype=BF16,
                               name="moe_up")
            ys = _stream_matmul(a, [moe_w2f], grp, first, nextg, wrap, n_tiles, tm=tm_moe,
                                tn=_pick(d, (1024, 512, 256, 128)), mode="plain", out_dtype=F32, name="moe_down")
            x, h = _combine(ys, x1, wt.T, g_next, pos[0], pos[1], tm=tm_small, h_dtype=h_dtype)
        if last:
            y_final = h

    y_prompt = y_final[:mp].reshape(batch, seq, d)
    y_sample = jnp.swapaxes(y_final[mp:].reshape(steps, nseq, d), 0, 1)
    stack_s = lambda vals: jnp.swapaxes(jnp.stack(vals), 1, 2)
    return (y_prompt, y_sample, jnp.stack(outs_p[0]), jnp.stack(outs_p[1]), jnp.stack(outs_p[2]),
            stack_s(outs_s[0]), stack_s(outs_s[1]), stack_s(outs_s[2]), stack_s(outs_s[3]))
```

```python
import functools

import jax
import jax.numpy as jnp
from jax import lax
from jax.experimental import pallas as pl
from jax.experimental.pallas import tpu as pltpu

F32 = jnp.float32
BF16 = jnp.bfloat16
I32 = jnp.int32

EPS = 1e-6
PAST_LEN = 16384
POOL_WINDOWS = (2, 4, 8, 16)
TOP_K = 2
LANES = 128
SUBLANES = 8
HIST = 32
ISSUE_UNROLL = 8
VMEM_LIMIT = 56 * 1024 * 1024


def _cparams(sem, vmem=VMEM_LIMIT):
    return pltpu.CompilerParams(dimension_semantics=sem, vmem_limit_bytes=vmem)


def _rms(x, g):
    return x * lax.rsqrt(jnp.mean(x * x, axis=-1, keepdims=True) + EPS) * g


def _layernorm(x, g, b):
    mu = jnp.mean(x, axis=-1, keepdims=True)
    xc = x - mu
    var = jnp.mean(xc * xc, axis=-1, keepdims=True)
    return xc * lax.rsqrt(var + EPS) * g + b


def _silu(x):
    return x * jax.nn.sigmoid(x)


def _rmsnorm_body(x_ref, g_ref, o_ref):
    o_ref[...] = _rms(x_ref[...], g_ref[...]).astype(o_ref.dtype)


def _rmsnorm(x, g, out_dtype, tm):
    m, d = x.shape
    return pl.pallas_call(
        _rmsnorm_body,
        grid=(m // tm,),
        in_specs=[pl.BlockSpec((tm, d), lambda i: (i, 0)),
                  pl.BlockSpec((1, d), lambda i: (0, 0))],
        out_specs=pl.BlockSpec((tm, d), lambda i: (i, 0)),
        out_shape=jax.ShapeDtypeStruct((m, d), out_dtype),
        compiler_params=_cparams(("arbitrary",)),
        name="rmsnorm",
    )(x, g.reshape(1, d))


CAST_ROWS = 256
STAGE_ROWS = 256
MOE_STREAM_STEPS = 4


def _mm_body(grp_ref, first_ref, nv_ref, *refs, n_w, mode):
    del grp_ref
    x_ref = refs[0]
    w_refs = refs[1:1 + n_w]
    pos = 1 + n_w
    res_ref = None
    if mode == "resid":
        res_ref = refs[pos]
        pos += 1
    o_ref = refs[pos]
    wb_refs = refs[pos + 1:pos + 1 + n_w]
    m = pl.program_id(1)
    valid = m < nv_ref[0]

    @pl.when(jnp.logical_and(valid, first_ref[m] == 1))
    def _cast():
        for w_ref, wb_ref in zip(w_refs, wb_refs):
            def body(c, carry, w_ref=w_ref, wb_ref=wb_ref):
                r = pl.multiple_of(c * CAST_ROWS, CAST_ROWS)
                wb_ref[pl.ds(r, CAST_ROWS), :] = w_ref[pl.ds(r, CAST_ROWS), :].astype(BF16)
                return carry
            lax.fori_loop(0, w_ref.shape[0] // CAST_ROWS, body, 0)

    @pl.when(valid)
    def _compute():
        x = x_ref[...]
        if mode == "gated":
            a = jnp.dot(x, wb_refs[0][...], preferred_element_type=F32)
            b = jnp.dot(x, wb_refs[1][...], preferred_element_type=F32)
            o_ref[...] = (_silu(a) * b).astype(o_ref.dtype)
        else:
            acc = jnp.dot(x, wb_refs[0][...], preferred_element_type=F32)
            if mode == "resid":
                acc = res_ref[...] + acc
            o_ref[...] = acc.astype(o_ref.dtype)

    @pl.when(jnp.logical_not(valid))
    def _zero():
        o_ref[...] = jnp.zeros_like(o_ref)


def _grouped_matmul(x, ws, tile_grp, tile_first, n_valid, *, tm, tn, mode, out_dtype, res=None, name,
                    weight_buffers=2):
    m, k = x.shape
    n = ws[0].shape[2]
    assert m % tm == 0 and n % tn == 0 and k % CAST_ROWS == 0
    n_w = len(ws)

    def row_of(mi, nv):
        return jnp.minimum(mi, nv[0] - 1)

    in_specs = [pl.BlockSpec((tm, k), lambda ni, mi, g, f, nv: (row_of(mi, nv), 0))]
    for _ in ws:
        in_specs.append(pl.BlockSpec((None, k, tn), lambda ni, mi, g, f, nv: (g[row_of(mi, nv)], 0, ni),
                                     pipeline_mode=pl.Buffered(weight_buffers)))
    args = [x] + list(ws)
    if mode == "resid":
        in_specs.append(pl.BlockSpec((tm, tn), lambda ni, mi, g, f, nv: (row_of(mi, nv), ni)))
        args.append(res)
    grid_spec = pltpu.PrefetchScalarGridSpec(
        num_scalar_prefetch=3,
        grid=(n // tn, m // tm),
        in_specs=in_specs,
        out_specs=pl.BlockSpec((tm, tn), lambda ni, mi, g, f, nv: (mi, ni)),
        scratch_shapes=[pltpu.VMEM((k, tn), BF16) for _ in ws],
    )
    return pl.pallas_call(
        functools.partial(_mm_body, n_w=n_w, mode=mode),
        grid_spec=grid_spec,
        out_shape=jax.ShapeDtypeStruct((m, n), out_dtype),
        compiler_params=_cparams(("arbitrary", "arbitrary")),
        name=name,
    )(tile_grp, tile_first, n_valid, *args)


def _dense_matmul(x, w, layer, *, tm, tn, mode, out_dtype, res=None, w2=None, name):
    tiles = x.shape[0] // tm
    grp = jnp.full((tiles,), layer, I32)
    first = jnp.zeros((tiles,), I32).at[0].set(1)
    nv = jnp.full((1,), tiles, I32)
    ws = [w] if w2 is None else [w, w2]
    return _grouped_matmul(x, ws, grp, first, nv, tm=tm, tn=tn, mode=mode, out_dtype=out_dtype,
                           res=res, name=name)


_SLOT, _ISSUED, _DONE, _NEXT_GRP, _NEXT_COL, _HAS_NEXT = range(6)


def _stream_body(grp_ref, first_ref, nextg_ref, wrap_ref, nv_ref, x_ref, *refs, n_w, mode, step_rows):
    res_ref = refs[0] if mode == "resid" else None
    refs = refs[1:] if mode == "resid" else refs
    w_hbm = refs[:n_w]
    o_ref, wb_ref, stage_ref, st_ref, sem_ref = refs[n_w:n_w + 5]
    n = pl.program_id(0)
    m = pl.program_id(1)
    k, tn = wb_ref.shape[2], wb_ref.shape[3]
    n_steps = k // step_rows
    half = step_rows // 2
    valid = m < nv_ref[0]

    def pieces(step):
        row0 = pl.multiple_of(step * step_rows, 2 * SUBLANES)
        col = pl.multiple_of(st_ref[_NEXT_COL] * tn, LANES)
        out = []
        for j in range(n_w):
            for p in range(2):
                src = w_hbm[j].at[st_ref[_NEXT_GRP], pl.ds(row0 + p * half, half), pl.ds(col, tn)]
                out.append(pltpu.make_async_copy(src, stage_ref.at[j, pl.ds(p * half, half), :], sem_ref.at[0]))
        return out

    def start(step):
        for c in pieces(step):
            c.start(priority=1)

    def wait(step):
        for c in pieces(step):
            c.wait()

    def cast(slot, step):
        row0 = pl.multiple_of(step * step_rows, 2 * SUBLANES)
        for j in range(n_w):
            wb_ref[slot, j, pl.ds(row0, step_rows), :] = stage_ref[j].astype(BF16)

    def compute(slot):
        x = x_ref[...]
        if mode == "gated":
            a = jnp.dot(x, wb_ref[slot, 0], preferred_element_type=F32)
            b = jnp.dot(x, wb_ref[slot, 1], preferred_element_type=F32)
            o_ref[...] = (_silu(a) * b).astype(o_ref.dtype)
        else:
            acc = jnp.dot(x, wb_ref[slot, 0], preferred_element_type=F32)
            if mode == "resid":
                acc = res_ref[...] + acc
            o_ref[...] = acc.astype(o_ref.dtype)

    @pl.when(jnp.logical_and(valid, first_ref[m] == 1))
    def _switch():
        @pl.when(jnp.logical_and(n == 0, m == 0))
        def _boot():
            st_ref[_SLOT] = 1
            st_ref[_ISSUED] = 0
            st_ref[_DONE] = 0
            st_ref[_NEXT_GRP] = grp_ref[0]
            st_ref[_NEXT_COL] = 0

        def finish(step, carry):
            @pl.when(st_ref[_ISSUED] <= step)
            def _issue():
                start(step)
            wait(step)
            cast(1 - st_ref[_SLOT], step)
            return carry
        lax.fori_loop(st_ref[_DONE], n_steps, finish, 0)

        st_ref[_SLOT] = 1 - st_ref[_SLOT]
        st_ref[_ISSUED] = 0
        st_ref[_DONE] = 0
        st_ref[_NEXT_GRP] = nextg_ref[m]
        st_ref[_NEXT_COL] = n + wrap_ref[m]
        last_col = n == pl.num_programs(0) - 1
        st_ref[_HAS_NEXT] = jnp.where(jnp.logical_and(wrap_ref[m] == 1, last_col), 0, 1)

    slot = st_ref[_SLOT]
    issued = st_ref[_ISSUED]
    done = st_ref[_DONE]
    in_flight = done < issued
    more = jnp.logical_and(st_ref[_HAS_NEXT] == 1, issued < n_steps)

    for do_cast in (False, True):
        for do_start in (False, True):
            @pl.when(jnp.logical_and(valid, jnp.logical_and(in_flight == do_cast, more == do_start)))
            def _step(do_cast=do_cast, do_start=do_start):
                if do_cast:
                    wait(done)
                    cast(1 - slot, done)
                    st_ref[_DONE] = done + 1
                compute(slot)
                if do_start:
                    start(issued)
                    st_ref[_ISSUED] = issued + 1

    @pl.when(jnp.logical_not(valid))
    def _zero():
        o_ref[...] = jnp.zeros_like(o_ref)


def _stream_matmul(x, ws, tile_grp, tile_first, tile_next, tile_wrap, n_valid, *, tm, tn, mode, out_dtype, name,
                   step_rows, res=None):
    m, k = x.shape
    n = ws[0].shape[2]
    assert m % tm == 0 and n % tn == 0 and tn % LANES == 0
    assert k % step_rows == 0 and step_rows % (4 * SUBLANES) == 0
    n_w = len(ws)
    row_of = lambda mi, nv: jnp.minimum(mi, nv[0] - 1)
    extra_specs, extra_args = [], []
    if mode == "resid":
        extra_specs = [pl.BlockSpec((tm, tn), lambda ni, mi, g, f, nx, wr, nv: (row_of(mi, nv), ni))]
        extra_args = [res]
    grid_spec = pltpu.PrefetchScalarGridSpec(
        num_scalar_prefetch=5,
        grid=(n // tn, m // tm),
        in_specs=[pl.BlockSpec((tm, k), lambda ni, mi, g, f, nx, wr, nv: (row_of(mi, nv), 0))]
                 + extra_specs + [pl.BlockSpec(memory_space=pl.ANY)] * n_w,
        out_specs=pl.BlockSpec((tm, tn), lambda ni, mi, g, f, nx, wr, nv: (mi, ni)),
        scratch_shapes=[pltpu.VMEM((2, n_w, k, tn), BF16), pltpu.VMEM((n_w, step_rows, tn), F32),
                        pltpu.SMEM((8,), I32), pltpu.SemaphoreType.DMA((1,))],
    )
    return pl.pallas_call(
        functools.partial(_stream_body, n_w=n_w, mode=mode, step_rows=step_rows),
        grid_spec=grid_spec,
        out_shape=jax.ShapeDtypeStruct((m, n), out_dtype),
        compiler_params=_cparams(("arbitrary", "arbitrary"), vmem=60 * 1024 * 1024),
        name=name,
    )(tile_grp, tile_first, tile_next, tile_wrap, n_valid, x, *extra_args, *ws)


def _dense_stream_matmul(x, w, layer, *, tm, tn, mode, out_dtype, res=None, w2=None, name):
    tiles = x.shape[0] // tm
    grp = jnp.full((tiles,), layer, I32)
    first = jnp.zeros((tiles,), I32).at[0].set(1)
    wrap = jnp.ones((tiles,), I32)
    nv = jnp.full((1,), tiles, I32)
    ws = [w] if w2 is None else [w, w2]
    step_rows = _stream_step_rows(w.shape[1], tiles - 1)
    return _stream_matmul(x, ws, grp, first, grp, wrap, nv, tm=tm, tn=tn, mode=mode, out_dtype=out_dtype,
                          res=res, name=name, step_rows=step_rows)


def _stream_step_rows(k, steps_available):
    best = k
    for steps in range(1, max(steps_available, 1) + 1):
        if k % steps == 0 and (k // steps) % (4 * SUBLANES) == 0:
            best = k // steps
    return best


def _mix_prompt_body(z_ref, caw_ref, cbw_ref, cbb_ref, lbg_ref, lbb_ref, lcg_ref, lcb_ref, spw_ref,
                     spbt_ref, pw_ref, ps_ref, y_ref, na_ref, nb_ref, nd_ref,
                     qa_ext, gb_ext, pd_ext, vn_buf, wm_ref, gsh_ref, *, rows, width, ka, kb, nbuf_d):
    i = pl.program_id(1)
    last = pl.num_programs(1) - 1
    w = width
    gw = w // len(POOL_WINDOWS)
    chunk = wm_ref.shape[1]
    cg = wm_ref.shape[0]
    cgw = w // cg
    sub = 32

    @pl.when(i == 0)
    def _init():
        zeros = jnp.zeros((HIST, w), F32)
        qa_ext[0:HIST, :] = zeros
        gb_ext[0:HIST, :] = zeros
        pd_ext[0:HIST, :] = zeros
        tri = lax.broadcasted_iota(I32, (chunk, chunk), 0) >= lax.broadcasted_iota(I32, (chunk, chunk), 1)
        for g in range(cg):
            wm_ref[g] = jnp.where(tri, spw_ref[g], 0.0).astype(BF16)

    def col(k):
        return z_ref[:, k * w:(k + 1) * w].astype(F32)

    qa_ext[HIST:HIST + rows, :] = col(0) * col(1)
    gb_ext[HIST:HIST + rows, :] = col(3) * jax.nn.sigmoid(col(4))
    pd_ext[HIST:HIST + rows, :] = col(7)
    vn_buf[...] = _layernorm(col(6), lcg_ref[...], lcb_ref[...]).astype(BF16)

    conv_a = None
    for j in range(ka):
        term = caw_ref[j:j + 1, :] * qa_ext[HIST - (ka - 1) + j:HIST - (ka - 1) + j + rows, :]
        conv_a = term if conv_a is None else conv_a + term
    y_ref[:, 0:w] = (col(2) * conv_a).astype(y_ref.dtype)

    for f in range(1, SUBLANES):
        gsh_ref[f - 1, SUBLANES:HIST + rows, :] = gb_ext[SUBLANES - f:HIST + rows - f, :]
    for r0 in range(0, rows, sub):
        acc = None
        for j in range(kb):
            back = kb - 1 - j
            f, s = back % SUBLANES, HIST + r0 - (back - back % SUBLANES)
            src = gb_ext[s:s + sub, :] if f == 0 else gsh_ref[f - 1, s:s + sub, :]
            term = cbw_ref[j:j + 1, :] * src
            acc = term if acc is None else acc + term
        yb = _silu(_layernorm(acc + cbb_ref[...], lbg_ref[...], lbb_ref[...]))
        y_ref[r0:r0 + sub, w:2 * w] = yb.astype(y_ref.dtype)

    for c0 in range(0, rows, chunk):
        for g in range(cg):
            lanes = slice(g * cgw, (g + 1) * cgw)
            s = jnp.dot(wm_ref[g], vn_buf[c0:c0 + chunk, lanes], preferred_element_type=F32)
            s = s + spbt_ref[:, g:g + 1]
            u = z_ref[c0:c0 + chunk, 5 * w + g * cgw:5 * w + (g + 1) * cgw].astype(F32)
            y_ref[c0:c0 + chunk, 2 * w + g * cgw:2 * w + (g + 1) * cgw] = (u * s).astype(y_ref.dtype)

    pos = i * rows + lax.broadcasted_iota(I32, (rows, 1), 0)
    for gi, win in enumerate(POOL_WINDOWS):
        lanes = slice(gi * gw, (gi + 1) * gw)
        tot = None
        for k in range(win):
            term = pd_ext[HIST - k:HIST - k + rows, lanes]
            tot = term if tot is None else tot + term
        cnt = jnp.minimum(pos + 1, win).astype(F32)
        d = tot / cnt - pd_ext[HIST:HIST + rows, lanes]
        yd = jnp.dot(d.astype(BF16), pw_ref[gi].astype(BF16), preferred_element_type=F32)
        y_ref[:, 3 * w + gi * gw:3 * w + (gi + 1) * gw] = (yd * ps_ref[:, lanes]).astype(y_ref.dtype)

    @pl.when(i == last)
    def _state():
        na_ref[...] = qa_ext[HIST + rows - (ka - 1):HIST + rows, :]
        nb_ref[...] = gb_ext[HIST + rows - (kb - 1):HIST + rows, :]
        nd_ref[...] = pd_ext[HIST + rows - nbuf_d:HIST + rows, :]

    qa_ext[0:HIST, :] = qa_ext[rows:rows + HIST, :]
    gb_ext[0:HIST, :] = gb_ext[rows:rows + HIST, :]
    pd_ext[0:HIST, :] = pd_ext[rows:rows + HIST, :]


def _mix_prompt(z, lw, *, batch, seq, rows):
    w = lw["conv_a_w"].shape[1]
    ka = lw["conv_a_w"].shape[0]
    kb = lw["conv_b_w"].shape[0]
    nbuf_d = max(POOL_WINDOWS) - 1
    cg, chunk = lw["sp_w"].shape[0], lw["sp_w"].shape[1]
    assert seq % rows == 0 and rows % chunk == 0 and kb - 1 <= HIST and nbuf_d <= HIST
    nt = seq // rows
    full2 = lambda a: pl.BlockSpec(a.shape, lambda b, i: (0, 0))
    full3 = lambda a: pl.BlockSpec(a.shape, lambda b, i: (0, 0, 0))
    vec = lambda a: a.reshape(1, w)
    small = [lw["conv_a_w"], lw["conv_b_w"], vec(lw["conv_b_b"]), vec(lw["ln_b_g"]), vec(lw["ln_b_b"]),
             vec(lw["ln_c_g"]), vec(lw["ln_c_b"])]
    spbt = lw["sp_b"].T
    in_specs = ([pl.BlockSpec((rows, 8 * w), lambda b, i: (b * nt + i, 0))]
                + [full2(a) for a in small]
                + [full3(lw["sp_w"]), full2(spbt), full3(lw["pool_w"]), full2(vec(lw["pool_scale"]))])
    out_shapes = (jax.ShapeDtypeStruct((batch * seq, 4 * w), BF16),
                  jax.ShapeDtypeStruct((batch, ka - 1, w), F32),
                  jax.ShapeDtypeStruct((batch, kb - 1, w), F32),
                  jax.ShapeDtypeStruct((batch, nbuf_d, w), F32))
    out_specs = (pl.BlockSpec((rows, 4 * w), lambda b, i: (b * nt + i, 0)),
                 pl.BlockSpec((None, ka - 1, w), lambda b, i: (b, 0, 0)),
                 pl.BlockSpec((None, kb - 1, w), lambda b, i: (b, 0, 0)),
                 pl.BlockSpec((None, nbuf_d, w), lambda b, i: (b, 0, 0)))
    scratch = [pltpu.VMEM((HIST + rows, w), F32), pltpu.VMEM((HIST + rows, w), F32),
               pltpu.VMEM((HIST + rows, w), F32), pltpu.VMEM((rows, w), BF16),
               pltpu.VMEM((cg, chunk, chunk), BF16),
               pltpu.VMEM((SUBLANES - 1, HIST + rows, w), F32)]
    return pl.pallas_call(
        functools.partial(_mix_prompt_body, rows=rows, width=w, ka=ka, kb=kb, nbuf_d=nbuf_d),
        grid=(batch, nt),
        in_specs=in_specs,
        out_specs=out_specs,
        out_shape=out_shapes,
        scratch_shapes=scratch,
        compiler_params=_cparams(("arbitrary", "arbitrary")),
        name="mix_prompt",
    )(z, *small, lw["sp_w"], spbt, lw["pool_w"], vec(lw["pool_scale"]))


def _mix_sample_body(spw_ref, spb_ref, z_ref, sa_ref, sb_ref, sd_ref, caw_ref, cbw_ref, cbb_ref,
                     lbg_ref, lbb_ref, lcg_ref, lcb_ref, pw_ref, ps_ref,
                     y_ref, na_ref, nb_ref, nd_ref, cv_ref, *, nseq, steps, width, ka, kb, nbuf_d, cg):
    w = width
    gw = w // len(POOL_WINDOWS)
    cgw = w // cg
    lane = lax.broadcasted_iota(I32, (1, w), 1)

    def col(t, k):
        return z_ref[t * nseq:(t + 1) * nseq, k * w:(k + 1) * w].astype(F32)

    def put(t, k, val):
        y_ref[t * nseq:(t + 1) * nseq, k * w:(k + 1) * w] = val.astype(y_ref.dtype)

    ext = [sa_ref[k] for k in range(ka - 1)] + [col(t, 0) * col(t, 1) for t in range(steps)]
    for t in range(steps):
        conv = None
        for j in range(ka):
            term = caw_ref[j:j + 1, :] * ext[t + j]
            conv = term if conv is None else conv + term
        put(t, 0, col(t, 2) * conv)
    for k in range(ka - 1):
        na_ref[k] = ext[steps + k]

    for t in range(steps):
        nb_ref[kb - 1 - steps + t] = col(t, 3) * jax.nn.sigmoid(col(t, 4))
    for k in range(kb - 1 - steps):
        nb_ref[k] = sb_ref[k + steps]

    def ext_b(k):
        return sb_ref[k] if k < kb - 1 else nb_ref[k - steps]

    for t in range(steps):
        acc = None
        for j in range(kb):
            term = cbw_ref[j:j + 1, :] * ext_b(t + j)
            acc = term if acc is None else acc + term
        put(t, 1, _silu(_layernorm(acc + cbb_ref[...], lbg_ref[...], lbb_ref[...])))

    def lane_groups(vals):
        out = jnp.full((1, w), vals[cg - 1], F32)
        for g in range(cg - 2, -1, -1):
            out = jnp.where(lane < (g + 1) * cgw, vals[g], out)
        return out

    for t in range(steps):
        cv_ref[t] = _layernorm(col(t, 6), lcg_ref[...], lcb_ref[...])
    for t in range(steps):
        s = lane_groups([spb_ref[g, t] for g in range(cg)])
        for sp in range(t + 1):
            s = s + lane_groups([spw_ref[g, t, sp] for g in range(cg)]) * cv_ref[sp]
        put(t, 2, col(t, 5) * s)

    for t in range(steps):
        nd_ref[nbuf_d - steps + t] = col(t, 7)
    for k in range(nbuf_d - steps):
        nd_ref[k] = sd_ref[k + steps]

    def ext_d(k):
        return sd_ref[k] if k < nbuf_d else nd_ref[k - steps]

    for t in range(steps):
        for gi, win in enumerate(POOL_WINDOWS):
            lanes = slice(gi * gw, (gi + 1) * gw)
            tot = None
            for k in range(win):
                term = ext_d(nbuf_d + t - k)[:, lanes]
                tot = term if tot is None else tot + term
            cnt = float(min(PAST_LEN + t + 1, win))
            d = tot / cnt - ext_d(nbuf_d + t)[:, lanes]
            yd = jnp.dot(d.astype(BF16), pw_ref[gi].astype(BF16), preferred_element_type=F32)
            y_ref[t * nseq:(t + 1) * nseq, 3 * w + gi * gw:3 * w + (gi + 1) * gw] = (
                yd * ps_ref[:, lanes]).astype(y_ref.dtype)


def _mix_sample(z, sa, sb, sd, lw, *, nseq, steps, m_prompt):
    w = lw["conv_a_w"].shape[1]
    ka = lw["conv_a_w"].shape[0]
    kb = lw["conv_b_w"].shape[0]
    nbuf_d = max(POOL_WINDOWS) - 1
    cg = lw["sp_w"].shape[0]
    ms = nseq * steps
    assert m_prompt % ms == 0 and steps <= nbuf_d and steps <= kb - 1
    rb = m_prompt // ms
    spw_small = lw["sp_w"][:, :steps, :steps]
    spb_small = lw["sp_b"][:, :steps]
    vec = lambda a: a.reshape(1, w)
    smem = pl.BlockSpec(memory_space=pltpu.SMEM)
    full2 = lambda a: pl.BlockSpec(a.shape, lambda i: (0, 0))
    full3 = lambda a: pl.BlockSpec(a.shape, lambda i: (0, 0, 0))
    small = [lw["conv_a_w"], lw["conv_b_w"], vec(lw["conv_b_b"]), vec(lw["ln_b_g"]), vec(lw["ln_b_b"]),
             vec(lw["ln_c_g"]), vec(lw["ln_c_b"])]
    in_specs = ([smem, smem,
                 pl.BlockSpec((ms, 8 * w), lambda i: (rb, 0)),
                 full3(sa), full3(sb), full3(sd)]
                + [full2(a) for a in small]
                + [full3(lw["pool_w"]), full2(vec(lw["pool_scale"]))])
    out_shapes = (jax.ShapeDtypeStruct((ms, 4 * w), BF16),
                  jax.ShapeDtypeStruct(sa.shape, F32), jax.ShapeDtypeStruct(sb.shape, F32),
                  jax.ShapeDtypeStruct(sd.shape, F32), jax.ShapeDtypeStruct((steps, nseq, w), F32))
    out_specs = (pl.BlockSpec((ms, 4 * w), lambda i: (0, 0)),
                 full3(sa), full3(sb), full3(sd),
                 pl.BlockSpec((steps, nseq, w), lambda i: (0, 0, 0)))
    return pl.pallas_call(
        functools.partial(_mix_sample_body, nseq=nseq, steps=steps, width=w, ka=ka, kb=kb,
                          nbuf_d=nbuf_d, cg=cg),
        grid=(1,),
        in_specs=in_specs,
        out_specs=out_specs,
        out_shape=out_shapes,
        compiler_params=_cparams(("arbitrary",)),
        name="mix_sample",
    )(spw_small, spb_small, z, sa, sb, sd, *small, lw["pool_w"], vec(lw["pool_scale"]))


def _merge_out_body(yp_ref, ys_ref, g0_ref, g1_ref, g2_ref, g3_ref, x_ref, gn_ref, wa_hbm, wb_hbm, wc_hbm, wd_hbm,
                    wo_hbm, *refs, width, prompt_tiles, layer, with_h):
    n_out = 2 if with_h else 1
    x1_ref = refs[0]
    wbr_ref, wout_ref, stage_ref, sem_ref = refs[n_out:n_out + 4]
    i = pl.program_id(0)
    rows = stage_ref.shape[1]
    d = wout_ref.shape[0]

    @pl.when(i == 0)
    def _load_weights():
        chunks = []
        for k, w_hbm in enumerate((wa_hbm, wb_hbm, wc_hbm, wd_hbm)):
            for r in range(0, width, rows):
                chunks.append((w_hbm.at[layer, pl.ds(r, rows), :], wbr_ref.at[k, pl.ds(r, rows), :]))
        for r in range(0, d, rows):
            chunks.append((wo_hbm.at[layer, pl.ds(r, rows), :], wout_ref.at[pl.ds(r, rows), :]))

        def fetch(c):
            return pltpu.make_async_copy(chunks[c][0], stage_ref.at[c % 2], sem_ref.at[c % 2])

        fetch(0).start(priority=0)
        for c in range(len(chunks)):
            if c + 1 < len(chunks):
                fetch(c + 1).start(priority=(c + 1) % 2)
            fetch(c).wait()
            chunks[c][1][...] = stage_ref[c % 2].astype(BF16)

    def run(y_ref):
        acc = None
        for k, g_ref in enumerate((g0_ref, g1_ref, g2_ref, g3_ref)):
            br = jnp.dot(y_ref[:, k * width:(k + 1) * width], wbr_ref[k], preferred_element_type=F32)
            term = jax.nn.sigmoid(g_ref[...].astype(F32)) * br
            acc = term if acc is None else acc + term
        x1 = x_ref[...] + jnp.dot(acc.astype(BF16), wout_ref[...], preferred_element_type=F32)
        x1_ref[...] = x1
        if with_h:
            refs[1][...] = _rms(x1, gn_ref[...]).astype(BF16)

    @pl.when(i < prompt_tiles)
    def _prompt():
        run(yp_ref)

    @pl.when(i >= prompt_tiles)
    def _sample():
        run(ys_ref)


def _merge_out(y_p, y_s, z, x, w_brs, w_out, layer, g, *, tm, with_h):
    mp, yw = y_p.shape
    ms = y_s.shape[0]
    assert mp % tm == 0 and ms % tm == 0
    tp = mp // tm
    w = yw // 4
    d = w_out.shape[2]
    assert (2 * yw) % d == 0 and w % STAGE_ROWS == 0 and d % STAGE_ROWS == 0
    goff = (2 * yw) // d
    gate_spec = lambda k: pl.BlockSpec((tm, d), lambda i, k=k: (i, goff + k))
    row = pl.BlockSpec((tm, d), lambda i: (i, 0))
    hbm = pl.BlockSpec(memory_space=pl.ANY)
    out_shape = [jax.ShapeDtypeStruct((mp + ms, d), F32)]
    out_specs = [row]
    if with_h:
        out_shape.append(jax.ShapeDtypeStruct((mp + ms, d), BF16))
        out_specs.append(row)
    return pl.pallas_call(
        functools.partial(_merge_out_body, width=w, prompt_tiles=tp, layer=layer, with_h=with_h),
        grid=((mp + ms) // tm,),
        in_specs=[pl.BlockSpec((tm, yw), lambda i: (jnp.minimum(i, tp - 1), 0)),
                  pl.BlockSpec((tm, yw), lambda i: (jnp.maximum(i - tp, 0), 0))]
                 + [gate_spec(k) for k in range(4)]
                 + [row, pl.BlockSpec((1, d), lambda i: (0, 0))] + [hbm] * 5,
        out_specs=out_specs,
        out_shape=out_shape,
        scratch_shapes=[pltpu.VMEM((4, w, d), BF16), pltpu.VMEM((d, d), BF16),
                        pltpu.VMEM((2, STAGE_ROWS, d), F32), pltpu.SemaphoreType.DMA((2,))],
        compiler_params=_cparams(("arbitrary",)),
        name="merge_out",
    )(y_p, y_s, z, z, z, z, x, g.reshape(1, d), *w_brs, w_out)


def _route_body(x_ref, g_ref, rwt_ref, e_ref, r_ref, wt_ref, cnt_ref, carry_ref, tri_ref):
    i = pl.program_id(0)
    ne = rwt_ref.shape[0]
    tm = x_ref.shape[0]

    @pl.when(i == 0)
    def _init():
        carry_ref[...] = jnp.zeros_like(carry_ref)
        upper = lax.broadcasted_iota(I32, (tm, tm), 0) <= lax.broadcasted_iota(I32, (tm, tm), 1)
        tri_ref[...] = jnp.where(upper, 1.0, 0.0).astype(BF16)

    h = _rms(x_ref[...], g_ref[...])
    h_hi = h.astype(BF16)
    h_lo = (h - h_hi.astype(F32)).astype(BF16)
    rw = rwt_ref[...]
    rw_hi = rw.astype(BF16)
    rw_lo = (rw - rw_hi.astype(F32)).astype(BF16)
    dn = (((1,), (1,)), ((), ()))
    dg = lambda a, b: lax.dot_general(a, b, dn, preferred_element_type=F32)
    logits = dg(rw_hi, h_hi) + dg(rw_hi, h_lo) + dg(rw_lo, h_hi)
    ex = jnp.exp(logits - jnp.max(logits, axis=0, keepdims=True))
    p = ex / jnp.sum(ex, axis=0, keepdims=True)
    eid = lax.broadcasted_iota(I32, (ne, tm), 0)
    m1 = jnp.max(p, axis=0, keepdims=True)
    i1 = jnp.min(jnp.where(p == m1, eid, ne), axis=0, keepdims=True)
    oh1 = eid == i1
    p2 = jnp.where(oh1, -1.0, p)
    m2 = jnp.max(p2, axis=0, keepdims=True)
    i2 = jnp.min(jnp.where(p2 == m2, eid, ne), axis=0, keepdims=True)
    oh2 = eid == i2
    den = m1 + m2
    sel = jnp.logical_or(oh1, oh2)
    sel_f = jnp.where(sel, 1.0, 0.0)
    cum = jnp.dot(sel_f.astype(BF16), tri_ref[...], preferred_element_type=F32) + carry_ref[...]
    rank1 = jnp.sum(jnp.where(oh1, cum, 0.0), axis=0, keepdims=True) - 1.0
    rank2 = jnp.sum(jnp.where(oh2, cum, 0.0), axis=0, keepdims=True) - 1.0
    carry_ref[...] = carry_ref[...] + jnp.sum(sel_f, axis=1, keepdims=True)
    e_ref[0:1, :] = i1
    e_ref[1:2, :] = i2
    r_ref[0:1, :] = rank1.astype(I32)
    r_ref[1:2, :] = rank2.astype(I32)
    wt_ref[0:1, :] = m1 / den
    wt_ref[1:2, :] = m2 / den

    @pl.when(i == pl.num_programs(0) - 1)
    def _counts():
        cnt_ref[...] = jnp.broadcast_to(carry_ref[...], cnt_ref.shape).astype(I32)


def _route(x1, g, router_w, *, tm):
    m, d = x1.shape
    ne = router_w.shape[1]
    tok = lambda dt: jax.ShapeDtypeStruct((TOP_K, m), dt)
    tok_spec = pl.BlockSpec((TOP_K, tm), lambda i: (0, i))
    return pl.pallas_call(
        _route_body,
        grid=(m // tm,),
        in_specs=[pl.BlockSpec((tm, d), lambda i: (i, 0)),
                  pl.BlockSpec((1, d), lambda i: (0, 0)),
                  pl.BlockSpec((ne, d), lambda i: (0, 0))],
        out_specs=(tok_spec, tok_spec, tok_spec, pl.BlockSpec((ne, LANES), lambda i: (0, 0))),
        out_shape=(tok(I32), tok(I32), tok(F32), jax.ShapeDtypeStruct((ne, LANES), I32)),
        scratch_shapes=[pltpu.VMEM((ne, 1), F32), pltpu.VMEM((tm, tm), BF16)],
        compiler_params=_cparams(("arbitrary",)),
        name="route",
    )(x1, g.reshape(1, d), router_w.T)


def _plan_body(e_ref, r_ref, cnt_ref, pos_ref, tgrp_ref, tfirst_ref, misc_ref, *, tile_shift):
    ne = cnt_ref.shape[0]
    tile = 1 << tile_shift
    sub = lax.broadcasted_iota(I32, (ne, LANES), 0)
    lane = lax.broadcasted_iota(I32, (ne, LANES), 1)

    def as_row(col):
        return jnp.sum(jnp.where(sub == lane, col, 0), axis=0, keepdims=True)

    cnt = cnt_ref[:, 0:1]
    padded = lax.shift_left(lax.shift_right_logical(cnt + (tile - 1), tile_shift), tile_shift)
    gstart = jnp.sum(jnp.where(lane < sub, as_row(padded), 0), axis=1, keepdims=True)
    gend = gstart + padded
    m = e_ref.shape[1]
    eid = lax.broadcasted_iota(I32, (ne, m), 0)
    for k in range(TOP_K):
        base = jnp.sum(jnp.where(eid == e_ref[k:k + 1, :], gstart, 0), axis=0, keepdims=True)
        pos_ref[k:k + 1, :] = base + r_ref[k:k + 1, :]
    start = lax.broadcasted_iota(I32, (1, LANES), 1) * tile
    grp = jnp.sum((gend <= start).astype(I32), axis=0, keepdims=True)
    tgrp_ref[...] = jnp.minimum(grp, ne - 1)
    tfirst_ref[...] = jnp.sum(jnp.logical_and(gstart == start, padded > 0).astype(I32), axis=0, keepdims=True)
    n_tiles = lax.shift_right_logical(jnp.max(gend, axis=0, keepdims=True), tile_shift)
    misc = jnp.where(sub == 0, n_tiles, 0)
    misc = jnp.where(sub == 1, as_row(gstart + cnt), misc)
    misc = jnp.where(sub == 2, as_row(gend), misc)
    lane_f = lane.astype(F32)
    nonempty = jnp.logical_and(as_row(padded) > 0, lane < ne)
    later = jnp.min(jnp.where(jnp.logical_and(nonempty, lane > sub), lane_f, float(ne)), axis=1, keepdims=True)
    firstg = jnp.min(jnp.where(nonempty, lane_f, float(ne)), axis=1, keepdims=True)
    nxt = jnp.where(later < ne, later, firstg).astype(I32)
    wraps = (later >= ne).astype(I32)
    mine = sub == jnp.minimum(grp, ne - 1)
    misc = jnp.where(sub == 3, jnp.sum(jnp.where(mine, nxt, 0), axis=0, keepdims=True), misc)
    misc = jnp.where(sub == 4, jnp.sum(jnp.where(mine, wraps, 0), axis=0, keepdims=True), misc)
    misc_ref[...] = misc


def _plan(e, r, cnt, *, tile):
    tile_shift = tile.bit_length() - 1
    assert 1 << tile_shift == tile
    ne = cnt.shape[0]
    m = e.shape[1]
    whole = lambda a: pl.BlockSpec(a.shape, lambda i: (0, 0))
    row = jax.ShapeDtypeStruct((1, LANES), I32)
    out_shape = (jax.ShapeDtypeStruct((TOP_K, m), I32), row, row, jax.ShapeDtypeStruct((ne, LANES), I32))
    return pl.pallas_call(
        functools.partial(_plan_body, tile_shift=tile_shift),
        grid=(1,),
        in_specs=[whole(e), whole(r), whole(cnt)],
        out_specs=tuple(pl.BlockSpec(s.shape, lambda i: (0, 0)) for s in out_shape),
        out_shape=out_shape,
        compiler_params=_cparams(("arbitrary",)),
        name="plan",
    )(e, r, cnt)


def _dispatch_body(pos0_ref, pos1_ref, nv_ref, fill_ref, end_ref, x_hbm, g_ref, o_ref, src_ref, buf_ref, sem_ref):
    j = pl.program_id(0)
    tm = o_ref.shape[0]
    n_tok = pos0_ref.shape[0]
    nv = nv_ref[0]

    def start_tile(tile, slot):
        def body(pair, carry):
            for p in range(2):
                r = pair * 2 + p
                tok = src_ref[tile * tm + r]
                pltpu.make_async_copy(x_hbm.at[pl.ds(tok, 1), :], buf_ref.at[slot, pl.ds(r, 1), :],
                                      sem_ref.at[slot]).start(priority=p)
            return carry
        lax.fori_loop(0, tm // 2, body, 0, unroll=ISSUE_UNROLL // 2)

    @pl.when(j == 0)
    def _prologue():
        for g in range(fill_ref.shape[0]):
            def fill(p, carry):
                src_ref[p] = 0
                return carry
            lax.fori_loop(fill_ref[g], end_ref[g], fill, 0)

        def inv(t, carry):
            src_ref[pos0_ref[t]] = t
            src_ref[pos1_ref[t]] = t
            return carry
        lax.fori_loop(0, n_tok, inv, 0, unroll=ISSUE_UNROLL)
        start_tile(0, 0)

    slot = lax.rem(j, 2)

    @pl.when(j < nv)
    def _work():
        @pl.when(j + 1 < nv)
        def _prefetch():
            start_tile(j + 1, 1 - slot)

        pltpu.make_async_copy(x_hbm.at[pl.ds(0, tm), :], buf_ref.at[slot], sem_ref.at[slot]).wait()
        o_ref[...] = _rms(buf_ref[slot], g_ref[...]).astype(o_ref.dtype)

    @pl.when(j >= nv)
    def _skip():
        o_ref[...] = jnp.zeros_like(o_ref)


def _dispatch(x1, g, pos0, pos1, n_tiles, fill_row, end_row, *, tm, max_tiles):
    m, d = x1.shape
    grid_spec = pltpu.PrefetchScalarGridSpec(
        num_scalar_prefetch=5,
        grid=(max_tiles,),
        in_specs=[pl.BlockSpec(memory_space=pl.ANY),
                  pl.BlockSpec((1, d), lambda j, *_: (0, 0))],
        out_specs=pl.BlockSpec((tm, d), lambda j, *_: (j, 0)),
        scratch_shapes=[pltpu.SMEM((max_tiles * tm,), I32),
                        pltpu.VMEM((2, tm, d), F32),
                        pltpu.SemaphoreType.DMA((2,))],
    )
    return pl.pallas_call(
        _dispatch_body,
        grid_spec=grid_spec,
        out_shape=jax.ShapeDtypeStruct((max_tiles * tm, d), BF16),
        compiler_params=_cparams(("arbitrary",)),
        name="dispatch",
    )(pos0, pos1, n_tiles, fill_row, end_row, x1, g.reshape(1, d))


def _combine_body(pos0_ref, pos1_ref, ys_hbm, x_ref, wt_ref, g_ref, xn_ref, h_ref, buf_ref, sem_ref):
    i = pl.program_id(0)
    n = pl.num_programs(0)
    tm = x_ref.shape[0]

    def start_tile(tile, slot):
        def body(r, carry):
            t = tile * tm + r
            pltpu.make_async_copy(ys_hbm.at[pl.ds(pos0_ref[t], 1), :], buf_ref.at[slot, 0, pl.ds(r, 1), :],
                                  sem_ref.at[slot, 0]).start(priority=0)
            pltpu.make_async_copy(ys_hbm.at[pl.ds(pos1_ref[t], 1), :], buf_ref.at[slot, 1, pl.ds(r, 1), :],
                                  sem_ref.at[slot, 1]).start(priority=1)
            return carry
        lax.fori_loop(0, tm, body, 0, unroll=ISSUE_UNROLL)

    @pl.when(i == 0)
    def _prologue():
        start_tile(0, 0)

    slot = lax.rem(i, 2)

    @pl.when(i + 1 < n)
    def _prefetch():
        start_tile(i + 1, 1 - slot)

    for k in range(TOP_K):
        pltpu.make_async_copy(ys_hbm.at[pl.ds(0, tm), :], buf_ref.at[slot, k], sem_ref.at[slot, k]).wait()
    xn = x_ref[...] + wt_ref[:, 0:1] * buf_ref[slot, 0] + wt_ref[:, 1:2] * buf_ref[slot, 1]
    xn_ref[...] = xn
    h_ref[...] = _rms(xn, g_ref[...]).astype(h_ref.dtype)


def _combine(ys, x1, wt_cols, g, pos0, pos1, *, tm, h_dtype):
    m, d = x1.shape
    row = lambda: pl.BlockSpec((tm, d), lambda i, *_: (i, 0))
    grid_spec = pltpu.PrefetchScalarGridSpec(
        num_scalar_prefetch=2,
        grid=(m // tm,),
        in_specs=[pl.BlockSpec(memory_space=pl.ANY), row(),
                  pl.BlockSpec((tm, TOP_K), lambda i, *_: (i, 0)),
                  pl.BlockSpec((1, d), lambda i, *_: (0, 0))],
        out_specs=(row(), row()),
        scratch_shapes=[pltpu.VMEM((2, TOP_K, tm, d), F32), pltpu.SemaphoreType.DMA((2, TOP_K))],
    )
    return pl.pallas_call(
        _combine_body,
        grid_spec=grid_spec,
        out_shape=(jax.ShapeDtypeStruct((m, d), F32), jax.ShapeDtypeStruct((m, d), h_dtype)),
        compiler_params=_cparams(("arbitrary",)),
        name="combine",
    )(pos0, pos1, ys, x1, wt_cols, g.reshape(1, d))


def _pick(total, prefs):
    for p in prefs:
        if total % p == 0:
            return p
    raise ValueError(f"no tile for {total} in {prefs}")


def _row_tile(m, cap):
    best = None
    for t in range(16, cap + 1, 16):
        if m % t == 0:
            best = t
    assert best is not None
    return best


def kernel(x_prompt, x_sample, state_conv_a, state_conv_b, state_pool, norm_mix_g, w_in, conv_a_w, w_br_a,
           conv_b_w, conv_b_b, ln_b_g, ln_b_b, w_br_b, ln_c_g, ln_c_b, sp_w, sp_b, w_br_c, pool_w,
           pool_scale, w_br_d, w_out, norm_ffn_g, ffn_w1, ffn_w3, ffn_w2, router_w, moe_w1, moe_w3, moe_w2,
           final_norm_g):
    batch, seq, d = x_prompt.shape
    nseq, steps, _ = x_sample.shape
    depth = w_in.shape[0]
    n_exp = router_w.shape[2]
    mp, ms = batch * seq, nseq * steps
    m = mp + ms

    x = jnp.concatenate([x_prompt.reshape(mp, d), jnp.swapaxes(x_sample, 0, 1).reshape(ms, d)], axis=0)
    st_a = jnp.swapaxes(state_conv_a, 1, 2)
    st_b = jnp.swapaxes(state_conv_b, 1, 2)
    st_d = jnp.swapaxes(state_pool, 1, 2)

    tm_big = _row_tile(m, 1088)
    tm_mid = _row_tile(m, 544)
    tm_small = _row_tile(m, 256)
    tm_merge = min(_row_tile(mp, 256), _row_tile(ms, 256))
    assert mp % tm_merge == 0 and ms % tm_merge == 0
    tm_moe = 512
    max_tiles = -(-(m * TOP_K) // tm_moe) + n_exp
    assert max_tiles <= LANES
    mix_rows = _pick(seq, (256, 128))
    moe_w1f = moe_w1.reshape((-1,) + moe_w1.shape[2:])
    moe_w3f = moe_w3.reshape((-1,) + moe_w3.shape[2:])
    moe_w2f = moe_w2.reshape((-1,) + moe_w2.shape[2:])
    dffe = moe_w1.shape[3]
    dff = ffn_w1.shape[2]

    h = _rmsnorm(x, norm_mix_g[0], BF16, tm_mid)
    outs_p = ([], [], [])
    outs_s = ([], [], [], [])
    y_final = None
    for i in range(depth):
        lw = dict(conv_a_w=conv_a_w[i], conv_b_w=conv_b_w[i], conv_b_b=conv_b_b[i], ln_b_g=ln_b_g[i],
                  ln_b_b=ln_b_b[i], ln_c_g=ln_c_g[i], ln_c_b=ln_c_b[i], sp_w=sp_w[i], sp_b=sp_b[i],
                  pool_w=pool_w[i], pool_scale=pool_scale[i])
        z = _dense_stream_matmul(h, w_in, i, tm=tm_big, tn=_pick(w_in.shape[2], (1024, 512, 256, 128)),
                                 mode="plain", out_dtype=BF16, name="inproj")
        y_p, na_p, nb_p, nd_p = _mix_prompt(z, lw, batch=batch, seq=seq, rows=mix_rows)
        y_s, na_s, nb_s, nd_s, cv_s = _mix_sample(z, st_a[i], st_b[i], st_d[i], lw, nseq=nseq, steps=steps,
                                                  m_prompt=mp)
        for acc, val in zip(outs_p, (na_p, nb_p, nd_p)):
            acc.append(val)
        for acc, val in zip(outs_s, (na_s, nb_s, nd_s, cv_s)):
            acc.append(val)
        merge_out = functools.partial(_merge_out, y_p, y_s, z, x, (w_br_a, w_br_b, w_br_c, w_br_d), w_out, i,
                                      norm_ffn_g[i], tm=tm_merge)
        last = i == depth - 1
        g_next = final_norm_g if last else norm_mix_g[i + 1]
        h_dtype = F32 if last else BF16
        j = i // 2
        if i % 2 == 0:
            x1, h2 = merge_out(with_h=True)
            a = _dense_stream_matmul(h2, ffn_w1, j, w2=ffn_w3, tm=tm_big, tn=_pick(dff, (512, 256, 128)),
                                     mode="gated", out_dtype=BF16, name="ffn_up")
            x = _dense_stream_matmul(a, ffn_w2, j, tm=tm_mid, tn=_pick(d, (512, 256, 128)), mode="resid",
                                     out_dtype=F32, res=x1, name="ffn_down")
            h = _rmsnorm(x, g_next, h_dtype, tm_mid)
        else:
            (x1,) = merge_out(with_h=False)
            e, r, wt, cnt = _route(x1, norm_ffn_g[i], router_w[j], tm=_pick(m, (512, 256, 128)))
            pos, tgrp, tfirst, misc = _plan(e, r, cnt, tile=tm_moe)
            n_tiles = misc[0, 0:1]
            hs = _dispatch(x1, norm_ffn_g[i], pos[0], pos[1], n_tiles, misc[1, :n_exp], misc[2, :n_exp],
                           tm=tm_moe, max_tiles=max_tiles)
            grp = tgrp[0, :max_tiles] + j * n_exp
            first = tfirst[0, :max_tiles]
            nextg = misc[3, :max_tiles] + j * n_exp
            wrap = misc[4, :max_tiles]
            a = _stream_matmul(hs, [moe_w1f, moe_w3f], grp, first, nextg, wrap, n_tiles, tm=tm_moe,
                               tn=_pick(dffe, (1408, 1024, 512, 256, 128)), mode="gated", out_dtype=BF16,
                               name="moe_up", step_rows=_stream_step_rows(d, MOE_STREAM_STEPS))
            ys = _stream_matmul(a, [moe_w2f], grp, first, nextg, wrap, n_tiles, tm=tm_moe,
                                tn=_pick(d, (1024, 512, 256, 128)), mode="plain", out_dtype=F32, name="moe_down",
                                step_rows=_stream_step_rows(dffe, MOE_STREAM_STEPS))
            x, h = _combine(ys, x1, wt.T, g_next, pos[0], pos[1], tm=tm_small, h_dtype=h_dtype)
        if last:
            y_final = h

    y_prompt = y_final[:mp].reshape(batch, seq, d)
    y_sample = jnp.swapaxes(y_final[mp:].reshape(steps, nseq, d), 0, 1)
    stack_s = lambda vals: jnp.swapaxes(jnp.stack(vals), 1, 2)
    return (y_prompt, y_sample, jnp.stack(outs_p[0]), jnp.stack(outs_p[1]), jnp.stack(outs_p[2]),
            stack_s(outs_s[0]), stack_s(outs_s[1]), stack_s(outs_s[2]), stack_s(outs_s[3]))
```

```python
import functools

import jax
import jax.numpy as jnp
from jax import lax
from jax.experimental import pallas as pl
from jax.experimental.pallas import tpu as pltpu

F32 = jnp.float32
BF16 = jnp.bfloat16
I32 = jnp.int32

EPS = 1e-6
PAST_LEN = 16384
POOL_WINDOWS = (2, 4, 8, 16)
TOP_K = 2
LANES = 128
SUBLANES = 8
HIST = 32
ISSUE_UNROLL = 8
VMEM_LIMIT = 56 * 1024 * 1024


def _cparams(sem, vmem=VMEM_LIMIT):
    return pltpu.CompilerParams(dimension_semantics=sem, vmem_limit_bytes=vmem)


def _rms(x, g):
    return x * lax.rsqrt(jnp.mean(x * x, axis=-1, keepdims=True) + EPS) * g


def _layernorm(x, g, b):
    mu = jnp.mean(x, axis=-1, keepdims=True)
    xc = x - mu
    var = jnp.mean(xc * xc, axis=-1, keepdims=True)
    return xc * lax.rsqrt(var + EPS) * g + b


def _silu(x):
    return x * jax.nn.sigmoid(x)


def _rmsnorm_body(x_ref, g_ref, o_ref):
    o_ref[...] = _rms(x_ref[...], g_ref[...]).astype(o_ref.dtype)


def _rmsnorm(x, g, out_dtype, tm):
    m, d = x.shape
    return pl.pallas_call(
        _rmsnorm_body,
        grid=(m // tm,),
        in_specs=[pl.BlockSpec((tm, d), lambda i: (i, 0)),
                  pl.BlockSpec((1, d), lambda i: (0, 0))],
        out_specs=pl.BlockSpec((tm, d), lambda i: (i, 0)),
        out_shape=jax.ShapeDtypeStruct((m, d), out_dtype),
        compiler_params=_cparams(("arbitrary",)),
        name="rmsnorm",
    )(x, g.reshape(1, d))


CAST_ROWS = 256
STAGE_ROWS = 256
MOE_STREAM_STEPS = 4


def _mm_body(grp_ref, first_ref, nv_ref, *refs, n_w, mode):
    del grp_ref
    x_ref = refs[0]
    w_refs = refs[1:1 + n_w]
    pos = 1 + n_w
    res_ref = None
    if mode == "resid":
        res_ref = refs[pos]
        pos += 1
    o_ref = refs[pos]
    wb_refs = refs[pos + 1:pos + 1 + n_w]
    m = pl.program_id(1)
    valid = m < nv_ref[0]

    @pl.when(jnp.logical_and(valid, first_ref[m] == 1))
    def _cast():
        for w_ref, wb_ref in zip(w_refs, wb_refs):
            def body(c, carry, w_ref=w_ref, wb_ref=wb_ref):
                r = pl.multiple_of(c * CAST_ROWS, CAST_ROWS)
                wb_ref[pl.ds(r, CAST_ROWS), :] = w_ref[pl.ds(r, CAST_ROWS), :].astype(BF16)
                return carry
            lax.fori_loop(0, w_ref.shape[0] // CAST_ROWS, body, 0)

    @pl.when(valid)
    def _compute():
        x = x_ref[...]
        if mode == "gated":
            a = jnp.dot(x, wb_refs[0][...], preferred_element_type=F32)
            b = jnp.dot(x, wb_refs[1][...], preferred_element_type=F32)
            o_ref[...] = (_silu(a) * b).astype(o_ref.dtype)
        else:
            acc = jnp.dot(x, wb_refs[0][...], preferred_element_type=F32)
            if mode == "resid":
                acc = res_ref[...] + acc
            o_ref[...] = acc.astype(o_ref.dtype)

    @pl.when(jnp.logical_not(valid))
    def _zero():
        o_ref[...] = jnp.zeros_like(o_ref)


def _grouped_matmul(x, ws, tile_grp, tile_first, n_valid, *, tm, tn, mode, out_dtype, res=None, name,
                    weight_buffers=2):
    m, k = x.shape
    n = ws[0].shape[2]
    assert m % tm == 0 and n % tn == 0 and k % CAST_ROWS == 0
    n_w = len(ws)

    def row_of(mi, nv):
        return jnp.minimum(mi, nv[0] - 1)

    in_specs = [pl.BlockSpec((tm, k), lambda ni, mi, g, f, nv: (row_of(mi, nv), 0))]
    for _ in ws:
        in_specs.append(pl.BlockSpec((None, k, tn), lambda ni, mi, g, f, nv: (g[row_of(mi, nv)], 0, ni),
                                     pipeline_mode=pl.Buffered(weight_buffers)))
    args = [x] + list(ws)
    if mode == "resid":
        in_specs.append(pl.BlockSpec((tm, tn), lambda ni, mi, g, f, nv: (row_of(mi, nv), ni)))
        args.append(res)
    grid_spec = pltpu.PrefetchScalarGridSpec(
        num_scalar_prefetch=3,
        grid=(n // tn, m // tm),
        in_specs=in_specs,
        out_specs=pl.BlockSpec((tm, tn), lambda ni, mi, g, f, nv: (mi, ni)),
        scratch_shapes=[pltpu.VMEM((k, tn), BF16) for _ in ws],
    )
    return pl.pallas_call(
        functools.partial(_mm_body, n_w=n_w, mode=mode),
        grid_spec=grid_spec,
        out_shape=jax.ShapeDtypeStruct((m, n), out_dtype),
        compiler_params=_cparams(("arbitrary", "arbitrary")),
        name=name,
    )(tile_grp, tile_first, n_valid, *args)


def _dense_matmul(x, w, layer, *, tm, tn, mode, out_dtype, res=None, w2=None, name):
    tiles = x.shape[0] // tm
    grp = jnp.full((tiles,), layer, I32)
    first = jnp.zeros((tiles,), I32).at[0].set(1)
    nv = jnp.full((1,), tiles, I32)
    ws = [w] if w2 is None else [w, w2]
    return _grouped_matmul(x, ws, grp, first, nv, tm=tm, tn=tn, mode=mode, out_dtype=out_dtype,
                           res=res, name=name)


_SLOT, _ISSUED, _DONE, _NEXT_GRP, _NEXT_COL, _HAS_NEXT = range(6)


def _stream_body(grp_ref, first_ref, nextg_ref, wrap_ref, nv_ref, x_ref, *refs, n_w, mode, step_rows):
    res_ref = refs[0] if mode == "resid" else None
    refs = refs[1:] if mode == "resid" else refs
    w_hbm = refs[:n_w]
    o_ref, wb_ref, stage_ref, st_ref, sem_ref = refs[n_w:n_w + 5]
    n = pl.program_id(0)
    m = pl.program_id(1)
    k, tn = wb_ref.shape[2], wb_ref.shape[3]
    n_steps = k // step_rows
    half = step_rows // 2
    valid = m < nv_ref[0]

    def pieces(step):
        row0 = pl.multiple_of(step * step_rows, 2 * SUBLANES)
        col = pl.multiple_of(st_ref[_NEXT_COL] * tn, LANES)
        out = []
        for j in range(n_w):
            for p in range(2):
                src = w_hbm[j].at[st_ref[_NEXT_GRP], pl.ds(row0 + p * half, half), pl.ds(col, tn)]
                dst = stage_ref.at[lax.rem(step, 2), j, pl.ds(p * half, half), :]
                out.append(pltpu.make_async_copy(src, dst, sem_ref.at[0]))
        return out

    def start(step):
        for c in pieces(step):
            c.start(priority=1)

    def wait(step):
        for c in pieces(step):
            c.wait()

    def cast(slot, step):
        row0 = pl.multiple_of(step * step_rows, 2 * SUBLANES)
        for j in range(n_w):
            wb_ref[slot, j, pl.ds(row0, step_rows), :] = stage_ref[lax.rem(step, 2), j].astype(BF16)

    def compute(slot):
        x = x_ref[...]
        if mode == "gated":
            a = jnp.dot(x, wb_ref[slot, 0], preferred_element_type=F32)
            b = jnp.dot(x, wb_ref[slot, 1], preferred_element_type=F32)
            o_ref[...] = (_silu(a) * b).astype(o_ref.dtype)
        else:
            acc = jnp.dot(x, wb_ref[slot, 0], preferred_element_type=F32)
            if mode == "resid":
                acc = res_ref[...] + acc
            o_ref[...] = acc.astype(o_ref.dtype)

    @pl.when(jnp.logical_and(valid, first_ref[m] == 1))
    def _switch():
        @pl.when(jnp.logical_and(n == 0, m == 0))
        def _boot():
            st_ref[_SLOT] = 1
            st_ref[_ISSUED] = 0
            st_ref[_DONE] = 0
            st_ref[_NEXT_GRP] = grp_ref[0]
            st_ref[_NEXT_COL] = 0

        def finish(step, carry):
            @pl.when(st_ref[_ISSUED] <= step)
            def _fetch():
                start(step)
                wait(step)
            cast(1 - st_ref[_SLOT], step)
            return carry
        lax.fori_loop(st_ref[_DONE], n_steps, finish, 0)

        st_ref[_SLOT] = 1 - st_ref[_SLOT]
        st_ref[_ISSUED] = 0
        st_ref[_DONE] = 0
        st_ref[_NEXT_GRP] = nextg_ref[m]
        st_ref[_NEXT_COL] = n + wrap_ref[m]
        last_col = n == pl.num_programs(0) - 1
        st_ref[_HAS_NEXT] = jnp.where(jnp.logical_and(wrap_ref[m] == 1, last_col), 0, 1)

    slot = st_ref[_SLOT]
    issued = st_ref[_ISSUED]
    done = st_ref[_DONE]
    in_flight = done < issued
    more = jnp.logical_and(st_ref[_HAS_NEXT] == 1, issued < n_steps)

    for do_cast in (False, True):
        for do_start in (False, True):
            @pl.when(jnp.logical_and(valid, jnp.logical_and(in_flight == do_cast, more == do_start)))
            def _step(do_cast=do_cast, do_start=do_start):
                if do_start:
                    start(issued)
                if do_cast:
                    cast(1 - slot, done)
                    st_ref[_DONE] = done + 1
                compute(slot)
                if do_start:
                    wait(issued)
                    st_ref[_ISSUED] = issued + 1

    @pl.when(jnp.logical_not(valid))
    def _zero():
        o_ref[...] = jnp.zeros_like(o_ref)


def _stream_matmul(x, ws, tile_grp, tile_first, tile_next, tile_wrap, n_valid, *, tm, tn, mode, out_dtype, name,
                   step_rows, res=None):
    m, k = x.shape
    n = ws[0].shape[2]
    assert m % tm == 0 and n % tn == 0 and tn % LANES == 0
    assert k % step_rows == 0 and step_rows % (4 * SUBLANES) == 0
    n_w = len(ws)
    row_of = lambda mi, nv: jnp.minimum(mi, nv[0] - 1)
    extra_specs, extra_args = [], []
    if mode == "resid":
        extra_specs = [pl.BlockSpec((tm, tn), lambda ni, mi, g, f, nx, wr, nv: (row_of(mi, nv), ni))]
        extra_args = [res]
    grid_spec = pltpu.PrefetchScalarGridSpec(
        num_scalar_prefetch=5,
        grid=(n // tn, m // tm),
        in_specs=[pl.BlockSpec((tm, k), lambda ni, mi, g, f, nx, wr, nv: (row_of(mi, nv), 0))]
                 + extra_specs + [pl.BlockSpec(memory_space=pl.ANY)] * n_w,
        out_specs=pl.BlockSpec((tm, tn), lambda ni, mi, g, f, nx, wr, nv: (mi, ni)),
        scratch_shapes=[pltpu.VMEM((2, n_w, k, tn), BF16), pltpu.VMEM((2, n_w, step_rows, tn), F32),
                        pltpu.SMEM((8,), I32), pltpu.SemaphoreType.DMA((1,))],
    )
    return pl.pallas_call(
        functools.partial(_stream_body, n_w=n_w, mode=mode, step_rows=step_rows),
        grid_spec=grid_spec,
        out_shape=jax.ShapeDtypeStruct((m, n), out_dtype),
        compiler_params=_cparams(("arbitrary", "arbitrary"), vmem=60 * 1024 * 1024),
        name=name,
    )(tile_grp, tile_first, tile_next, tile_wrap, n_valid, x, *extra_args, *ws)


def _dense_stream_matmul(x, w, layer, *, tm, tn, mode, out_dtype, res=None, w2=None, name):
    tiles = x.shape[0] // tm
    grp = jnp.full((tiles,), layer, I32)
    first = jnp.zeros((tiles,), I32).at[0].set(1)
    wrap = jnp.ones((tiles,), I32)
    nv = jnp.full((1,), tiles, I32)
    ws = [w] if w2 is None else [w, w2]
    step_rows = _stream_step_rows(w.shape[1], tiles - 1)
    return _stream_matmul(x, ws, grp, first, grp, wrap, nv, tm=tm, tn=tn, mode=mode, out_dtype=out_dtype,
                          res=res, name=name, step_rows=step_rows)


def _stream_step_rows(k, steps_available):
    best = k
    for steps in range(1, max(steps_available, 1) + 1):
        if k % steps == 0 and (k // steps) % (4 * SUBLANES) == 0:
            best = k // steps
    return best


def _mix_prompt_body(z_ref, caw_ref, cbw_ref, cbb_ref, lbg_ref, lbb_ref, lcg_ref, lcb_ref, spw_ref,
                     spbt_ref, pw_ref, ps_ref, y_ref, na_ref, nb_ref, nd_ref,
                     qa_ext, gb_ext, pd_ext, vn_buf, wm_ref, gsh_ref, *, rows, width, ka, kb, nbuf_d):
    i = pl.program_id(1)
    last = pl.num_programs(1) - 1
    w = width
    gw = w // len(POOL_WINDOWS)
    chunk = wm_ref.shape[1]
    cg = wm_ref.shape[0]
    cgw = w // cg
    sub = 32

    @pl.when(i == 0)
    def _init():
        zeros = jnp.zeros((HIST, w), F32)
        qa_ext[0:HIST, :] = zeros
        gb_ext[0:HIST, :] = zeros
        pd_ext[0:HIST, :] = zeros
        tri = lax.broadcasted_iota(I32, (chunk, chunk), 0) >= lax.broadcasted_iota(I32, (chunk, chunk), 1)
        for g in range(cg):
            wm_ref[g] = jnp.where(tri, spw_ref[g], 0.0).astype(BF16)

    def col(k):
        return z_ref[:, k * w:(k + 1) * w].astype(F32)

    qa_ext[HIST:HIST + rows, :] = col(0) * col(1)
    gb_ext[HIST:HIST + rows, :] = col(3) * jax.nn.sigmoid(col(4))
    pd_ext[HIST:HIST + rows, :] = col(7)
    vn_buf[...] = _layernorm(col(6), lcg_ref[...], lcb_ref[...]).astype(BF16)

    conv_a = None
    for j in range(ka):
        term = caw_ref[j:j + 1, :] * qa_ext[HIST - (ka - 1) + j:HIST - (ka - 1) + j + rows, :]
        conv_a = term if conv_a is None else conv_a + term
    y_ref[:, 0:w] = (col(2) * conv_a).astype(y_ref.dtype)

    for f in range(1, SUBLANES):
        gsh_ref[f - 1, SUBLANES:HIST + rows, :] = gb_ext[SUBLANES - f:HIST + rows - f, :]
    for r0 in range(0, rows, sub):
        acc = None
        for j in range(kb):
            back = kb - 1 - j
            f, s = back % SUBLANES, HIST + r0 - (back - back % SUBLANES)
            src = gb_ext[s:s + sub, :] if f == 0 else gsh_ref[f - 1, s:s + sub, :]
            term = cbw_ref[j:j + 1, :] * src
            acc = term if acc is None else acc + term
        yb = _silu(_layernorm(acc + cbb_ref[...], lbg_ref[...], lbb_ref[...]))
        y_ref[r0:r0 + sub, w:2 * w] = yb.astype(y_ref.dtype)

    for c0 in range(0, rows, chunk):
        for g in range(cg):
            lanes = slice(g * cgw, (g + 1) * cgw)
            s = jnp.dot(wm_ref[g], vn_buf[c0:c0 + chunk, lanes], preferred_element_type=F32)
            s = s + spbt_ref[:, g:g + 1]
            u = z_ref[c0:c0 + chunk, 5 * w + g * cgw:5 * w + (g + 1) * cgw].astype(F32)
            y_ref[c0:c0 + chunk, 2 * w + g * cgw:2 * w + (g + 1) * cgw] = (u * s).astype(y_ref.dtype)

    pos = i * rows + lax.broadcasted_iota(I32, (rows, 1), 0)
    for gi, win in enumerate(POOL_WINDOWS):
        lanes = slice(gi * gw, (gi + 1) * gw)
        tot = None
        for k in range(win):
            term = pd_ext[HIST - k:HIST - k + rows, lanes]
            tot = term if tot is None else tot + term
        cnt = jnp.minimum(pos + 1, win).astype(F32)
        d = tot / cnt - pd_ext[HIST:HIST + rows, lanes]
        yd = jnp.dot(d.astype(BF16), pw_ref[gi].astype(BF16), preferred_element_type=F32)
        y_ref[:, 3 * w + gi * gw:3 * w + (gi + 1) * gw] = (yd * ps_ref[:, lanes]).astype(y_ref.dtype)

    @pl.when(i == last)
    def _state():
        na_ref[...] = qa_ext[HIST + rows - (ka - 1):HIST + rows, :]
        nb_ref[...] = gb_ext[HIST + rows - (kb - 1):HIST + rows, :]
        nd_ref[...] = pd_ext[HIST + rows - nbuf_d:HIST + rows, :]

    qa_ext[0:HIST, :] = qa_ext[rows:rows + HIST, :]
    gb_ext[0:HIST, :] = gb_ext[rows:rows + HIST, :]
    pd_ext[0:HIST, :] = pd_ext[rows:rows + HIST, :]


def _mix_prompt(z, lw, *, batch, seq, rows):
    w = lw["conv_a_w"].shape[1]
    ka = lw["conv_a_w"].shape[0]
    kb = lw["conv_b_w"].shape[0]
    nbuf_d = max(POOL_WINDOWS) - 1
    cg, chunk = lw["sp_w"].shape[0], lw["sp_w"].shape[1]
    assert seq % rows == 0 and rows % chunk == 0 and kb - 1 <= HIST and nbuf_d <= HIST
    nt = seq // rows
    full2 = lambda a: pl.BlockSpec(a.shape, lambda b, i: (0, 0))
    full3 = lambda a: pl.BlockSpec(a.shape, lambda b, i: (0, 0, 0))
    vec = lambda a: a.reshape(1, w)
    small = [lw["conv_a_w"], lw["conv_b_w"], vec(lw["conv_b_b"]), vec(lw["ln_b_g"]), vec(lw["ln_b_b"]),
             vec(lw["ln_c_g"]), vec(lw["ln_c_b"])]
    spbt = lw["sp_b"].T
    in_specs = ([pl.BlockSpec((rows, 8 * w), lambda b, i: (b * nt + i, 0))]
                + [full2(a) for a in small]
                + [full3(lw["sp_w"]), full2(spbt), full3(lw["pool_w"]), full2(vec(lw["pool_scale"]))])
    out_shapes = (jax.ShapeDtypeStruct((batch * seq, 4 * w), BF16),
                  jax.ShapeDtypeStruct((batch, ka - 1, w), F32),
                  jax.ShapeDtypeStruct((batch, kb - 1, w), F32),
                  jax.ShapeDtypeStruct((batch, nbuf_d, w), F32))
    out_specs = (pl.BlockSpec((rows, 4 * w), lambda b, i: (b * nt + i, 0)),
                 pl.BlockSpec((None, ka - 1, w), lambda b, i: (b, 0, 0)),
                 pl.BlockSpec((None, kb - 1, w), lambda b, i: (b, 0, 0)),
                 pl.BlockSpec((None, nbuf_d, w), lambda b, i: (b, 0, 0)))
    scratch = [pltpu.VMEM((HIST + rows, w), F32), pltpu.VMEM((HIST + rows, w), F32),
               pltpu.VMEM((HIST + rows, w), F32), pltpu.VMEM((rows, w), BF16),
               pltpu.VMEM((cg, chunk, chunk), BF16),
               pltpu.VMEM((SUBLANES - 1, HIST + rows, w), F32)]
    return pl.pallas_call(
        functools.partial(_mix_prompt_body, rows=rows, width=w, ka=ka, kb=kb, nbuf_d=nbuf_d),
        grid=(batch, nt),
        in_specs=in_specs,
        out_specs=out_specs,
        out_shape=out_shapes,
        scratch_shapes=scratch,
        compiler_params=_cparams(("arbitrary", "arbitrary")),
        name="mix_prompt",
    )(z, *small, lw["sp_w"], spbt, lw["pool_w"], vec(lw["pool_scale"]))


def _mix_sample_body(spw_ref, spb_ref, z_ref, sa_ref, sb_ref, sd_ref, caw_ref, cbw_ref, cbb_ref,
                     lbg_ref, lbb_ref, lcg_ref, lcb_ref, pw_ref, ps_ref,
                     y_ref, na_ref, nb_ref, nd_ref, cv_ref, *, nseq, steps, width, ka, kb, nbuf_d, cg):
    w = width
    gw = w // len(POOL_WINDOWS)
    cgw = w // cg
    lane = lax.broadcasted_iota(I32, (1, w), 1)

    def col(t, k):
        return z_ref[t * nseq:(t + 1) * nseq, k * w:(k + 1) * w].astype(F32)

    def put(t, k, val):
        y_ref[t * nseq:(t + 1) * nseq, k * w:(k + 1) * w] = val.astype(y_ref.dtype)

    ext = [sa_ref[k] for k in range(ka - 1)] + [col(t, 0) * col(t, 1) for t in range(steps)]
    for t in range(steps):
        conv = None
        for j in range(ka):
            term = caw_ref[j:j + 1, :] * ext[t + j]
            conv = term if conv is None else conv + term
        put(t, 0, col(t, 2) * conv)
    for k in range(ka - 1):
        na_ref[k] = ext[steps + k]

    for t in range(steps):
        nb_ref[kb - 1 - steps + t] = col(t, 3) * jax.nn.sigmoid(col(t, 4))
    for k in range(kb - 1 - steps):
        nb_ref[k] = sb_ref[k + steps]

    def ext_b(k):
        return sb_ref[k] if k < kb - 1 else nb_ref[k - steps]

    for t in range(steps):
        acc = None
        for j in range(kb):
            term = cbw_ref[j:j + 1, :] * ext_b(t + j)
            acc = term if acc is None else acc + term
        put(t, 1, _silu(_layernorm(acc + cbb_ref[...], lbg_ref[...], lbb_ref[...])))

    def lane_groups(vals):
        out = jnp.full((1, w), vals[cg - 1], F32)
        for g in range(cg - 2, -1, -1):
            out = jnp.where(lane < (g + 1) * cgw, vals[g], out)
        return out

    for t in range(steps):
        cv_ref[t] = _layernorm(col(t, 6), lcg_ref[...], lcb_ref[...])
    for t in range(steps):
        s = lane_groups([spb_ref[g, t] for g in range(cg)])
        for sp in range(t + 1):
            s = s + lane_groups([spw_ref[g, t, sp] for g in range(cg)]) * cv_ref[sp]
        put(t, 2, col(t, 5) * s)

    for t in range(steps):
        nd_ref[nbuf_d - steps + t] = col(t, 7)
    for k in range(nbuf_d - steps):
        nd_ref[k] = sd_ref[k + steps]

    def ext_d(k):
        return sd_ref[k] if k < nbuf_d else nd_ref[k - steps]

    for t in range(steps):
        for gi, win in enumerate(POOL_WINDOWS):
            lanes = slice(gi * gw, (gi + 1) * gw)
            tot = None
            for k in range(win):
                term = ext_d(nbuf_d + t - k)[:, lanes]
                tot = term if tot is None else tot + term
            cnt = float(min(PAST_LEN + t + 1, win))
            d = tot / cnt - ext_d(nbuf_d + t)[:, lanes]
            yd = jnp.dot(d.astype(BF16), pw_ref[gi].astype(BF16), preferred_element_type=F32)
            y_ref[t * nseq:(t + 1) * nseq, 3 * w + gi * gw:3 * w + (gi + 1) * gw] = (
                yd * ps_ref[:, lanes]).astype(y_ref.dtype)


def _mix_sample(z, sa, sb, sd, lw, *, nseq, steps, m_prompt):
    w = lw["conv_a_w"].shape[1]
    ka = lw["conv_a_w"].shape[0]
    kb = lw["conv_b_w"].shape[0]
    nbuf_d = max(POOL_WINDOWS) - 1
    cg = lw["sp_w"].shape[0]
    ms = nseq * steps
    assert m_prompt % ms == 0 and steps <= nbuf_d and steps <= kb - 1
    rb = m_prompt // ms
    spw_small = lw["sp_w"][:, :steps, :steps]
    spb_small = lw["sp_b"][:, :steps]
    vec = lambda a: a.reshape(1, w)
    smem = pl.BlockSpec(memory_space=pltpu.SMEM)
    full2 = lambda a: pl.BlockSpec(a.shape, lambda i: (0, 0))
    full3 = lambda a: pl.BlockSpec(a.shape, lambda i: (0, 0, 0))
    small = [lw["conv_a_w"], lw["conv_b_w"], vec(lw["conv_b_b"]), vec(lw["ln_b_g"]), vec(lw["ln_b_b"]),
             vec(lw["ln_c_g"]), vec(lw["ln_c_b"])]
    in_specs = ([smem, smem,
                 pl.BlockSpec((ms, 8 * w), lambda i: (rb, 0)),
                 full3(sa), full3(sb), full3(sd)]
                + [full2(a) for a in small]
                + [full3(lw["pool_w"]), full2(vec(lw["pool_scale"]))])
    out_shapes = (jax.ShapeDtypeStruct((ms, 4 * w), BF16),
                  jax.ShapeDtypeStruct(sa.shape, F32), jax.ShapeDtypeStruct(sb.shape, F32),
                  jax.ShapeDtypeStruct(sd.shape, F32), jax.ShapeDtypeStruct((steps, nseq, w), F32))
    out_specs = (pl.BlockSpec((ms, 4 * w), lambda i: (0, 0)),
                 full3(sa), full3(sb), full3(sd),
                 pl.BlockSpec((steps, nseq, w), lambda i: (0, 0, 0)))
    return pl.pallas_call(
        functools.partial(_mix_sample_body, nseq=nseq, steps=steps, width=w, ka=ka, kb=kb,
                          nbuf_d=nbuf_d, cg=cg),
        grid=(1,),
        in_specs=in_specs,
        out_specs=out_specs,
        out_shape=out_shapes,
        compiler_params=_cparams(("arbitrary",)),
        name="mix_sample",
    )(spw_small, spb_small, z, sa, sb, sd, *small, lw["pool_w"], vec(lw["pool_scale"]))


def _merge_out_body(yp_ref, ys_ref, g0_ref, g1_ref, g2_ref, g3_ref, x_ref, gn_ref, wa_hbm, wb_hbm, wc_hbm, wd_hbm,
                    wo_hbm, *refs, width, prompt_tiles, layer, with_h):
    n_out = 2 if with_h else 1
    x1_ref = refs[0]
    wbr_ref, wout_ref, stage_ref, sem_ref = refs[n_out:n_out + 4]
    i = pl.program_id(0)
    rows = stage_ref.shape[1]
    d = wout_ref.shape[0]

    @pl.when(i == 0)
    def _load_weights():
        chunks = []
        for k, w_hbm in enumerate((wa_hbm, wb_hbm, wc_hbm, wd_hbm)):
            for r in range(0, width, rows):
                chunks.append((w_hbm.at[layer, pl.ds(r, rows), :], wbr_ref.at[k, pl.ds(r, rows), :]))
        for r in range(0, d, rows):
            chunks.append((wo_hbm.at[layer, pl.ds(r, rows), :], wout_ref.at[pl.ds(r, rows), :]))

        def fetch(c):
            return pltpu.make_async_copy(chunks[c][0], stage_ref.at[c % 2], sem_ref.at[c % 2])

        fetch(0).start(priority=0)
        for c in range(len(chunks)):
            if c + 1 < len(chunks):
                fetch(c + 1).start(priority=(c + 1) % 2)
            fetch(c).wait()
            chunks[c][1][...] = stage_ref[c % 2].astype(BF16)

    def run(y_ref):
        acc = None
        for k, g_ref in enumerate((g0_ref, g1_ref, g2_ref, g3_ref)):
            br = jnp.dot(y_ref[:, k * width:(k + 1) * width], wbr_ref[k], preferred_element_type=F32)
            term = jax.nn.sigmoid(g_ref[...].astype(F32)) * br
            acc = term if acc is None else acc + term
        x1 = x_ref[...] + jnp.dot(acc.astype(BF16), wout_ref[...], preferred_element_type=F32)
        x1_ref[...] = x1
        if with_h:
            refs[1][...] = _rms(x1, gn_ref[...]).astype(BF16)

    @pl.when(i < prompt_tiles)
    def _prompt():
        run(yp_ref)

    @pl.when(i >= prompt_tiles)
    def _sample():
        run(ys_ref)


def _merge_out(y_p, y_s, z, x, w_brs, w_out, layer, g, *, tm, with_h):
    mp, yw = y_p.shape
    ms = y_s.shape[0]
    assert mp % tm == 0 and ms % tm == 0
    tp = mp // tm
    w = yw // 4
    d = w_out.shape[2]
    assert (2 * yw) % d == 0 and w % STAGE_ROWS == 0 and d % STAGE_ROWS == 0
    goff = (2 * yw) // d
    gate_spec = lambda k: pl.BlockSpec((tm, d), lambda i, k=k: (i, goff + k))
    row = pl.BlockSpec((tm, d), lambda i: (i, 0))
    hbm = pl.BlockSpec(memory_space=pl.ANY)
    out_shape = [jax.ShapeDtypeStruct((mp + ms, d), F32)]
    out_specs = [row]
    if with_h:
        out_shape.append(jax.ShapeDtypeStruct((mp + ms, d), BF16))
        out_specs.append(row)
    return pl.pallas_call(
        functools.partial(_merge_out_body, width=w, prompt_tiles=tp, layer=layer, with_h=with_h),
        grid=((mp + ms) // tm,),
        in_specs=[pl.BlockSpec((tm, yw), lambda i: (jnp.minimum(i, tp - 1), 0)),
                  pl.BlockSpec((tm, yw), lambda i: (jnp.maximum(i - tp, 0), 0))]
                 + [gate_spec(k) for k in range(4)]
                 + [row, pl.BlockSpec((1, d), lambda i: (0, 0))] + [hbm] * 5,
        out_specs=out_specs,
        out_shape=out_shape,
        scratch_shapes=[pltpu.VMEM((4, w, d), BF16), pltpu.VMEM((d, d), BF16),
                        pltpu.VMEM((2, STAGE_ROWS, d), F32), pltpu.SemaphoreType.DMA((2,))],
        compiler_params=_cparams(("arbitrary",)),
        name="merge_out",
    )(y_p, y_s, z, z, z, z, x, g.reshape(1, d), *w_brs, w_out)


def _route_body(x_ref, g_ref, rwt_ref, e_ref, r_ref, wt_ref, cnt_ref, carry_ref, tri_ref):
    i = pl.program_id(0)
    ne = rwt_ref.shape[0]
    tm = x_ref.shape[0]

    @pl.when(i == 0)
    def _init():
        carry_ref[...] = jnp.zeros_like(carry_ref)
        upper = lax.broadcasted_iota(I32, (tm, tm), 0) <= lax.broadcasted_iota(I32, (tm, tm), 1)
        tri_ref[...] = jnp.where(upper, 1.0, 0.0).astype(BF16)

    h = _rms(x_ref[...], g_ref[...])
    h_hi = h.astype(BF16)
    h_lo = (h - h_hi.astype(F32)).astype(BF16)
    rw = rwt_ref[...]
    rw_hi = rw.astype(BF16)
    rw_lo = (rw - rw_hi.astype(F32)).astype(BF16)
    dn = (((1,), (1,)), ((), ()))
    dg = lambda a, b: lax.dot_general(a, b, dn, preferred_element_type=F32)
    logits = dg(rw_hi, h_hi) + dg(rw_hi, h_lo) + dg(rw_lo, h_hi)
    ex = jnp.exp(logits - jnp.max(logits, axis=0, keepdims=True))
    p = ex / jnp.sum(ex, axis=0, keepdims=True)
    eid = lax.broadcasted_iota(I32, (ne, tm), 0)
    m1 = jnp.max(p, axis=0, keepdims=True)
    i1 = jnp.min(jnp.where(p == m1, eid, ne), axis=0, keepdims=True)
    oh1 = eid == i1
    p2 = jnp.where(oh1, -1.0, p)
    m2 = jnp.max(p2, axis=0, keepdims=True)
    i2 = jnp.min(jnp.where(p2 == m2, eid, ne), axis=0, keepdims=True)
    oh2 = eid == i2
    den = m1 + m2
    sel = jnp.logical_or(oh1, oh2)
    sel_f = jnp.where(sel, 1.0, 0.0)
    cum = jnp.dot(sel_f.astype(BF16), tri_ref[...], preferred_element_type=F32) + carry_ref[...]
    rank1 = jnp.sum(jnp.where(oh1, cum, 0.0), axis=0, keepdims=True) - 1.0
    rank2 = jnp.sum(jnp.where(oh2, cum, 0.0), axis=0, keepdims=True) - 1.0
    carry_ref[...] = carry_ref[...] + jnp.sum(sel_f, axis=1, keepdims=True)
    e_ref[0:1, :] = i1
    e_ref[1:2, :] = i2
    r_ref[0:1, :] = rank1.astype(I32)
    r_ref[1:2, :] = rank2.astype(I32)
    wt_ref[0:1, :] = m1 / den
    wt_ref[1:2, :] = m2 / den

    @pl.when(i == pl.num_programs(0) - 1)
    def _counts():
        cnt_ref[...] = jnp.broadcast_to(carry_ref[...], cnt_ref.shape).astype(I32)


def _route(x1, g, router_w, *, tm):
    m, d = x1.shape
    ne = router_w.shape[1]
    tok = lambda dt: jax.ShapeDtypeStruct((TOP_K, m), dt)
    tok_spec = pl.BlockSpec((TOP_K, tm), lambda i: (0, i))
    return pl.pallas_call(
        _route_body,
        grid=(m // tm,),
        in_specs=[pl.BlockSpec((tm, d), lambda i: (i, 0)),
                  pl.BlockSpec((1, d), lambda i: (0, 0)),
                  pl.BlockSpec((ne, d), lambda i: (0, 0))],
        out_specs=(tok_spec, tok_spec, tok_spec, pl.BlockSpec((ne, LANES), lambda i: (0, 0))),
        out_shape=(tok(I32), tok(I32), tok(F32), jax.ShapeDtypeStruct((ne, LANES), I32)),
        scratch_shapes=[pltpu.VMEM((ne, 1), F32), pltpu.VMEM((tm, tm), BF16)],
        compiler_params=_cparams(("arbitrary",)),
        name="route",
    )(x1, g.reshape(1, d), router_w.T)


def _plan_body(e_ref, r_ref, cnt_ref, pos_ref, tgrp_ref, tfirst_ref, misc_ref, *, tile_shift):
    ne = cnt_ref.shape[0]
    tile = 1 << tile_shift
    sub = lax.broadcasted_iota(I32, (ne, LANES), 0)
    lane = lax.broadcasted_iota(I32, (ne, LANES), 1)

    def as_row(col):
        return jnp.sum(jnp.where(sub == lane, col, 0), axis=0, keepdims=True)

    cnt = cnt_ref[:, 0:1]
    padded = lax.shift_left(lax.shift_right_logical(cnt + (tile - 1), tile_shift), tile_shift)
    gstart = jnp.sum(jnp.where(lane < sub, as_row(padded), 0), axis=1, keepdims=True)
    gend = gstart + padded
    m = e_ref.shape[1]
    eid = lax.broadcasted_iota(I32, (ne, m), 0)
    for k in range(TOP_K):
        base = jnp.sum(jnp.where(eid == e_ref[k:k + 1, :], gstart, 0), axis=0, keepdims=True)
        pos_ref[k:k + 1, :] = base + r_ref[k:k + 1, :]
    start = lax.broadcasted_iota(I32, (1, LANES), 1) * tile
    grp = jnp.sum((gend <= start).astype(I32), axis=0, keepdims=True)
    tgrp_ref[...] = jnp.minimum(grp, ne - 1)
    tfirst_ref[...] = jnp.sum(jnp.logical_and(gstart == start, padded > 0).astype(I32), axis=0, keepdims=True)
    n_tiles = lax.shift_right_logical(jnp.max(gend, axis=0, keepdims=True), tile_shift)
    misc = jnp.where(sub == 0, n_tiles, 0)
    misc = jnp.where(sub == 1, as_row(gstart + cnt), misc)
    misc = jnp.where(sub == 2, as_row(gend), misc)
    lane_f = lane.astype(F32)
    nonempty = jnp.logical_and(as_row(padded) > 0, lane < ne)
    later = jnp.min(jnp.where(jnp.logical_and(nonempty, lane > sub), lane_f, float(ne)), axis=1, keepdims=True)
    firstg = jnp.min(jnp.where(nonempty, lane_f, float(ne)), axis=1, keepdims=True)
    nxt = jnp.where(later < ne, later, firstg).astype(I32)
    wraps = (later >= ne).astype(I32)
    mine = sub == jnp.minimum(grp, ne - 1)
    misc = jnp.where(sub == 3, jnp.sum(jnp.where(mine, nxt, 0), axis=0, keepdims=True), misc)
    misc = jnp.where(sub == 4, jnp.sum(jnp.where(mine, wraps, 0), axis=0, keepdims=True), misc)
    misc_ref[...] = misc


def _plan(e, r, cnt, *, tile):
    tile_shift = tile.bit_length() - 1
    assert 1 << tile_shift == tile
    ne = cnt.shape[0]
    m = e.shape[1]
    whole = lambda a: pl.BlockSpec(a.shape, lambda i: (0, 0))
    row = jax.ShapeDtypeStruct((1, LANES), I32)
    out_shape = (jax.ShapeDtypeStruct((TOP_K, m), I32), row, row, jax.ShapeDtypeStruct((ne, LANES), I32))
    return pl.pallas_call(
        functools.partial(_plan_body, tile_shift=tile_shift),
        grid=(1,),
        in_specs=[whole(e), whole(r), whole(cnt)],
        out_specs=tuple(pl.BlockSpec(s.shape, lambda i: (0, 0)) for s in out_shape),
        out_shape=out_shape,
        compiler_params=_cparams(("arbitrary",)),
        name="plan",
    )(e, r, cnt)


def _dispatch_body(pos0_ref, pos1_ref, nv_ref, fill_ref, end_ref, x_hbm, g_ref, o_ref, src_ref, buf_ref, sem_ref):
    j = pl.program_id(0)
    tm = o_ref.shape[0]
    n_tok = pos0_ref.shape[0]
    nv = nv_ref[0]

    def start_tile(tile, slot):
        def body(pair, carry):
            for p in range(2):
                r = pair * 2 + p
                tok = src_ref[tile * tm + r]
                pltpu.make_async_copy(x_hbm.at[pl.ds(tok, 1), :], buf_ref.at[slot, pl.ds(r, 1), :],
                                      sem_ref.at[slot]).start(priority=p)
            return carry
        lax.fori_loop(0, tm // 2, body, 0, unroll=ISSUE_UNROLL // 2)

    @pl.when(j == 0)
    def _prologue():
        for g in range(fill_ref.shape[0]):
            def fill(p, carry):
                src_ref[p] = 0
                return carry
            lax.fori_loop(fill_ref[g], end_ref[g], fill, 0)

        def inv(t, carry):
            src_ref[pos0_ref[t]] = t
            src_ref[pos1_ref[t]] = t
            return carry
        lax.fori_loop(0, n_tok, inv, 0, unroll=ISSUE_UNROLL)
        start_tile(0, 0)

    slot = lax.rem(j, 2)

    @pl.when(j < nv)
    def _work():
        @pl.when(j + 1 < nv)
        def _prefetch():
            start_tile(j + 1, 1 - slot)

        pltpu.make_async_copy(x_hbm.at[pl.ds(0, tm), :], buf_ref.at[slot], sem_ref.at[slot]).wait()
        o_ref[...] = _rms(buf_ref[slot], g_ref[...]).astype(o_ref.dtype)

    @pl.when(j >= nv)
    def _skip():
        o_ref[...] = jnp.zeros_like(o_ref)


def _dispatch(x1, g, pos0, pos1, n_tiles, fill_row, end_row, *, tm, max_tiles):
    m, d = x1.shape
    grid_spec = pltpu.PrefetchScalarGridSpec(
        num_scalar_prefetch=5,
        grid=(max_tiles,),
        in_specs=[pl.BlockSpec(memory_space=pl.ANY),
                  pl.BlockSpec((1, d), lambda j, *_: (0, 0))],
        out_specs=pl.BlockSpec((tm, d), lambda j, *_: (j, 0)),
        scratch_shapes=[pltpu.SMEM((max_tiles * tm,), I32),
                        pltpu.VMEM((2, tm, d), F32),
                        pltpu.SemaphoreType.DMA((2,))],
    )
    return pl.pallas_call(
        _dispatch_body,
        grid_spec=grid_spec,
        out_shape=jax.ShapeDtypeStruct((max_tiles * tm, d), BF16),
        compiler_params=_cparams(("arbitrary",)),
        name="dispatch",
    )(pos0, pos1, n_tiles, fill_row, end_row, x1, g.reshape(1, d))


def _combine_body(pos0_ref, pos1_ref, ys_hbm, x_ref, wt_ref, g_ref, xn_ref, h_ref, buf_ref, sem_ref):
    i = pl.program_id(0)
    n = pl.num_programs(0)
    tm = x_ref.shape[0]

    def start_tile(tile, slot):
        def body(r, carry):
            t = tile * tm + r
            pltpu.make_async_copy(ys_hbm.at[pl.ds(pos0_ref[t], 1), :], buf_ref.at[slot, 0, pl.ds(r, 1), :],
                                  sem_ref.at[slot, 0]).start(priority=0)
            pltpu.make_async_copy(ys_hbm.at[pl.ds(pos1_ref[t], 1), :], buf_ref.at[slot, 1, pl.ds(r, 1), :],
                                  sem_ref.at[slot, 1]).start(priority=1)
            return carry
        lax.fori_loop(0, tm, body, 0, unroll=ISSUE_UNROLL)

    @pl.when(i == 0)
    def _prologue():
        start_tile(0, 0)

    slot = lax.rem(i, 2)

    @pl.when(i + 1 < n)
    def _prefetch():
        start_tile(i + 1, 1 - slot)

    for k in range(TOP_K):
        pltpu.make_async_copy(ys_hbm.at[pl.ds(0, tm), :], buf_ref.at[slot, k], sem_ref.at[slot, k]).wait()
    xn = x_ref[...] + wt_ref[:, 0:1] * buf_ref[slot, 0] + wt_ref[:, 1:2] * buf_ref[slot, 1]
    xn_ref[...] = xn
    h_ref[...] = _rms(xn, g_ref[...]).astype(h_ref.dtype)


def _combine(ys, x1, wt_cols, g, pos0, pos1, *, tm, h_dtype):
    m, d = x1.shape
    row = lambda: pl.BlockSpec((tm, d), lambda i, *_: (i, 0))
    grid_spec = pltpu.PrefetchScalarGridSpec(
        num_scalar_prefetch=2,
        grid=(m // tm,),
        in_specs=[pl.BlockSpec(memory_space=pl.ANY), row(),
                  pl.BlockSpec((tm, TOP_K), lambda i, *_: (i, 0)),
                  pl.BlockSpec((1, d), lambda i, *_: (0, 0))],
        out_specs=(row(), row()),
        scratch_shapes=[pltpu.VMEM((2, TOP_K, tm, d), F32), pltpu.SemaphoreType.DMA((2, TOP_K))],
    )
    return pl.pallas_call(
        _combine_body,
        grid_spec=grid_spec,
        out_shape=(jax.ShapeDtypeStruct((m, d), F32), jax.ShapeDtypeStruct((m, d), h_dtype)),
        compiler_params=_cparams(("arbitrary",)),
        name="combine",
    )(pos0, pos1, ys, x1, wt_cols, g.reshape(1, d))


def _pick(total, prefs):
    for p in prefs:
        if total % p == 0:
            return p
    raise ValueError(f"no tile for {total} in {prefs}")


def _row_tile(m, cap):
    best = None
    for t in range(16, cap + 1, 16):
        if m % t == 0:
            best = t
    assert best is not None
    return best


def kernel(x_prompt, x_sample, state_conv_a, state_conv_b, state_pool, norm_mix_g, w_in, conv_a_w, w_br_a,
           conv_b_w, conv_b_b, ln_b_g, ln_b_b, w_br_b, ln_c_g, ln_c_b, sp_w, sp_b, w_br_c, pool_w,
           pool_scale, w_br_d, w_out, norm_ffn_g, ffn_w1, ffn_w3, ffn_w2, router_w, moe_w1, moe_w3, moe_w2,
           final_norm_g):
    batch, seq, d = x_prompt.shape
    nseq, steps, _ = x_sample.shape
    depth = w_in.shape[0]
    n_exp = router_w.shape[2]
    mp, ms = batch * seq, nseq * steps
    m = mp + ms

    x = jnp.concatenate([x_prompt.reshape(mp, d), jnp.swapaxes(x_sample, 0, 1).reshape(ms, d)], axis=0)
    st_a = jnp.swapaxes(state_conv_a, 1, 2)
    st_b = jnp.swapaxes(state_conv_b, 1, 2)
    st_d = jnp.swapaxes(state_pool, 1, 2)

    tm_big = _row_tile(m, 1088)
    tm_mid = _row_tile(m, 544)
    tm_small = _row_tile(m, 256)
    tm_merge = min(_row_tile(mp, 256), _row_tile(ms, 256))
    assert mp % tm_merge == 0 and ms % tm_merge == 0
    tm_moe = 512
    max_tiles = -(-(m * TOP_K) // tm_moe) + n_exp
    assert max_tiles <= LANES
    mix_rows = _pick(seq, (256, 128))
    moe_w1f = moe_w1.reshape((-1,) + moe_w1.shape[2:])
    moe_w3f = moe_w3.reshape((-1,) + moe_w3.shape[2:])
    moe_w2f = moe_w2.reshape((-1,) + moe_w2.shape[2:])
    dffe = moe_w1.shape[3]
    dff = ffn_w1.shape[2]

    h = _rmsnorm(x, norm_mix_g[0], BF16, tm_mid)
    outs_p = ([], [], [])
    outs_s = ([], [], [], [])
    y_final = None
    for i in range(depth):
        lw = dict(conv_a_w=conv_a_w[i], conv_b_w=conv_b_w[i], conv_b_b=conv_b_b[i], ln_b_g=ln_b_g[i],
                  ln_b_b=ln_b_b[i], ln_c_g=ln_c_g[i], ln_c_b=ln_c_b[i], sp_w=sp_w[i], sp_b=sp_b[i],
                  pool_w=pool_w[i], pool_scale=pool_scale[i])
        z = _dense_stream_matmul(h, w_in, i, tm=tm_big, tn=_pick(w_in.shape[2], (1024, 512, 256, 128)),
                                 mode="plain", out_dtype=BF16, name="inproj")
        y_p, na_p, nb_p, nd_p = _mix_prompt(z, lw, batch=batch, seq=seq, rows=mix_rows)
        y_s, na_s, nb_s, nd_s, cv_s = _mix_sample(z, st_a[i], st_b[i], st_d[i], lw, nseq=nseq, steps=steps,
                                                  m_prompt=mp)
        for acc, val in zip(outs_p, (na_p, nb_p, nd_p)):
            acc.append(val)
        for acc, val in zip(outs_s, (na_s, nb_s, nd_s, cv_s)):
            acc.append(val)
        merge_out = functools.partial(_merge_out, y_p, y_s, z, x, (w_br_a, w_br_b, w_br_c, w_br_d), w_out, i,
                                      norm_ffn_g[i], tm=tm_merge)
        last = i == depth - 1
        g_next = final_norm_g if last else norm_mix_g[i + 1]
        h_dtype = F32 if last else BF16
        j = i // 2
        if i % 2 == 0:
            x1, h2 = merge_out(with_h=True)
            a = _dense_stream_matmul(h2, ffn_w1, j, w2=ffn_w3, tm=tm_big, tn=_pick(dff, (512, 256, 128)),
                                     mode="gated", out_dtype=BF16, name="ffn_up")
            x = _dense_stream_matmul(a, ffn_w2, j, tm=tm_mid, tn=_pick(d, (512, 256, 128)), mode="resid",
                                     out_dtype=F32, res=x1, name="ffn_down")
            h = _rmsnorm(x, g_next, h_dtype, tm_mid)
        else:
            (x1,) = merge_out(with_h=False)
            e, r, wt, cnt = _route(x1, norm_ffn_g[i], router_w[j], tm=_pick(m, (512, 256, 128)))
            pos, tgrp, tfirst, misc = _plan(e, r, cnt, tile=tm_moe)
            n_tiles = misc[0, 0:1]
            hs = _dispatch(x1, norm_ffn_g[i], pos[0], pos[1], n_tiles, misc[1, :n_exp], misc[2, :n_exp],
                           tm=tm_moe, max_tiles=max_tiles)
            grp = tgrp[0, :max_tiles] + j * n_exp
            first = tfirst[0, :max_tiles]
            nextg = misc[3, :max_tiles] + j * n_exp
            wrap = misc[4, :max_tiles]
            a = _stream_matmul(hs, [moe_w1f, moe_w3f], grp, first, nextg, wrap, n_tiles, tm=tm_moe,
                               tn=_pick(dffe, (1408, 1024, 512, 256, 128)), mode="gated", out_dtype=BF16,
                               name="moe_up", step_rows=_stream_step_rows(d, MOE_STREAM_STEPS))
            ys = _stream_matmul(a, [moe_w2f], grp, first, nextg, wrap, n_tiles, tm=tm_moe,
                                tn=_pick(d, (1024, 512, 256, 128)), mode="plain", out_dtype=F32, name="moe_down",
                                step_rows=_stream_step_rows(dffe, MOE_STREAM_STEPS))
            x, h = _combine(ys, x1, wt.T, g_next, pos[0], pos[1], tm=tm_small, h_dtype=h_dtype)
        if last:
            y_final = h

    y_prompt = y_final[:mp].reshape(batch, seq, d)
    y_sample = jnp.swapaxes(y_final[mp:].reshape(steps, nseq, d), 0, 1)
    stack_s = lambda vals: jnp.swapaxes(jnp.stack(vals), 1, 2)
    return (y_prompt, y_sample, jnp.stack(outs_p[0]), jnp.stack(outs_p[1]), jnp.stack(outs_p[2]),
            stack_s(outs_s[0]), stack_s(outs_s[1]), stack_s(outs_s[2]), stack_s(outs_s[3]))
```

```python
import functools

import jax
import jax.numpy as jnp
from jax import lax
from jax.experimental import pallas as pl
from jax.experimental.pallas import tpu as pltpu

F32 = jnp.float32
BF16 = jnp.bfloat16
I32 = jnp.int32

EPS = 1e-6
PAST_LEN = 16384
POOL_WINDOWS = (2, 4, 8, 16)
TOP_K = 2
LANES = 128
SUBLANES = 8
HIST = 32
ISSUE_UNROLL = 8
VMEM_LIMIT = 56 * 1024 * 1024


def _cparams(sem, vmem=VMEM_LIMIT):
    return pltpu.CompilerParams(dimension_semantics=sem, vmem_limit_bytes=vmem)


def _rms(x, g):
    return x * lax.rsqrt(jnp.mean(x * x, axis=-1, keepdims=True) + EPS) * g


def _layernorm(x, g, b):
    mu = jnp.mean(x, axis=-1, keepdims=True)
    xc = x - mu
    var = jnp.mean(xc * xc, axis=-1, keepdims=True)
    return xc * lax.rsqrt(var + EPS) * g + b


def _silu(x):
    return x * jax.nn.sigmoid(x)


def _rmsnorm_body(x_ref, g_ref, o_ref):
    o_ref[...] = _rms(x_ref[...], g_ref[...]).astype(o_ref.dtype)


def _rmsnorm(x, g, out_dtype, tm):
    m, d = x.shape
    return pl.pallas_call(
        _rmsnorm_body,
        grid=(m // tm,),
        in_specs=[pl.BlockSpec((tm, d), lambda i: (i, 0)),
                  pl.BlockSpec((1, d), lambda i: (0, 0))],
        out_specs=pl.BlockSpec((tm, d), lambda i: (i, 0)),
        out_shape=jax.ShapeDtypeStruct((m, d), out_dtype),
        compiler_params=_cparams(("arbitrary",)),
        name="rmsnorm",
    )(x, g.reshape(1, d))


STAGE_ROWS = 256
MOE_STREAM_STEPS = 4
STREAM_PIECES = 4


_SLOT, _ISSUED, _DONE, _NEXT_GRP, _NEXT_COL, _HAS_NEXT = range(6)


def _stream_body(grp_ref, first_ref, nextg_ref, wrap_ref, nv_ref, x_ref, *refs, n_w, mode, step_rows):
    res_ref = refs[0] if mode == "resid" else None
    refs = refs[1:] if mode == "resid" else refs
    w_hbm = refs[:n_w]
    o_ref, wb_ref, stage_ref, st_ref, sem_ref = refs[n_w:n_w + 5]
    n = pl.program_id(0)
    m = pl.program_id(1)
    k, tn = wb_ref.shape[2], wb_ref.shape[3]
    n_steps = k // step_rows
    part = step_rows // STREAM_PIECES
    valid = m < nv_ref[0]

    def pieces(step):
        row0 = pl.multiple_of(step * step_rows, 2 * SUBLANES)
        col = pl.multiple_of(st_ref[_NEXT_COL] * tn, LANES)
        out = []
        for j in range(n_w):
            for p in range(STREAM_PIECES):
                src = w_hbm[j].at[st_ref[_NEXT_GRP], pl.ds(row0 + p * part, part), pl.ds(col, tn)]
                dst = stage_ref.at[lax.rem(step, 2), j, pl.ds(p * part, part), :]
                out.append(pltpu.make_async_copy(src, dst, sem_ref.at[0]))
        return out

    def start(step):
        for i, c in enumerate(pieces(step)):
            c.start(priority=i % 2)

    def wait(step):
        for c in pieces(step):
            c.wait()

    def cast(slot, step):
        row0 = pl.multiple_of(step * step_rows, 2 * SUBLANES)
        for j in range(n_w):
            wb_ref[slot, j, pl.ds(row0, step_rows), :] = stage_ref[lax.rem(step, 2), j].astype(BF16)

    def compute(slot):
        x = x_ref[...]
        if mode == "gated":
            a = jnp.dot(x, wb_ref[slot, 0], preferred_element_type=F32)
            b = jnp.dot(x, wb_ref[slot, 1], preferred_element_type=F32)
            o_ref[...] = (_silu(a) * b).astype(o_ref.dtype)
        else:
            acc = jnp.dot(x, wb_ref[slot, 0], preferred_element_type=F32)
            if mode == "resid":
                acc = res_ref[...] + acc
            o_ref[...] = acc.astype(o_ref.dtype)

    @pl.when(jnp.logical_and(valid, first_ref[m] == 1))
    def _switch():
        @pl.when(jnp.logical_and(n == 0, m == 0))
        def _boot():
            st_ref[_SLOT] = 1
            st_ref[_ISSUED] = 0
            st_ref[_DONE] = 0
            st_ref[_NEXT_GRP] = grp_ref[0]
            st_ref[_NEXT_COL] = 0

        def finish(step, carry):
            @pl.when(st_ref[_ISSUED] <= step)
            def _fetch():
                start(step)
                wait(step)
            cast(1 - st_ref[_SLOT], step)
            return carry
        lax.fori_loop(st_ref[_DONE], n_steps, finish, 0)

        st_ref[_SLOT] = 1 - st_ref[_SLOT]
        st_ref[_ISSUED] = 0
        st_ref[_DONE] = 0
        st_ref[_NEXT_GRP] = nextg_ref[m]
        st_ref[_NEXT_COL] = n + wrap_ref[m]
        last_col = n == pl.num_programs(0) - 1
        st_ref[_HAS_NEXT] = jnp.where(jnp.logical_and(wrap_ref[m] == 1, last_col), 0, 1)

    slot = st_ref[_SLOT]
    issued = st_ref[_ISSUED]
    done = st_ref[_DONE]
    in_flight = done < issued
    more = jnp.logical_and(st_ref[_HAS_NEXT] == 1, issued < n_steps)

    for do_cast in (False, True):
        for do_start in (False, True):
            @pl.when(jnp.logical_and(valid, jnp.logical_and(in_flight == do_cast, more == do_start)))
            def _step(do_cast=do_cast, do_start=do_start):
                if do_start:
                    start(issued)
                if do_cast:
                    cast(1 - slot, done)
                    st_ref[_DONE] = done + 1
                compute(slot)
                if do_start:
                    wait(issued)
                    st_ref[_ISSUED] = issued + 1

    @pl.when(jnp.logical_not(valid))
    def _zero():
        o_ref[...] = jnp.zeros_like(o_ref)


def _stream_matmul(x, ws, tile_grp, tile_first, tile_next, tile_wrap, n_valid, *, tm, tn, mode, out_dtype, name,
                   step_rows, res=None):
    m, k = x.shape
    n = ws[0].shape[2]
    assert m % tm == 0 and n % tn == 0 and tn % LANES == 0
    assert k % step_rows == 0 and step_rows % (STREAM_PIECES * SUBLANES) == 0
    n_w = len(ws)
    row_of = lambda mi, nv: jnp.minimum(mi, nv[0] - 1)
    extra_specs, extra_args = [], []
    if mode == "resid":
        extra_specs = [pl.BlockSpec((tm, tn), lambda ni, mi, g, f, nx, wr, nv: (row_of(mi, nv), ni))]
        extra_args = [res]
    grid_spec = pltpu.PrefetchScalarGridSpec(
        num_scalar_prefetch=5,
        grid=(n // tn, m // tm),
        in_specs=[pl.BlockSpec((tm, k), lambda ni, mi, g, f, nx, wr, nv: (row_of(mi, nv), 0))]
                 + extra_specs + [pl.BlockSpec(memory_space=pl.ANY)] * n_w,
        out_specs=pl.BlockSpec((tm, tn), lambda ni, mi, g, f, nx, wr, nv: (mi, ni)),
        scratch_shapes=[pltpu.VMEM((2, n_w, k, tn), BF16), pltpu.VMEM((2, n_w, step_rows, tn), F32),
                        pltpu.SMEM((8,), I32), pltpu.SemaphoreType.DMA((1,))],
    )
    return pl.pallas_call(
        functools.partial(_stream_body, n_w=n_w, mode=mode, step_rows=step_rows),
        grid_spec=grid_spec,
        out_shape=jax.ShapeDtypeStruct((m, n), out_dtype),
        compiler_params=_cparams(("arbitrary", "arbitrary"), vmem=60 * 1024 * 1024),
        name=name,
    )(tile_grp, tile_first, tile_next, tile_wrap, n_valid, x, *extra_args, *ws)


def _dense_stream_matmul(x, w, layer, *, tm, tn, mode, out_dtype, res=None, w2=None, name):
    tiles = x.shape[0] // tm
    grp = jnp.full((tiles,), layer, I32)
    first = jnp.zeros((tiles,), I32).at[0].set(1)
    wrap = jnp.ones((tiles,), I32)
    nv = jnp.full((1,), tiles, I32)
    ws = [w] if w2 is None else [w, w2]
    step_rows = _stream_step_rows(w.shape[1], tiles - 1)
    return _stream_matmul(x, ws, grp, first, grp, wrap, nv, tm=tm, tn=tn, mode=mode, out_dtype=out_dtype,
                          res=res, name=name, step_rows=step_rows)


def _stream_step_rows(k, steps_available):
    best = k
    for steps in range(1, max(steps_available, 1) + 1):
        if k % steps == 0 and (k // steps) % (4 * SUBLANES) == 0:
            best = k // steps
    return best


def _mix_prompt_body(z_ref, caw_ref, cbw_ref, cbb_ref, lbg_ref, lbb_ref, lcg_ref, lcb_ref, spw_ref,
                     spbt_ref, pw_ref, ps_ref, y_ref, na_ref, nb_ref, nd_ref,
                     qa_ext, gb_ext, pd_ext, vn_buf, wm_ref, gsh_ref, *, rows, width, ka, kb, nbuf_d):
    i = pl.program_id(1)
    last = pl.num_programs(1) - 1
    w = width
    gw = w // len(POOL_WINDOWS)
    chunk = wm_ref.shape[1]
    cg = wm_ref.shape[0]
    cgw = w // cg
    sub = 32

    @pl.when(i == 0)
    def _init():
        zeros = jnp.zeros((HIST, w), F32)
        qa_ext[0:HIST, :] = zeros
        gb_ext[0:HIST, :] = zeros
        pd_ext[0:HIST, :] = zeros
        tri = lax.broadcasted_iota(I32, (chunk, chunk), 0) >= lax.broadcasted_iota(I32, (chunk, chunk), 1)
        for g in range(cg):
            wm_ref[g] = jnp.where(tri, spw_ref[g], 0.0).astype(BF16)

    def col(k):
        return z_ref[:, k * w:(k + 1) * w].astype(F32)

    qa_ext[HIST:HIST + rows, :] = col(0) * col(1)
    gb_ext[HIST:HIST + rows, :] = col(3) * jax.nn.sigmoid(col(4))
    pd_ext[HIST:HIST + rows, :] = col(7)
    vn_buf[...] = _layernorm(col(6), lcg_ref[...], lcb_ref[...]).astype(BF16)

    conv_a = None
    for j in range(ka):
        term = caw_ref[j:j + 1, :] * qa_ext[HIST - (ka - 1) + j:HIST - (ka - 1) + j + rows, :]
        conv_a = term if conv_a is None else conv_a + term
    y_ref[:, 0:w] = (col(2) * conv_a).astype(y_ref.dtype)

    for f in range(1, SUBLANES):
        gsh_ref[f - 1, SUBLANES:HIST + rows, :] = gb_ext[SUBLANES - f:HIST + rows - f, :]
    for r0 in range(0, rows, sub):
        acc = None
        for j in range(kb):
            back = kb - 1 - j
            f, s = back % SUBLANES, HIST + r0 - (back - back % SUBLANES)
            src = gb_ext[s:s + sub, :] if f == 0 else gsh_ref[f - 1, s:s + sub, :]
            term = cbw_ref[j:j + 1, :] * src
            acc = term if acc is None else acc + term
        yb = _silu(_layernorm(acc + cbb_ref[...], lbg_ref[...], lbb_ref[...]))
        y_ref[r0:r0 + sub, w:2 * w] = yb.astype(y_ref.dtype)

    for c0 in range(0, rows, chunk):
        for g in range(cg):
            lanes = slice(g * cgw, (g + 1) * cgw)
            s = jnp.dot(wm_ref[g], vn_buf[c0:c0 + chunk, lanes], preferred_element_type=F32)
            s = s + spbt_ref[:, g:g + 1]
            u = z_ref[c0:c0 + chunk, 5 * w + g * cgw:5 * w + (g + 1) * cgw].astype(F32)
            y_ref[c0:c0 + chunk, 2 * w + g * cgw:2 * w + (g + 1) * cgw] = (u * s).astype(y_ref.dtype)

    pos = i * rows + lax.broadcasted_iota(I32, (rows, 1), 0)
    for gi, win in enumerate(POOL_WINDOWS):
        lanes = slice(gi * gw, (gi + 1) * gw)
        tot = None
        for k in range(win):
            term = pd_ext[HIST - k:HIST - k + rows, lanes]
            tot = term if tot is None else tot + term
        cnt = jnp.minimum(pos + 1, win).astype(F32)
        d = tot / cnt - pd_ext[HIST:HIST + rows, lanes]
        yd = jnp.dot(d.astype(BF16), pw_ref[gi].astype(BF16), preferred_element_type=F32)
        y_ref[:, 3 * w + gi * gw:3 * w + (gi + 1) * gw] = (yd * ps_ref[:, lanes]).astype(y_ref.dtype)

    @pl.when(i == last)
    def _state():
        na_ref[...] = qa_ext[HIST + rows - (ka - 1):HIST + rows, :]
        nb_ref[...] = gb_ext[HIST + rows - (kb - 1):HIST + rows, :]
        nd_ref[...] = pd_ext[HIST + rows - nbuf_d:HIST + rows, :]

    qa_ext[0:HIST, :] = qa_ext[rows:rows + HIST, :]
    gb_ext[0:HIST, :] = gb_ext[rows:rows + HIST, :]
    pd_ext[0:HIST, :] = pd_ext[rows:rows + HIST, :]


def _mix_prompt(z, lw, *, batch, seq, rows):
    w = lw["conv_a_w"].shape[1]
    ka = lw["conv_a_w"].shape[0]
    kb = lw["conv_b_w"].shape[0]
    nbuf_d = max(POOL_WINDOWS) - 1
    cg, chunk = lw["sp_w"].shape[0], lw["sp_w"].shape[1]
    assert seq % rows == 0 and rows % chunk == 0 and kb - 1 <= HIST and nbuf_d <= HIST
    nt = seq // rows
    full2 = lambda a: pl.BlockSpec(a.shape, lambda b, i: (0, 0))
    full3 = lambda a: pl.BlockSpec(a.shape, lambda b, i: (0, 0, 0))
    vec = lambda a: a.reshape(1, w)
    small = [lw["conv_a_w"], lw["conv_b_w"], vec(lw["conv_b_b"]), vec(lw["ln_b_g"]), vec(lw["ln_b_b"]),
             vec(lw["ln_c_g"]), vec(lw["ln_c_b"])]
    spbt = lw["sp_b"].T
    in_specs = ([pl.BlockSpec((rows, 8 * w), lambda b, i: (b * nt + i, 0))]
                + [full2(a) for a in small]
                + [full3(lw["sp_w"]), full2(spbt), full3(lw["pool_w"]), full2(vec(lw["pool_scale"]))])
    out_shapes = (jax.ShapeDtypeStruct((batch * seq, 4 * w), BF16),
                  jax.ShapeDtypeStruct((batch, ka - 1, w), F32),
                  jax.ShapeDtypeStruct((batch, kb - 1, w), F32),
                  jax.ShapeDtypeStruct((batch, nbuf_d, w), F32))
    out_specs = (pl.BlockSpec((rows, 4 * w), lambda b, i: (b * nt + i, 0)),
                 pl.BlockSpec((None, ka - 1, w), lambda b, i: (b, 0, 0)),
                 pl.BlockSpec((None, kb - 1, w), lambda b, i: (b, 0, 0)),
                 pl.BlockSpec((None, nbuf_d, w), lambda b, i: (b, 0, 0)))
    scratch = [pltpu.VMEM((HIST + rows, w), F32), pltpu.VMEM((HIST + rows, w), F32),
               pltpu.VMEM((HIST + rows, w), F32), pltpu.VMEM((rows, w), BF16),
               pltpu.VMEM((cg, chunk, chunk), BF16),
               pltpu.VMEM((SUBLANES - 1, HIST + rows, w), F32)]
    return pl.pallas_call(
        functools.partial(_mix_prompt_body, rows=rows, width=w, ka=ka, kb=kb, nbuf_d=nbuf_d),
        grid=(batch, nt),
        in_specs=in_specs,
        out_specs=out_specs,
        out_shape=out_shapes,
        scratch_shapes=scratch,
        compiler_params=_cparams(("arbitrary", "arbitrary")),
        name="mix_prompt",
    )(z, *small, lw["sp_w"], spbt, lw["pool_w"], vec(lw["pool_scale"]))


def _mix_sample_body(spw_ref, spb_ref, z_ref, sa_ref, sb_ref, sd_ref, caw_ref, cbw_ref, cbb_ref,
                     lbg_ref, lbb_ref, lcg_ref, lcb_ref, pw_ref, ps_ref,
                     y_ref, na_ref, nb_ref, nd_ref, cv_ref, *, nseq, steps, width, ka, kb, nbuf_d, cg):
    w = width
    gw = w // len(POOL_WINDOWS)
    cgw = w // cg
    lane = lax.broadcasted_iota(I32, (1, w), 1)

    def col(t, k):
        return z_ref[t * nseq:(t + 1) * nseq, k * w:(k + 1) * w].astype(F32)

    def put(t, k, val):
        y_ref[t * nseq:(t + 1) * nseq, k * w:(k + 1) * w] = val.astype(y_ref.dtype)

    ext = [sa_ref[k] for k in range(ka - 1)] + [col(t, 0) * col(t, 1) for t in range(steps)]
    for t in range(steps):
        conv = None
        for j in range(ka):
            term = caw_ref[j:j + 1, :] * ext[t + j]
            conv = term if conv is None else conv + term
        put(t, 0, col(t, 2) * conv)
    for k in range(ka - 1):
        na_ref[k] = ext[steps + k]

    for t in range(steps):
        nb_ref[kb - 1 - steps + t] = col(t, 3) * jax.nn.sigmoid(col(t, 4))
    for k in range(kb - 1 - steps):
        nb_ref[k] = sb_ref[k + steps]

    def ext_b(k):
        return sb_ref[k] if k < kb - 1 else nb_ref[k - steps]

    for t in range(steps):
        acc = None
        for j in range(kb):
            term = cbw_ref[j:j + 1, :] * ext_b(t + j)
            acc = term if acc is None else acc + term
        put(t, 1, _silu(_layernorm(acc + cbb_ref[...], lbg_ref[...], lbb_ref[...])))

    def lane_groups(vals):
        out = jnp.full((1, w), vals[cg - 1], F32)
        for g in range(cg - 2, -1, -1):
            out = jnp.where(lane < (g + 1) * cgw, vals[g], out)
        return out

    for t in range(steps):
        cv_ref[t] = _layernorm(col(t, 6), lcg_ref[...], lcb_ref[...])
    for t in range(steps):
        s = lane_groups([spb_ref[g, t] for g in range(cg)])
        for sp in range(t + 1):
            s = s + lane_groups([spw_ref[g, t, sp] for g in range(cg)]) * cv_ref[sp]
        put(t, 2, col(t, 5) * s)

    for t in range(steps):
        nd_ref[nbuf_d - steps + t] = col(t, 7)
    for k in range(nbuf_d - steps):
        nd_ref[k] = sd_ref[k + steps]

    def ext_d(k):
        return sd_ref[k] if k < nbuf_d else nd_ref[k - steps]

    for t in range(steps):
        for gi, win in enumerate(POOL_WINDOWS):
            lanes = slice(gi * gw, (gi + 1) * gw)
            tot = None
            for k in range(win):
                term = ext_d(nbuf_d + t - k)[:, lanes]
                tot = term if tot is None else tot + term
            cnt = float(min(PAST_LEN + t + 1, win))
            d = tot / cnt - ext_d(nbuf_d + t)[:, lanes]
            yd = jnp.dot(d.astype(BF16), pw_ref[gi].astype(BF16), preferred_element_type=F32)
            y_ref[t * nseq:(t + 1) * nseq, 3 * w + gi * gw:3 * w + (gi + 1) * gw] = (
                yd * ps_ref[:, lanes]).astype(y_ref.dtype)


def _mix_sample(z, sa, sb, sd, lw, *, nseq, steps, m_prompt):
    w = lw["conv_a_w"].shape[1]
    ka = lw["conv_a_w"].shape[0]
    kb = lw["conv_b_w"].shape[0]
    nbuf_d = max(POOL_WINDOWS) - 1
    cg = lw["sp_w"].shape[0]
    ms = nseq * steps
    assert m_prompt % ms == 0 and steps <= nbuf_d and steps <= kb - 1
    rb = m_prompt // ms
    spw_small = lw["sp_w"][:, :steps, :steps]
    spb_small = lw["sp_b"][:, :steps]
    vec = lambda a: a.reshape(1, w)
    smem = pl.BlockSpec(memory_space=pltpu.SMEM)
    full2 = lambda a: pl.BlockSpec(a.shape, lambda i: (0, 0))
    full3 = lambda a: pl.BlockSpec(a.shape, lambda i: (0, 0, 0))
    small = [lw["conv_a_w"], lw["conv_b_w"], vec(lw["conv_b_b"]), vec(lw["ln_b_g"]), vec(lw["ln_b_b"]),
             vec(lw["ln_c_g"]), vec(lw["ln_c_b"])]
    in_specs = ([smem, smem,
                 pl.BlockSpec((ms, 8 * w), lambda i: (rb, 0)),
                 full3(sa), full3(sb), full3(sd)]
                + [full2(a) for a in small]
                + [full3(lw["pool_w"]), full2(vec(lw["pool_scale"]))])
    out_shapes = (jax.ShapeDtypeStruct((ms, 4 * w), BF16),
                  jax.ShapeDtypeStruct(sa.shape, F32), jax.ShapeDtypeStruct(sb.shape, F32),
                  jax.ShapeDtypeStruct(sd.shape, F32), jax.ShapeDtypeStruct((steps, nseq, w), F32))
    out_specs = (pl.BlockSpec((ms, 4 * w), lambda i: (0, 0)),
                 full3(sa), full3(sb), full3(sd),
                 pl.BlockSpec((steps, nseq, w), lambda i: (0, 0, 0)))
    return pl.pallas_call(
        functools.partial(_mix_sample_body, nseq=nseq, steps=steps, width=w, ka=ka, kb=kb,
                          nbuf_d=nbuf_d, cg=cg),
        grid=(1,),
        in_specs=in_specs,
        out_specs=out_specs,
        out_shape=out_shapes,
        compiler_params=_cparams(("arbitrary",)),
        name="mix_sample",
    )(spw_small, spb_small, z, sa, sb, sd, *small, lw["pool_w"], vec(lw["pool_scale"]))


def _merge_out_body(yp_ref, ys_ref, g0_ref, g1_ref, g2_ref, g3_ref, x_ref, gn_ref, wa_hbm, wb_hbm, wc_hbm, wd_hbm,
                    wo_hbm, *refs, width, prompt_tiles, layer, with_h):
    n_out = 2 if with_h else 1
    x1_ref = refs[0]
    wbr_ref, wout_ref, stage_ref, sem_ref = refs[n_out:n_out + 4]
    i = pl.program_id(0)
    rows = stage_ref.shape[1]
    d = wout_ref.shape[0]

    @pl.when(i == 0)
    def _load_weights():
        chunks = []
        for k, w_hbm in enumerate((wa_hbm, wb_hbm, wc_hbm, wd_hbm)):
            for r in range(0, width, rows):
                chunks.append((w_hbm.at[layer, pl.ds(r, rows), :], wbr_ref.at[k, pl.ds(r, rows), :]))
        for r in range(0, d, rows):
            chunks.append((wo_hbm.at[layer, pl.ds(r, rows), :], wout_ref.at[pl.ds(r, rows), :]))

        def fetch(c):
            return pltpu.make_async_copy(chunks[c][0], stage_ref.at[c % 2], sem_ref.at[c % 2])

        fetch(0).start(priority=0)
        for c in range(len(chunks)):
            if c + 1 < len(chunks):
                fetch(c + 1).start(priority=(c + 1) % 2)
            fetch(c).wait()
            chunks[c][1][...] = stage_ref[c % 2].astype(BF16)

    def run(y_ref):
        acc = None
        for k, g_ref in enumerate((g0_ref, g1_ref, g2_ref, g3_ref)):
            br = jnp.dot(y_ref[:, k * width:(k + 1) * width], wbr_ref[k], preferred_element_type=F32)
            term = jax.nn.sigmoid(g_ref[...].astype(F32)) * br
            acc = term if acc is None else acc + term
        x1 = x_ref[...] + jnp.dot(acc.astype(BF16), wout_ref[...], preferred_element_type=F32)
        x1_ref[...] = x1
        if with_h:
            refs[1][...] = _rms(x1, gn_ref[...]).astype(BF16)

    @pl.when(i < prompt_tiles)
    def _prompt():
        run(yp_ref)

    @pl.when(i >= prompt_tiles)
    def _sample():
        run(ys_ref)


def _merge_out(y_p, y_s, z, x, w_brs, w_out, layer, g, *, tm, with_h):
    mp, yw = y_p.shape
    ms = y_s.shape[0]
    assert mp % tm == 0 and ms % tm == 0
    tp = mp // tm
    w = yw // 4
    d = w_out.shape[2]
    assert (2 * yw) % d == 0 and w % STAGE_ROWS == 0 and d % STAGE_ROWS == 0
    goff = (2 * yw) // d
    gate_spec = lambda k: pl.BlockSpec((tm, d), lambda i, k=k: (i, goff + k))
    row = pl.BlockSpec((tm, d), lambda i: (i, 0))
    hbm = pl.BlockSpec(memory_space=pl.ANY)
    out_shape = [jax.ShapeDtypeStruct((mp + ms, d), F32)]
    out_specs = [row]
    if with_h:
        out_shape.append(jax.ShapeDtypeStruct((mp + ms, d), BF16))
        out_specs.append(row)
    return pl.pallas_call(
        functools.partial(_merge_out_body, width=w, prompt_tiles=tp, layer=layer, with_h=with_h),
        grid=((mp + ms) // tm,),
        in_specs=[pl.BlockSpec((tm, yw), lambda i: (jnp.minimum(i, tp - 1), 0)),
                  pl.BlockSpec((tm, yw), lambda i: (jnp.maximum(i - tp, 0), 0))]
                 + [gate_spec(k) for k in range(4)]
                 + [row, pl.BlockSpec((1, d), lambda i: (0, 0))] + [hbm] * 5,
        out_specs=out_specs,
        out_shape=out_shape,
        scratch_shapes=[pltpu.VMEM((4, w, d), BF16), pltpu.VMEM((d, d), BF16),
                        pltpu.VMEM((2, STAGE_ROWS, d), F32), pltpu.SemaphoreType.DMA((2,))],
        compiler_params=_cparams(("arbitrary",)),
        name="merge_out",
    )(y_p, y_s, z, z, z, z, x, g.reshape(1, d), *w_brs, w_out)


def _route_body(x_ref, g_ref, rwt_ref, e_ref, r_ref, wt_ref, cnt_ref, carry_ref, tri_ref):
    i = pl.program_id(0)
    ne = rwt_ref.shape[0]
    tm = x_ref.shape[0]

    @pl.when(i == 0)
    def _init():
        carry_ref[...] = jnp.zeros_like(carry_ref)
        upper = lax.broadcasted_iota(I32, (tm, tm), 0) <= lax.broadcasted_iota(I32, (tm, tm), 1)
        tri_ref[...] = jnp.where(upper, 1.0, 0.0).astype(BF16)

    h = _rms(x_ref[...], g_ref[...])
    h_hi = h.astype(BF16)
    h_lo = (h - h_hi.astype(F32)).astype(BF16)
    rw = rwt_ref[...]
    rw_hi = rw.astype(BF16)
    rw_lo = (rw - rw_hi.astype(F32)).astype(BF16)
    dn = (((1,), (1,)), ((), ()))
    dg = lambda a, b: lax.dot_general(a, b, dn, preferred_element_type=F32)
    logits = dg(rw_hi, h_hi) + dg(rw_hi, h_lo) + dg(rw_lo, h_hi)
    ex = jnp.exp(logits - jnp.max(logits, axis=0, keepdims=True))
    p = ex / jnp.sum(ex, axis=0, keepdims=True)
    eid = lax.broadcasted_iota(I32, (ne, tm), 0)
    m1 = jnp.max(p, axis=0, keepdims=True)
    i1 = jnp.min(jnp.where(p == m1, eid, ne), axis=0, keepdims=True)
    oh1 = eid == i1
    p2 = jnp.where(oh1, -1.0, p)
    m2 = jnp.max(p2, axis=0, keepdims=True)
    i2 = jnp.min(jnp.where(p2 == m2, eid, ne), axis=0, keepdims=True)
    oh2 = eid == i2
    den = m1 + m2
    sel = jnp.logical_or(oh1, oh2)
    sel_f = jnp.where(sel, 1.0, 0.0)
    cum = jnp.dot(sel_f.astype(BF16), tri_ref[...], preferred_element_type=F32) + carry_ref[...]
    rank1 = jnp.sum(jnp.where(oh1, cum, 0.0), axis=0, keepdims=True) - 1.0
    rank2 = jnp.sum(jnp.where(oh2, cum, 0.0), axis=0, keepdims=True) - 1.0
    carry_ref[...] = carry_ref[...] + jnp.sum(sel_f, axis=1, keepdims=True)
    e_ref[0:1, :] = i1
    e_ref[1:2, :] = i2
    r_ref[0:1, :] = rank1.astype(I32)
    r_ref[1:2, :] = rank2.astype(I32)
    wt_ref[0:1, :] = m1 / den
    wt_ref[1:2, :] = m2 / den

    @pl.when(i == pl.num_programs(0) - 1)
    def _counts():
        cnt_ref[...] = jnp.broadcast_to(carry_ref[...], cnt_ref.shape).astype(I32)


def _route(x1, g, router_w, *, tm):
    m, d = x1.shape
    ne = router_w.shape[1]
    tok = lambda dt: jax.ShapeDtypeStruct((TOP_K, m), dt)
    tok_spec = pl.BlockSpec((TOP_K, tm), lambda i: (0, i))
    return pl.pallas_call(
        _route_body,
        grid=(m // tm,),
        in_specs=[pl.BlockSpec((tm, d), lambda i: (i, 0)),
                  pl.BlockSpec((1, d), lambda i: (0, 0)),
                  pl.BlockSpec((ne, d), lambda i: (0, 0))],
        out_specs=(tok_spec, tok_spec, tok_spec, pl.BlockSpec((ne, LANES), lambda i: (0, 0))),
        out_shape=(tok(I32), tok(I32), tok(F32), jax.ShapeDtypeStruct((ne, LANES), I32)),
        scratch_shapes=[pltpu.VMEM((ne, 1), F32), pltpu.VMEM((tm, tm), BF16)],
        compiler_params=_cparams(("arbitrary",)),
        name="route",
    )(x1, g.reshape(1, d), router_w.T)


def _plan_body(e_ref, r_ref, cnt_ref, pos_ref, tgrp_ref, tfirst_ref, misc_ref, *, tile_shift):
    ne = cnt_ref.shape[0]
    tile = 1 << tile_shift
    sub = lax.broadcasted_iota(I32, (ne, LANES), 0)
    lane = lax.broadcasted_iota(I32, (ne, LANES), 1)

    def as_row(col):
        return jnp.sum(jnp.where(sub == lane, col, 0), axis=0, keepdims=True)

    cnt = cnt_ref[:, 0:1]
    padded = lax.shift_left(lax.shift_right_logical(cnt + (tile - 1), tile_shift), tile_shift)
    gstart = jnp.sum(jnp.where(lane < sub, as_row(padded), 0), axis=1, keepdims=True)
    gend = gstart + padded
    m = e_ref.shape[1]
    eid = lax.broadcasted_iota(I32, (ne, m), 0)
    for k in range(TOP_K):
        base = jnp.sum(jnp.where(eid == e_ref[k:k + 1, :], gstart, 0), axis=0, keepdims=True)
        pos_ref[k:k + 1, :] = base + r_ref[k:k + 1, :]
    start = lax.broadcasted_iota(I32, (1, LANES), 1) * tile
    grp = jnp.sum((gend <= start).astype(I32), axis=0, keepdims=True)
    tgrp_ref[...] = jnp.minimum(grp, ne - 1)
    tfirst_ref[...] = jnp.sum(jnp.logical_and(gstart == start, padded > 0).astype(I32), axis=0, keepdims=True)
    n_tiles = lax.shift_right_logical(jnp.max(gend, axis=0, keepdims=True), tile_shift)
    misc = jnp.where(sub == 0, n_tiles, 0)
    misc = jnp.where(sub == 1, as_row(gstart + cnt), misc)
    misc = jnp.where(sub == 2, as_row(gend), misc)
    lane_f = lane.astype(F32)
    nonempty = jnp.logical_and(as_row(padded) > 0, lane < ne)
    later = jnp.min(jnp.where(jnp.logical_and(nonempty, lane > sub), lane_f, float(ne)), axis=1, keepdims=True)
    firstg = jnp.min(jnp.where(nonempty, lane_f, float(ne)), axis=1, keepdims=True)
    nxt = jnp.where(later < ne, later, firstg).astype(I32)
    wraps = (later >= ne).astype(I32)
    mine = sub == jnp.minimum(grp, ne - 1)
    misc = jnp.where(sub == 3, jnp.sum(jnp.where(mine, nxt, 0), axis=0, keepdims=True), misc)
    misc = jnp.where(sub == 4, jnp.sum(jnp.where(mine, wraps, 0), axis=0, keepdims=True), misc)
    misc_ref[...] = misc


def _plan(e, r, cnt, *, tile):
    tile_shift = tile.bit_length() - 1
    assert 1 << tile_shift == tile
    ne = cnt.shape[0]
    m = e.shape[1]
    whole = lambda a: pl.BlockSpec(a.shape, lambda i: (0, 0))
    row = jax.ShapeDtypeStruct((1, LANES), I32)
    out_shape = (jax.ShapeDtypeStruct((TOP_K, m), I32), row, row, jax.ShapeDtypeStruct((ne, LANES), I32))
    return pl.pallas_call(
        functools.partial(_plan_body, tile_shift=tile_shift),
        grid=(1,),
        in_specs=[whole(e), whole(r), whole(cnt)],
        out_specs=tuple(pl.BlockSpec(s.shape, lambda i: (0, 0)) for s in out_shape),
        out_shape=out_shape,
        compiler_params=_cparams(("arbitrary",)),
        name="plan",
    )(e, r, cnt)


def _dispatch_body(pos0_ref, pos1_ref, nv_ref, fill_ref, end_ref, x_hbm, g_ref, o_ref, src_ref, buf_ref, sem_ref):
    j = pl.program_id(0)
    tm = o_ref.shape[0]
    n_tok = pos0_ref.shape[0]
    nv = nv_ref[0]

    def start_tile(tile, slot):
        def body(pair, carry):
            for p in range(2):
                r = pair * 2 + p
                tok = src_ref[tile * tm + r]
                pltpu.make_async_copy(x_hbm.at[pl.ds(tok, 1), :], buf_ref.at[slot, pl.ds(r, 1), :],
                                      sem_ref.at[slot]).start(priority=p)
            return carry
        lax.fori_loop(0, tm // 2, body, 0, unroll=ISSUE_UNROLL // 2)

    @pl.when(j == 0)
    def _prologue():
        for g in range(fill_ref.shape[0]):
            def fill(p, carry):
                src_ref[p] = 0
                return carry
            lax.fori_loop(fill_ref[g], end_ref[g], fill, 0)

        def inv(t, carry):
            src_ref[pos0_ref[t]] = t
            src_ref[pos1_ref[t]] = t
            return carry
        lax.fori_loop(0, n_tok, inv, 0, unroll=ISSUE_UNROLL)
        start_tile(0, 0)

    slot = lax.rem(j, 2)

    @pl.when(j < nv)
    def _work():
        @pl.when(j + 1 < nv)
        def _prefetch():
            start_tile(j + 1, 1 - slot)

        pltpu.make_async_copy(x_hbm.at[pl.ds(0, tm), :], buf_ref.at[slot], sem_ref.at[slot]).wait()
        o_ref[...] = _rms(buf_ref[slot], g_ref[...]).astype(o_ref.dtype)

    @pl.when(j >= nv)
    def _skip():
        o_ref[...] = jnp.zeros_like(o_ref)


def _dispatch(x1, g, pos0, pos1, n_tiles, fill_row, end_row, *, tm, max_tiles):
    m, d = x1.shape
    grid_spec = pltpu.PrefetchScalarGridSpec(
        num_scalar_prefetch=5,
        grid=(max_tiles,),
        in_specs=[pl.BlockSpec(memory_space=pl.ANY),
                  pl.BlockSpec((1, d), lambda j, *_: (0, 0))],
        out_specs=pl.BlockSpec((tm, d), lambda j, *_: (j, 0)),
        scratch_shapes=[pltpu.SMEM((max_tiles * tm,), I32),
                        pltpu.VMEM((2, tm, d), F32),
                        pltpu.SemaphoreType.DMA((2,))],
    )
    return pl.pallas_call(
        _dispatch_body,
        grid_spec=grid_spec,
        out_shape=jax.ShapeDtypeStruct((max_tiles * tm, d), BF16),
        compiler_params=_cparams(("arbitrary",)),
        name="dispatch",
    )(pos0, pos1, n_tiles, fill_row, end_row, x1, g.reshape(1, d))


def _combine_body(pos0_ref, pos1_ref, ys_hbm, x_ref, wt_ref, g_ref, xn_ref, h_ref, buf_ref, sem_ref):
    i = pl.program_id(0)
    n = pl.num_programs(0)
    tm = x_ref.shape[0]

    def start_tile(tile, slot):
        def body(r, carry):
            t = tile * tm + r
            pltpu.make_async_copy(ys_hbm.at[pl.ds(pos0_ref[t], 1), :], buf_ref.at[slot, 0, pl.ds(r, 1), :],
                                  sem_ref.at[slot, 0]).start(priority=0)
            pltpu.make_async_copy(ys_hbm.at[pl.ds(pos1_ref[t], 1), :], buf_ref.at[slot, 1, pl.ds(r, 1), :],
                                  sem_ref.at[slot, 1]).start(priority=1)
            return carry
        lax.fori_loop(0, tm, body, 0, unroll=ISSUE_UNROLL)

    @pl.when(i == 0)
    def _prologue():
        start_tile(0, 0)

    slot = lax.rem(i, 2)

    @pl.when(i + 1 < n)
    def _prefetch():
        start_tile(i + 1, 1 - slot)

    for k in range(TOP_K):
        pltpu.make_async_copy(ys_hbm.at[pl.ds(0, tm), :], buf_ref.at[slot, k], sem_ref.at[slot, k]).wait()
    xn = x_ref[...] + wt_ref[:, 0:1] * buf_ref[slot, 0] + wt_ref[:, 1:2] * buf_ref[slot, 1]
    xn_ref[...] = xn
    h_ref[...] = _rms(xn, g_ref[...]).astype(h_ref.dtype)


def _combine(ys, x1, wt_cols, g, pos0, pos1, *, tm, h_dtype):
    m, d = x1.shape
    row = lambda: pl.BlockSpec((tm, d), lambda i, *_: (i, 0))
    grid_spec = pltpu.PrefetchScalarGridSpec(
        num_scalar_prefetch=2,
        grid=(m // tm,),
        in_specs=[pl.BlockSpec(memory_space=pl.ANY), row(),
                  pl.BlockSpec((tm, TOP_K), lambda i, *_: (i, 0)),
                  pl.BlockSpec((1, d), lambda i, *_: (0, 0))],
        out_specs=(row(), row()),
        scratch_shapes=[pltpu.VMEM((2, TOP_K, tm, d), F32), pltpu.SemaphoreType.DMA((2, TOP_K))],
    )
    return pl.pallas_call(
        _combine_body,
        grid_spec=grid_spec,
        out_shape=(jax.ShapeDtypeStruct((m, d), F32), jax.ShapeDtypeStruct((m, d), h_dtype)),
        compiler_params=_cparams(("arbitrary",)),
        name="combine",
    )(pos0, pos1, ys, x1, wt_cols, g.reshape(1, d))


def _pick(total, prefs):
    for p in prefs:
        if total % p == 0:
            return p
    raise ValueError(f"no tile for {total} in {prefs}")


def _row_tile(m, cap):
    best = None
    for t in range(16, cap + 1, 16):
        if m % t == 0:
            best = t
    assert best is not None
    return best


def kernel(x_prompt, x_sample, state_conv_a, state_conv_b, state_pool, norm_mix_g, w_in, conv_a_w, w_br_a,
           conv_b_w, conv_b_b, ln_b_g, ln_b_b, w_br_b, ln_c_g, ln_c_b, sp_w, sp_b, w_br_c, pool_w,
           pool_scale, w_br_d, w_out, norm_ffn_g, ffn_w1, ffn_w3, ffn_w2, router_w, moe_w1, moe_w3, moe_w2,
           final_norm_g):
    batch, seq, d = x_prompt.shape
    nseq, steps, _ = x_sample.shape
    depth = w_in.shape[0]
    n_exp = router_w.shape[2]
    mp, ms = batch * seq, nseq * steps
    m = mp + ms

    x = jnp.concatenate([x_prompt.reshape(mp, d), jnp.swapaxes(x_sample, 0, 1).reshape(ms, d)], axis=0)
    st_a = jnp.swapaxes(state_conv_a, 1, 2)
    st_b = jnp.swapaxes(state_conv_b, 1, 2)
    st_d = jnp.swapaxes(state_pool, 1, 2)

    tm_big = _row_tile(m, 1088)
    tm_mid = _row_tile(m, 544)
    tm_small = _row_tile(m, 256)
    tm_merge = min(_row_tile(mp, 256), _row_tile(ms, 256))
    assert mp % tm_merge == 0 and ms % tm_merge == 0
    tm_moe = 512
    max_tiles = -(-(m * TOP_K) // tm_moe) + n_exp
    assert max_tiles <= LANES
    mix_rows = _pick(seq, (256, 128))
    moe_w1f = moe_w1.reshape((-1,) + moe_w1.shape[2:])
    moe_w3f = moe_w3.reshape((-1,) + moe_w3.shape[2:])
    moe_w2f = moe_w2.reshape((-1,) + moe_w2.shape[2:])
    dffe = moe_w1.shape[3]
    dff = ffn_w1.shape[2]

    h = _rmsnorm(x, norm_mix_g[0], BF16, tm_mid)
    outs_p = ([], [], [])
    outs_s = ([], [], [], [])
    y_final = None
    for i in range(depth):
        lw = dict(conv_a_w=conv_a_w[i], conv_b_w=conv_b_w[i], conv_b_b=conv_b_b[i], ln_b_g=ln_b_g[i],
                  ln_b_b=ln_b_b[i], ln_c_g=ln_c_g[i], ln_c_b=ln_c_b[i], sp_w=sp_w[i], sp_b=sp_b[i],
                  pool_w=pool_w[i], pool_scale=pool_scale[i])
        z = _dense_stream_matmul(h, w_in, i, tm=tm_big, tn=_pick(w_in.shape[2], (2048, 1024, 512, 256, 128)),
                                 mode="plain", out_dtype=BF16, name="inproj")
        y_p, na_p, nb_p, nd_p = _mix_prompt(z, lw, batch=batch, seq=seq, rows=mix_rows)
        y_s, na_s, nb_s, nd_s, cv_s = _mix_sample(z, st_a[i], st_b[i], st_d[i], lw, nseq=nseq, steps=steps,
                                                  m_prompt=mp)
        for acc, val in zip(outs_p, (na_p, nb_p, nd_p)):
            acc.append(val)
        for acc, val in zip(outs_s, (na_s, nb_s, nd_s, cv_s)):
            acc.append(val)
        merge_out = functools.partial(_merge_out, y_p, y_s, z, x, (w_br_a, w_br_b, w_br_c, w_br_d), w_out, i,
                                      norm_ffn_g[i], tm=tm_merge)
        last = i == depth - 1
        g_next = final_norm_g if last else norm_mix_g[i + 1]
        h_dtype = F32 if last else BF16
        j = i // 2
        if i % 2 == 0:
            x1, h2 = merge_out(with_h=True)
            a = _dense_stream_matmul(h2, ffn_w1, j, w2=ffn_w3, tm=tm_big, tn=_pick(dff, (512, 256, 128)),
                                     mode="gated", out_dtype=BF16, name="ffn_up")
            x = _dense_stream_matmul(a, ffn_w2, j, tm=tm_big, tn=_pick(d, (512, 256, 128)), mode="resid",
                                     out_dtype=F32, res=x1, name="ffn_down")
            h = _rmsnorm(x, g_next, h_dtype, tm_mid)
        else:
            (x1,) = merge_out(with_h=False)
            e, r, wt, cnt = _route(x1, norm_ffn_g[i], router_w[j], tm=_pick(m, (512, 256, 128)))
            pos, tgrp, tfirst, misc = _plan(e, r, cnt, tile=tm_moe)
            n_tiles = misc[0, 0:1]
            hs = _dispatch(x1, norm_ffn_g[i], pos[0], pos[1], n_tiles, misc[1, :n_exp], misc[2, :n_exp],
                           tm=tm_moe, max_tiles=max_tiles)
            grp = tgrp[0, :max_tiles] + j * n_exp
            first = tfirst[0, :max_tiles]
            nextg = misc[3, :max_tiles] + j * n_exp
            wrap = misc[4, :max_tiles]
            a = _stream_matmul(hs, [moe_w1f, moe_w3f], grp, first, nextg, wrap, n_tiles, tm=tm_moe,
                               tn=_pick(dffe, (1408, 1024, 512, 256, 128)), mode="gated", out_dtype=BF16,
                               name="moe_up", step_rows=_stream_step_rows(d, MOE_STREAM_STEPS))
            ys = _stream_matmul(a, [moe_w2f], grp, first, nextg, wrap, n_tiles, tm=tm_moe,
                                tn=_pick(d, (1024, 512, 256, 128)), mode="plain", out_dtype=F32, name="moe_down",
                                step_rows=_stream_step_rows(dffe, MOE_STREAM_STEPS))
            x, h = _combine(ys, x1, wt.T, g_next, pos[0], pos[1], tm=tm_small, h_dtype=h_dtype)
        if last:
            y_final = h

    y_prompt = y_final[:mp].reshape(batch, seq, d)
    y_sample = jnp.swapaxes(y_final[mp:].reshape(steps, nseq, d), 0, 1)
    stack_s = lambda vals: jnp.swapaxes(jnp.stack(vals), 1, 2)
    return (y_prompt, y_sample, jnp.stack(outs_p[0]), jnp.stack(outs_p[1]), jnp.stack(outs_p[2]),
            stack_s(outs_s[0]), stack_s(outs_s[1]), stack_s(outs_s[2]), stack_s(outs_s[3]))
```

```python
import functools

import jax
import jax.numpy as jnp
from jax import lax
from jax.experimental import pallas as pl
from jax.experimental.pallas import tpu as pltpu

F32 = jnp.float32
BF16 = jnp.bfloat16
I32 = jnp.int32

EPS = 1e-6
PAST_LEN = 16384
POOL_WINDOWS = (2, 4, 8, 16)
TOP_K = 2
LANES = 128
SUBLANES = 8
HIST = 32
ISSUE_UNROLL = 8
VMEM_LIMIT = 56 * 1024 * 1024


def _cparams(sem, vmem=VMEM_LIMIT):
    return pltpu.CompilerParams(dimension_semantics=sem, vmem_limit_bytes=vmem)


def _rms(x, g):
    return x * lax.rsqrt(jnp.mean(x * x, axis=-1, keepdims=True) + EPS) * g


def _layernorm(x, g, b):
    mu = jnp.mean(x, axis=-1, keepdims=True)
    xc = x - mu
    var = jnp.mean(xc * xc, axis=-1, keepdims=True)
    return xc * lax.rsqrt(var + EPS) * g + b


def _silu(x):
    return x * jax.nn.sigmoid(x)


def _rmsnorm_body(x_ref, g_ref, o_ref):
    o_ref[...] = _rms(x_ref[...], g_ref[...]).astype(o_ref.dtype)


def _rmsnorm(x, g, out_dtype, tm):
    m, d = x.shape
    return pl.pallas_call(
        _rmsnorm_body,
        grid=(m // tm,),
        in_specs=[pl.BlockSpec((tm, d), lambda i: (i, 0)),
                  pl.BlockSpec((1, d), lambda i: (0, 0))],
        out_specs=pl.BlockSpec((tm, d), lambda i: (i, 0)),
        out_shape=jax.ShapeDtypeStruct((m, d), out_dtype),
        compiler_params=_cparams(("arbitrary",)),
        name="rmsnorm",
    )(x, g.reshape(1, d))


STAGE_ROWS = 256
MOE_STREAM_STEPS = 4
STREAM_PIECES = 2


_SLOT, _ISSUED, _DONE, _NEXT_GRP, _NEXT_COL, _HAS_NEXT = range(6)


def _stream_body(grp_ref, first_ref, nextg_ref, wrap_ref, nv_ref, x_ref, *refs, n_w, mode, step_rows):
    res_ref = refs[0] if mode == "resid" else None
    refs = refs[1:] if mode == "resid" else refs
    w_hbm = refs[:n_w]
    o_ref, wb_ref, stage_ref, st_ref, sem_ref = refs[n_w:n_w + 5]
    n = pl.program_id(0)
    m = pl.program_id(1)
    k, tn = wb_ref.shape[2], wb_ref.shape[3]
    n_steps = k // step_rows
    part = step_rows // STREAM_PIECES
    valid = m < nv_ref[0]

    def pieces(step):
        row0 = pl.multiple_of(step * step_rows, 2 * SUBLANES)
        col = pl.multiple_of(st_ref[_NEXT_COL] * tn, LANES)
        out = []
        for j in range(n_w):
            for p in range(STREAM_PIECES):
                src = w_hbm[j].at[st_ref[_NEXT_GRP], pl.ds(row0 + p * part, part), pl.ds(col, tn)]
                dst = stage_ref.at[lax.rem(step, 2), j, pl.ds(p * part, part), :]
                out.append(pltpu.make_async_copy(src, dst, sem_ref.at[0]))
        return out

    def start(step):
        for c in pieces(step):
            c.start(priority=1)

    def wait(step):
        for c in pieces(step):
            c.wait()

    def cast(slot, step):
        row0 = pl.multiple_of(step * step_rows, 2 * SUBLANES)
        for j in range(n_w):
            wb_ref[slot, j, pl.ds(row0, step_rows), :] = stage_ref[lax.rem(step, 2), j].astype(BF16)

    def compute(slot):
        x = x_ref[...]
        if mode == "gated":
            a = jnp.dot(x, wb_ref[slot, 0], preferred_element_type=F32)
            b = jnp.dot(x, wb_ref[slot, 1], preferred_element_type=F32)
            o_ref[...] = (_silu(a) * b).astype(o_ref.dtype)
        else:
            acc = jnp.dot(x, wb_ref[slot, 0], preferred_element_type=F32)
            if mode == "resid":
                acc = res_ref[...] + acc
            o_ref[...] = acc.astype(o_ref.dtype)

    @pl.when(jnp.logical_and(valid, first_ref[m] == 1))
    def _switch():
        @pl.when(jnp.logical_and(n == 0, m == 0))
        def _boot():
            st_ref[_SLOT] = 1
            st_ref[_ISSUED] = 0
            st_ref[_DONE] = 0
            st_ref[_NEXT_GRP] = grp_ref[0]
            st_ref[_NEXT_COL] = 0

        def finish(step, carry):
            @pl.when(st_ref[_ISSUED] <= step)
            def _fetch():
                start(step)
                wait(step)
            cast(1 - st_ref[_SLOT], step)
            return carry
        lax.fori_loop(st_ref[_DONE], n_steps, finish, 0)

        st_ref[_SLOT] = 1 - st_ref[_SLOT]
        st_ref[_ISSUED] = 0
        st_ref[_DONE] = 0
        st_ref[_NEXT_GRP] = nextg_ref[m]
        st_ref[_NEXT_COL] = n + wrap_ref[m]
        last_col = n == pl.num_programs(0) - 1
        st_ref[_HAS_NEXT] = jnp.where(jnp.logical_and(wrap_ref[m] == 1, last_col), 0, 1)

    slot = st_ref[_SLOT]
    issued = st_ref[_ISSUED]
    done = st_ref[_DONE]
    in_flight = done < issued
    more = jnp.logical_and(st_ref[_HAS_NEXT] == 1, issued < n_steps)

    for do_cast in (False, True):
        for do_start in (False, True):
            @pl.when(jnp.logical_and(valid, jnp.logical_and(in_flight == do_cast, more == do_start)))
            def _step(do_cast=do_cast, do_start=do_start):
                if do_start:
                    start(issued)
                if do_cast:
                    cast(1 - slot, done)
                    st_ref[_DONE] = done + 1
                compute(slot)
                if do_start:
                    wait(issued)
                    st_ref[_ISSUED] = issued + 1

    @pl.when(jnp.logical_not(valid))
    def _zero():
        o_ref[...] = jnp.zeros_like(o_ref)


def _stream_matmul(x, ws, tile_grp, tile_first, tile_next, tile_wrap, n_valid, *, tm, tn, mode, out_dtype, name,
                   step_rows, res=None):
    m, k = x.shape
    n = ws[0].shape[2]
    assert m % tm == 0 and n % tn == 0 and tn % LANES == 0
    assert k % step_rows == 0 and step_rows % (STREAM_PIECES * SUBLANES) == 0
    n_w = len(ws)
    row_of = lambda mi, nv: jnp.minimum(mi, nv[0] - 1)
    extra_specs, extra_args = [], []
    if mode == "resid":
        extra_specs = [pl.BlockSpec((tm, tn), lambda ni, mi, g, f, nx, wr, nv: (row_of(mi, nv), ni))]
        extra_args = [res]
    grid_spec = pltpu.PrefetchScalarGridSpec(
        num_scalar_prefetch=5,
        grid=(n // tn, m // tm),
        in_specs=[pl.BlockSpec((tm, k), lambda ni, mi, g, f, nx, wr, nv: (row_of(mi, nv), 0))]
                 + extra_specs + [pl.BlockSpec(memory_space=pl.ANY)] * n_w,
        out_specs=pl.BlockSpec((tm, tn), lambda ni, mi, g, f, nx, wr, nv: (mi, ni)),
        scratch_shapes=[pltpu.VMEM((2, n_w, k, tn), BF16), pltpu.VMEM((2, n_w, step_rows, tn), F32),
                        pltpu.SMEM((8,), I32), pltpu.SemaphoreType.DMA((1,))],
    )
    return pl.pallas_call(
        functools.partial(_stream_body, n_w=n_w, mode=mode, step_rows=step_rows),
        grid_spec=grid_spec,
        out_shape=jax.ShapeDtypeStruct((m, n), out_dtype),
        compiler_params=_cparams(("arbitrary", "arbitrary"), vmem=60 * 1024 * 1024),
        name=name,
    )(tile_grp, tile_first, tile_next, tile_wrap, n_valid, x, *extra_args, *ws)


def _dense_stream_matmul(x, w, layer, *, tm, tn, mode, out_dtype, res=None, w2=None, name):
    tiles = x.shape[0] // tm
    grp = jnp.full((tiles,), layer, I32)
    first = jnp.zeros((tiles,), I32).at[0].set(1)
    wrap = jnp.ones((tiles,), I32)
    nv = jnp.full((1,), tiles, I32)
    ws = [w] if w2 is None else [w, w2]
    step_rows = _stream_step_rows(w.shape[1], tiles - 1)
    return _stream_matmul(x, ws, grp, first, grp, wrap, nv, tm=tm, tn=tn, mode=mode, out_dtype=out_dtype,
                          res=res, name=name, step_rows=step_rows)


def _stream_step_rows(k, steps_available):
    best = k
    for steps in range(1, max(steps_available, 1) + 1):
        if k % steps == 0 and (k // steps) % (4 * SUBLANES) == 0:
            best = k // steps
    return best


def _mix_prompt_body(z_ref, caw_ref, cbw_ref, cbb_ref, lbg_ref, lbb_ref, lcg_ref, lcb_ref, spw_ref,
                     spbt_ref, pw_ref, ps_ref, y_ref, na_ref, nb_ref, nd_ref,
                     qa_ext, gb_ext, pd_ext, vn_buf, wm_ref, gsh_ref, *, rows, width, ka, kb, nbuf_d):
    i = pl.program_id(1)
    last = pl.num_programs(1) - 1
    w = width
    gw = w // len(POOL_WINDOWS)
    chunk = wm_ref.shape[1]
    cg = wm_ref.shape[0]
    cgw = w // cg
    sub = 32

    @pl.when(i == 0)
    def _init():
        zeros = jnp.zeros((HIST, w), F32)
        qa_ext[0:HIST, :] = zeros
        gb_ext[0:HIST, :] = zeros
        pd_ext[0:HIST, :] = zeros
        tri = lax.broadcasted_iota(I32, (chunk, chunk), 0) >= lax.broadcasted_iota(I32, (chunk, chunk), 1)
        for g in range(cg):
            wm_ref[g] = jnp.where(tri, spw_ref[g], 0.0).astype(BF16)

    def col(k):
        return z_ref[:, k * w:(k + 1) * w].astype(F32)

    qa_ext[HIST:HIST + rows, :] = col(0) * col(1)
    gb_ext[HIST:HIST + rows, :] = col(3) * jax.nn.sigmoid(col(4))
    pd_ext[HIST:HIST + rows, :] = col(7)
    vn_buf[...] = _layernorm(col(6), lcg_ref[...], lcb_ref[...]).astype(BF16)

    conv_a = None
    for j in range(ka):
        term = caw_ref[j:j + 1, :] * qa_ext[HIST - (ka - 1) + j:HIST - (ka - 1) + j + rows, :]
        conv_a = term if conv_a is None else conv_a + term
    y_ref[:, 0:w] = (col(2) * conv_a).astype(y_ref.dtype)

    for f in range(1, SUBLANES):
        gsh_ref[f - 1, SUBLANES:HIST + rows, :] = gb_ext[SUBLANES - f:HIST + rows - f, :]
    for r0 in range(0, rows, sub):
        acc = None
        for j in range(kb):
            back = kb - 1 - j
            f, s = back % SUBLANES, HIST + r0 - (back - back % SUBLANES)
            src = gb_ext[s:s + sub, :] if f == 0 else gsh_ref[f - 1, s:s + sub, :]
            term = cbw_ref[j:j + 1, :] * src
            acc = term if acc is None else acc + term
        yb = _silu(_layernorm(acc + cbb_ref[...], lbg_ref[...], lbb_ref[...]))
        y_ref[r0:r0 + sub, w:2 * w] = yb.astype(y_ref.dtype)

    for c0 in range(0, rows, chunk):
        for g in range(cg):
            lanes = slice(g * cgw, (g + 1) * cgw)
            s = jnp.dot(wm_ref[g], vn_buf[c0:c0 + chunk, lanes], preferred_element_type=F32)
            s = s + spbt_ref[:, g:g + 1]
            u = z_ref[c0:c0 + chunk, 5 * w + g * cgw:5 * w + (g + 1) * cgw].astype(F32)
            y_ref[c0:c0 + chunk, 2 * w + g * cgw:2 * w + (g + 1) * cgw] = (u * s).astype(y_ref.dtype)

    pos = i * rows + lax.broadcasted_iota(I32, (rows, 1), 0)
    for gi, win in enumerate(POOL_WINDOWS):
        lanes = slice(gi * gw, (gi + 1) * gw)
        tot = None
        for k in range(win):
            term = pd_ext[HIST - k:HIST - k + rows, lanes]
            tot = term if tot is None else tot + term
        cnt = jnp.minimum(pos + 1, win).astype(F32)
        d = tot / cnt - pd_ext[HIST:HIST + rows, lanes]
        yd = jnp.dot(d.astype(BF16), pw_ref[gi].astype(BF16), preferred_element_type=F32)
        y_ref[:, 3 * w + gi * gw:3 * w + (gi + 1) * gw] = (yd * ps_ref[:, lanes]).astype(y_ref.dtype)

    @pl.when(i == last)
    def _state():
        na_ref[...] = qa_ext[HIST + rows - (ka - 1):HIST + rows, :]
        nb_ref[...] = gb_ext[HIST + rows - (kb - 1):HIST + rows, :]
        nd_ref[...] = pd_ext[HIST + rows - nbuf_d:HIST + rows, :]

    qa_ext[0:HIST, :] = qa_ext[rows:rows + HIST, :]
    gb_ext[0:HIST, :] = gb_ext[rows:rows + HIST, :]
    pd_ext[0:HIST, :] = pd_ext[rows:rows + HIST, :]


def _mix_prompt(z, lw, *, batch, seq, rows):
    w = lw["conv_a_w"].shape[1]
    ka = lw["conv_a_w"].shape[0]
    kb = lw["conv_b_w"].shape[0]
    nbuf_d = max(POOL_WINDOWS) - 1
    cg, chunk = lw["sp_w"].shape[0], lw["sp_w"].shape[1]
    assert seq % rows == 0 and rows % chunk == 0 and kb - 1 <= HIST and nbuf_d <= HIST
    nt = seq // rows
    full2 = lambda a: pl.BlockSpec(a.shape, lambda b, i: (0, 0))
    full3 = lambda a: pl.BlockSpec(a.shape, lambda b, i: (0, 0, 0))
    vec = lambda a: a.reshape(1, w)
    small = [lw["conv_a_w"], lw["conv_b_w"], vec(lw["conv_b_b"]), vec(lw["ln_b_g"]), vec(lw["ln_b_b"]),
             vec(lw["ln_c_g"]), vec(lw["ln_c_b"])]
    spbt = lw["sp_b"].T
    in_specs = ([pl.BlockSpec((rows, 8 * w), lambda b, i: (b * nt + i, 0))]
                + [full2(a) for a in small]
                + [full3(lw["sp_w"]), full2(spbt), full3(lw["pool_w"]), full2(vec(lw["pool_scale"]))])
    out_shapes = (jax.ShapeDtypeStruct((batch * seq, 4 * w), BF16),
                  jax.ShapeDtypeStruct((batch, ka - 1, w), F32),
                  jax.ShapeDtypeStruct((batch, kb - 1, w), F32),
                  jax.ShapeDtypeStruct((batch, nbuf_d, w), F32))
    out_specs = (pl.BlockSpec((rows, 4 * w), lambda b, i: (b * nt + i, 0)),
                 pl.BlockSpec((None, ka - 1, w), lambda b, i: (b, 0, 0)),
                 pl.BlockSpec((None, kb - 1, w), lambda b, i: (b, 0, 0)),
                 pl.BlockSpec((None, nbuf_d, w), lambda b, i: (b, 0, 0)))
    scratch = [pltpu.VMEM((HIST + rows, w), F32), pltpu.VMEM((HIST + rows, w), F32),
               pltpu.VMEM((HIST + rows, w), F32), pltpu.VMEM((rows, w), BF16),
               pltpu.VMEM((cg, chunk, chunk), BF16),
               pltpu.VMEM((SUBLANES - 1, HIST + rows, w), F32)]
    return pl.pallas_call(
        functools.partial(_mix_prompt_body, rows=rows, width=w, ka=ka, kb=kb, nbuf_d=nbuf_d),
        grid=(batch, nt),
        in_specs=in_specs,
        out_specs=out_specs,
        out_shape=out_shapes,
        scratch_shapes=scratch,
        compiler_params=_cparams(("arbitrary", "arbitrary")),
        name="mix_prompt",
    )(z, *small, lw["sp_w"], spbt, lw["pool_w"], vec(lw["pool_scale"]))


def _mix_sample_body(spw_ref, spb_ref, z_ref, sa_ref, sb_ref, sd_ref, caw_ref, cbw_ref, cbb_ref,
                     lbg_ref, lbb_ref, lcg_ref, lcb_ref, pw_ref, ps_ref,
                     y_ref, na_ref, nb_ref, nd_ref, cv_ref, *, nseq, steps, width, ka, kb, nbuf_d, cg):
    w = width
    gw = w // len(POOL_WINDOWS)
    cgw = w // cg
    lane = lax.broadcasted_iota(I32, (1, w), 1)

    def col(t, k):
        return z_ref[t * nseq:(t + 1) * nseq, k * w:(k + 1) * w].astype(F32)

    def put(t, k, val):
        y_ref[t * nseq:(t + 1) * nseq, k * w:(k + 1) * w] = val.astype(y_ref.dtype)

    ext = [sa_ref[k] for k in range(ka - 1)] + [col(t, 0) * col(t, 1) for t in range(steps)]
    for t in range(steps):
        conv = None
        for j in range(ka):
            term = caw_ref[j:j + 1, :] * ext[t + j]
            conv = term if conv is None else conv + term
        put(t, 0, col(t, 2) * conv)
    for k in range(ka - 1):
        na_ref[k] = ext[steps + k]

    for t in range(steps):
        nb_ref[kb - 1 - steps + t] = col(t, 3) * jax.nn.sigmoid(col(t, 4))
    for k in range(kb - 1 - steps):
        nb_ref[k] = sb_ref[k + steps]

    def ext_b(k):
        return sb_ref[k] if k < kb - 1 else nb_ref[k - steps]

    for t in range(steps):
        acc = None
        for j in range(kb):
            term = cbw_ref[j:j + 1, :] * ext_b(t + j)
            acc = term if acc is None else acc + term
        put(t, 1, _silu(_layernorm(acc + cbb_ref[...], lbg_ref[...], lbb_ref[...])))

    def lane_groups(vals):
        out = jnp.full((1, w), vals[cg - 1], F32)
        for g in range(cg - 2, -1, -1):
            out = jnp.where(lane < (g + 1) * cgw, vals[g], out)
        return out

    for t in range(steps):
        cv_ref[t] = _layernorm(col(t, 6), lcg_ref[...], lcb_ref[...])
    for t in range(steps):
        s = lane_groups([spb_ref[g, t] for g in range(cg)])
        for sp in range(t + 1):
            s = s + lane_groups([spw_ref[g, t, sp] for g in range(cg)]) * cv_ref[sp]
        put(t, 2, col(t, 5) * s)

    for t in range(steps):
        nd_ref[nbuf_d - steps + t] = col(t, 7)
    for k in range(nbuf_d - steps):
        nd_ref[k] = sd_ref[k + steps]

    def ext_d(k):
        return sd_ref[k] if k < nbuf_d else nd_ref[k - steps]

    for t in range(steps):
        for gi, win in enumerate(POOL_WINDOWS):
            lanes = slice(gi * gw, (gi + 1) * gw)
            tot = None
            for k in range(win):
                term = ext_d(nbuf_d + t - k)[:, lanes]
                tot = term if tot is None else tot + term
            cnt = float(min(PAST_LEN + t + 1, win))
            d = tot / cnt - ext_d(nbuf_d + t)[:, lanes]
            yd = jnp.dot(d.astype(BF16), pw_ref[gi].astype(BF16), preferred_element_type=F32)
            y_ref[t * nseq:(t + 1) * nseq, 3 * w + gi * gw:3 * w + (gi + 1) * gw] = (
                yd * ps_ref[:, lanes]).astype(y_ref.dtype)


def _mix_sample(z, sa, sb, sd, lw, *, nseq, steps, m_prompt):
    w = lw["conv_a_w"].shape[1]
    ka = lw["conv_a_w"].shape[0]
    kb = lw["conv_b_w"].shape[0]
    nbuf_d = max(POOL_WINDOWS) - 1
    cg = lw["sp_w"].shape[0]
    ms = nseq * steps
    assert m_prompt % ms == 0 and steps <= nbuf_d and steps <= kb - 1
    rb = m_prompt // ms
    spw_small = lw["sp_w"][:, :steps, :steps]
    spb_small = lw["sp_b"][:, :steps]
    vec = lambda a: a.reshape(1, w)
    smem = pl.BlockSpec(memory_space=pltpu.SMEM)
    full2 = lambda a: pl.BlockSpec(a.shape, lambda i: (0, 0))
    full3 = lambda a: pl.BlockSpec(a.shape, lambda i: (0, 0, 0))
    small = [lw["conv_a_w"], lw["conv_b_w"], vec(lw["conv_b_b"]), vec(lw["ln_b_g"]), vec(lw["ln_b_b"]),
             vec(lw["ln_c_g"]), vec(lw["ln_c_b"])]
    in_specs = ([smem, smem,
                 pl.BlockSpec((ms, 8 * w), lambda i: (rb, 0)),
                 full3(sa), full3(sb), full3(sd)]
                + [full2(a) for a in small]
                + [full3(lw["pool_w"]), full2(vec(lw["pool_scale"]))])
    out_shapes = (jax.ShapeDtypeStruct((ms, 4 * w), BF16),
                  jax.ShapeDtypeStruct(sa.shape, F32), jax.ShapeDtypeStruct(sb.shape, F32),
                  jax.ShapeDtypeStruct(sd.shape, F32), jax.ShapeDtypeStruct((steps, nseq, w), F32))
    out_specs = (pl.BlockSpec((ms, 4 * w), lambda i: (0, 0)),
                 full3(sa), full3(sb), full3(sd),
                 pl.BlockSpec((steps, nseq, w), lambda i: (0, 0, 0)))
    return pl.pallas_call(
        functools.partial(_mix_sample_body, nseq=nseq, steps=steps, width=w, ka=ka, kb=kb,
                          nbuf_d=nbuf_d, cg=cg),
        grid=(1,),
        in_specs=in_specs,
        out_specs=out_specs,
        out_shape=out_shapes,
        compiler_params=_cparams(("arbitrary",)),
        name="mix_sample",
    )(spw_small, spb_small, z, sa, sb, sd, *small, lw["pool_w"], vec(lw["pool_scale"]))


def _merge_out_body(yp_ref, ys_ref, g0_ref, g1_ref, g2_ref, g3_ref, x_ref, gn_ref, wa_hbm, wb_hbm, wc_hbm, wd_hbm,
                    wo_hbm, *refs, width, prompt_tiles, layer, with_h):
    n_out = 2 if with_h else 1
    x1_ref = refs[0]
    wbr_ref, wout_ref, stage_ref, sem_ref = refs[n_out:n_out + 4]
    i = pl.program_id(0)
    rows = stage_ref.shape[1]
    d = wout_ref.shape[0]

    @pl.when(i == 0)
    def _load_weights():
        chunks = []
        for k, w_hbm in enumerate((wa_hbm, wb_hbm, wc_hbm, wd_hbm)):
            for r in range(0, width, rows):
                chunks.append((w_hbm.at[layer, pl.ds(r, rows), :], wbr_ref.at[k, pl.ds(r, rows), :]))
        for r in range(0, d, rows):
            chunks.append((wo_hbm.at[layer, pl.ds(r, rows), :], wout_ref.at[pl.ds(r, rows), :]))

        def fetch(c):
            return pltpu.make_async_copy(chunks[c][0], stage_ref.at[c % 2], sem_ref.at[c % 2])

        fetch(0).start(priority=0)
        for c in range(len(chunks)):
            if c + 1 < len(chunks):
                fetch(c + 1).start(priority=(c + 1) % 2)
            fetch(c).wait()
            chunks[c][1][...] = stage_ref[c % 2].astype(BF16)

    def run(y_ref):
        acc = None
        for k, g_ref in enumerate((g0_ref, g1_ref, g2_ref, g3_ref)):
            br = jnp.dot(y_ref[:, k * width:(k + 1) * width], wbr_ref[k], preferred_element_type=F32)
            term = jax.nn.sigmoid(g_ref[...].astype(F32)) * br
            acc = term if acc is None else acc + term
        x1 = x_ref[...] + jnp.dot(acc.astype(BF16), wout_ref[...], preferred_element_type=F32)
        x1_ref[...] = x1
        if with_h:
            refs[1][...] = _rms(x1, gn_ref[...]).astype(BF16)

    @pl.when(i < prompt_tiles)
    def _prompt():
        run(yp_ref)

    @pl.when(i >= prompt_tiles)
    def _sample():
        run(ys_ref)


def _merge_out(y_p, y_s, z, x, w_brs, w_out, layer, g, *, tm, with_h):
    mp, yw = y_p.shape
    ms = y_s.shape[0]
    assert mp % tm == 0 and ms % tm == 0
    tp = mp // tm
    w = yw // 4
    d = w_out.shape[2]
    assert (2 * yw) % d == 0 and w % STAGE_ROWS == 0 and d % STAGE_ROWS == 0
    goff = (2 * yw) // d
    gate_spec = lambda k: pl.BlockSpec((tm, d), lambda i, k=k: (i, goff + k))
    row = pl.BlockSpec((tm, d), lambda i: (i, 0))
    hbm = pl.BlockSpec(memory_space=pl.ANY)
    out_shape = [jax.ShapeDtypeStruct((mp + ms, d), F32)]
    out_specs = [row]
    if with_h:
        out_shape.append(jax.ShapeDtypeStruct((mp + ms, d), BF16))
        out_specs.append(row)
    return pl.pallas_call(
        functools.partial(_merge_out_body, width=w, prompt_tiles=tp, layer=layer, with_h=with_h),
        grid=((mp + ms) // tm,),
        in_specs=[pl.BlockSpec((tm, yw), lambda i: (jnp.minimum(i, tp - 1), 0)),
                  pl.BlockSpec((tm, yw), lambda i: (jnp.maximum(i - tp, 0), 0))]
                 + [gate_spec(k) for k in range(4)]
                 + [row, pl.BlockSpec((1, d), lambda i: (0, 0))] + [hbm] * 5,
        out_specs=out_specs,
        out_shape=out_shape,
        scratch_shapes=[pltpu.VMEM((4, w, d), BF16), pltpu.VMEM((d, d), BF16),
                        pltpu.VMEM((2, STAGE_ROWS, d), F32), pltpu.SemaphoreType.DMA((2,))],
        compiler_params=_cparams(("arbitrary",)),
        name="merge_out",
    )(y_p, y_s, z, z, z, z, x, g.reshape(1, d), *w_brs, w_out)


def _route_body(x_ref, g_ref, rwt_ref, e_ref, r_ref, wt_ref, cnt_ref, carry_ref, tri_ref):
    i = pl.program_id(0)
    ne = rwt_ref.shape[0]
    tm = x_ref.shape[0]

    @pl.when(i == 0)
    def _init():
        carry_ref[...] = jnp.zeros_like(carry_ref)
        upper = lax.broadcasted_iota(I32, (tm, tm), 0) <= lax.broadcasted_iota(I32, (tm, tm), 1)
        tri_ref[...] = jnp.where(upper, 1.0, 0.0).astype(BF16)

    h = _rms(x_ref[...], g_ref[...])
    h_hi = h.astype(BF16)
    h_lo = (h - h_hi.astype(F32)).astype(BF16)
    rw = rwt_ref[...]
    rw_hi = rw.astype(BF16)
    rw_lo = (rw - rw_hi.astype(F32)).astype(BF16)
    dn = (((1,), (1,)), ((), ()))
    dg = lambda a, b: lax.dot_general(a, b, dn, preferred_element_type=F32)
    logits = dg(rw_hi, h_hi) + dg(rw_hi, h_lo) + dg(rw_lo, h_hi)
    ex = jnp.exp(logits - jnp.max(logits, axis=0, keepdims=True))
    p = ex / jnp.sum(ex, axis=0, keepdims=True)
    eid = lax.broadcasted_iota(I32, (ne, tm), 0)
    m1 = jnp.max(p, axis=0, keepdims=True)
    i1 = jnp.min(jnp.where(p == m1, eid, ne), axis=0, keepdims=True)
    oh1 = eid == i1
    p2 = jnp.where(oh1, -1.0, p)
    m2 = jnp.max(p2, axis=0, keepdims=True)
    i2 = jnp.min(jnp.where(p2 == m2, eid, ne), axis=0, keepdims=True)
    oh2 = eid == i2
    den = m1 + m2
    sel = jnp.logical_or(oh1, oh2)
    sel_f = jnp.where(sel, 1.0, 0.0)
    cum = jnp.dot(sel_f.astype(BF16), tri_ref[...], preferred_element_type=F32) + carry_ref[...]
    rank1 = jnp.sum(jnp.where(oh1, cum, 0.0), axis=0, keepdims=True) - 1.0
    rank2 = jnp.sum(jnp.where(oh2, cum, 0.0), axis=0, keepdims=True) - 1.0
    carry_ref[...] = carry_ref[...] + jnp.sum(sel_f, axis=1, keepdims=True)
    e_ref[0:1, :] = i1
    e_ref[1:2, :] = i2
    r_ref[0:1, :] = rank1.astype(I32)
    r_ref[1:2, :] = rank2.astype(I32)
    wt_ref[0:1, :] = m1 / den
    wt_ref[1:2, :] = m2 / den

    @pl.when(i == pl.num_programs(0) - 1)
    def _counts():
        cnt_ref[...] = jnp.broadcast_to(carry_ref[...], cnt_ref.shape).astype(I32)


def _route(x1, g, router_w, *, tm):
    m, d = x1.shape
    ne = router_w.shape[1]
    tok = lambda dt: jax.ShapeDtypeStruct((TOP_K, m), dt)
    tok_spec = pl.BlockSpec((TOP_K, tm), lambda i: (0, i))
    return pl.pallas_call(
        _route_body,
        grid=(m // tm,),
        in_specs=[pl.BlockSpec((tm, d), lambda i: (i, 0)),
                  pl.BlockSpec((1, d), lambda i: (0, 0)),
                  pl.BlockSpec((ne, d), lambda i: (0, 0))],
        out_specs=(tok_spec, tok_spec, tok_spec, pl.BlockSpec((ne, LANES), lambda i: (0, 0))),
        out_shape=(tok(I32), tok(I32), tok(F32), jax.ShapeDtypeStruct((ne, LANES), I32)),
        scratch_shapes=[pltpu.VMEM((ne, 1), F32), pltpu.VMEM((tm, tm), BF16)],
        compiler_params=_cparams(("arbitrary",)),
        name="route",
    )(x1, g.reshape(1, d), router_w.T)


def _plan_body(e_ref, r_ref, cnt_ref, pos_ref, tgrp_ref, tfirst_ref, misc_ref, *, tile_shift):
    ne = cnt_ref.shape[0]
    tile = 1 << tile_shift
    sub = lax.broadcasted_iota(I32, (ne, LANES), 0)
    lane = lax.broadcasted_iota(I32, (ne, LANES), 1)

    def as_row(col):
        return jnp.sum(jnp.where(sub == lane, col, 0), axis=0, keepdims=True)

    cnt = cnt_ref[:, 0:1]
    padded = lax.shift_left(lax.shift_right_logical(cnt + (tile - 1), tile_shift), tile_shift)
    gstart = jnp.sum(jnp.where(lane < sub, as_row(padded), 0), axis=1, keepdims=True)
    gend = gstart + padded
    m = e_ref.shape[1]
    eid = lax.broadcasted_iota(I32, (ne, m), 0)
    for k in range(TOP_K):
        base = jnp.sum(jnp.where(eid == e_ref[k:k + 1, :], gstart, 0), axis=0, keepdims=True)
        pos_ref[k:k + 1, :] = base + r_ref[k:k + 1, :]
    start = lax.broadcasted_iota(I32, (1, LANES), 1) * tile
    grp = jnp.sum((gend <= start).astype(I32), axis=0, keepdims=True)
    tgrp_ref[...] = jnp.minimum(grp, ne - 1)
    tfirst_ref[...] = jnp.sum(jnp.logical_and(gstart == start, padded > 0).astype(I32), axis=0, keepdims=True)
    n_tiles = lax.shift_right_logical(jnp.max(gend, axis=0, keepdims=True), tile_shift)
    misc = jnp.where(sub == 0, n_tiles, 0)
    misc = jnp.where(sub == 1, as_row(gstart + cnt), misc)
    misc = jnp.where(sub == 2, as_row(gend), misc)
    lane_f = lane.astype(F32)
    nonempty = jnp.logical_and(as_row(padded) > 0, lane < ne)
    later = jnp.min(jnp.where(jnp.logical_and(nonempty, lane > sub), lane_f, float(ne)), axis=1, keepdims=True)
    firstg = jnp.min(jnp.where(nonempty, lane_f, float(ne)), axis=1, keepdims=True)
    nxt = jnp.where(later < ne, later, firstg).astype(I32)
    wraps = (later >= ne).astype(I32)
    mine = sub == jnp.minimum(grp, ne - 1)
    misc = jnp.where(sub == 3, jnp.sum(jnp.where(mine, nxt, 0), axis=0, keepdims=True), misc)
    misc = jnp.where(sub == 4, jnp.sum(jnp.where(mine, wraps, 0), axis=0, keepdims=True), misc)
    misc_ref[...] = misc


def _plan(e, r, cnt, *, tile):
    tile_shift = tile.bit_length() - 1
    assert 1 << tile_shift == tile
    ne = cnt.shape[0]
    m = e.shape[1]
    whole = lambda a: pl.BlockSpec(a.shape, lambda i: (0, 0))
    row = jax.ShapeDtypeStruct((1, LANES), I32)
    out_shape = (jax.ShapeDtypeStruct((TOP_K, m), I32), row, row, jax.ShapeDtypeStruct((ne, LANES), I32))
    return pl.pallas_call(
        functools.partial(_plan_body, tile_shift=tile_shift),
        grid=(1,),
        in_specs=[whole(e), whole(r), whole(cnt)],
        out_specs=tuple(pl.BlockSpec(s.shape, lambda i: (0, 0)) for s in out_shape),
        out_shape=out_shape,
        compiler_params=_cparams(("arbitrary",)),
        name="plan",
    )(e, r, cnt)


def _dispatch_body(pos0_ref, pos1_ref, nv_ref, fill_ref, end_ref, x_hbm, g_ref, o_ref, src_ref, buf_ref, sem_ref):
    j = pl.program_id(0)
    tm = o_ref.shape[0]
    n_tok = pos0_ref.shape[0]
    nv = nv_ref[0]

    def start_tile(tile, slot):
        def body(pair, carry):
            for p in range(2):
                r = pair * 2 + p
                tok = src_ref[tile * tm + r]
                pltpu.make_async_copy(x_hbm.at[pl.ds(tok, 1), :], buf_ref.at[slot, pl.ds(r, 1), :],
                                      sem_ref.at[slot]).start(priority=p)
            return carry
        lax.fori_loop(0, tm // 2, body, 0, unroll=ISSUE_UNROLL // 2)

    @pl.when(j == 0)
    def _prologue():
        for g in range(fill_ref.shape[0]):
            def fill(p, carry):
                src_ref[p] = 0
                return carry
            lax.fori_loop(fill_ref[g], end_ref[g], fill, 0)

        def inv(t, carry):
            src_ref[pos0_ref[t]] = t
            src_ref[pos1_ref[t]] = t
            return carry
        lax.fori_loop(0, n_tok, inv, 0, unroll=ISSUE_UNROLL)
        start_tile(0, 0)

    slot = lax.rem(j, 2)

    @pl.when(j < nv)
    def _work():
        @pl.when(j + 1 < nv)
        def _prefetch():
            start_tile(j + 1, 1 - slot)

        pltpu.make_async_copy(x_hbm.at[pl.ds(0, tm), :], buf_ref.at[slot], sem_ref.at[slot]).wait()
        o_ref[...] = _rms(buf_ref[slot], g_ref[...]).astype(o_ref.dtype)

    @pl.when(j >= nv)
    def _skip():
        o_ref[...] = jnp.zeros_like(o_ref)


def _dispatch(x1, g, pos0, pos1, n_tiles, fill_row, end_row, *, tm, max_tiles):
    m, d = x1.shape
    grid_spec = pltpu.PrefetchScalarGridSpec(
        num_scalar_prefetch=5,
        grid=(max_tiles,),
        in_specs=[pl.BlockSpec(memory_space=pl.ANY),
                  pl.BlockSpec((1, d), lambda j, *_: (0, 0))],
        out_specs=pl.BlockSpec((tm, d), lambda j, *_: (j, 0)),
        scratch_shapes=[pltpu.SMEM((max_tiles * tm,), I32),
                        pltpu.VMEM((2, tm, d), F32),
                        pltpu.SemaphoreType.DMA((2,))],
    )
    return pl.pallas_call(
        _dispatch_body,
        grid_spec=grid_spec,
        out_shape=jax.ShapeDtypeStruct((max_tiles * tm, d), BF16),
        compiler_params=_cparams(("arbitrary",)),
        name="dispatch",
    )(pos0, pos1, n_tiles, fill_row, end_row, x1, g.reshape(1, d))


def _combine_body(pos0_ref, pos1_ref, ys_hbm, x_ref, wt_ref, g_ref, xn_ref, h_ref, buf_ref, sem_ref):
    i = pl.program_id(0)
    n = pl.num_programs(0)
    tm = x_ref.shape[0]

    def start_tile(tile, slot):
        def body(r, carry):
            t = tile * tm + r
            pltpu.make_async_copy(ys_hbm.at[pl.ds(pos0_ref[t], 1), :], buf_ref.at[slot, 0, pl.ds(r, 1), :],
                                  sem_ref.at[slot, 0]).start(priority=0)
            pltpu.make_async_copy(ys_hbm.at[pl.ds(pos1_ref[t], 1), :], buf_ref.at[slot, 1, pl.ds(r, 1), :],
                                  sem_ref.at[slot, 1]).start(priority=1)
            return carry
        lax.fori_loop(0, tm, body, 0, unroll=ISSUE_UNROLL)

    @pl.when(i == 0)
    def _prologue():
        start_tile(0, 0)

    slot = lax.rem(i, 2)

    @pl.when(i + 1 < n)
    def _prefetch():
        start_tile(i + 1, 1 - slot)

    for k in range(TOP_K):
        pltpu.make_async_copy(ys_hbm.at[pl.ds(0, tm), :], buf_ref.at[slot, k], sem_ref.at[slot, k]).wait()
    xn = x_ref[...] + wt_ref[:, 0:1] * buf_ref[slot, 0] + wt_ref[:, 1:2] * buf_ref[slot, 1]
    xn_ref[...] = xn
    h_ref[...] = _rms(xn, g_ref[...]).astype(h_ref.dtype)


def _combine(ys, x1, wt_cols, g, pos0, pos1, *, tm, h_dtype):
    m, d = x1.shape
    row = lambda: pl.BlockSpec((tm, d), lambda i, *_: (i, 0))
    grid_spec = pltpu.PrefetchScalarGridSpec(
        num_scalar_prefetch=2,
        grid=(m // tm,),
        in_specs=[pl.BlockSpec(memory_space=pl.ANY), row(),
                  pl.BlockSpec((tm, TOP_K), lambda i, *_: (i, 0)),
                  pl.BlockSpec((1, d), lambda i, *_: (0, 0))],
        out_specs=(row(), row()),
        scratch_shapes=[pltpu.VMEM((2, TOP_K, tm, d), F32), pltpu.SemaphoreType.DMA((2, TOP_K))],
    )
    return pl.pallas_call(
        _combine_body,
        grid_spec=grid_spec,
        out_shape=(jax.ShapeDtypeStruct((m, d), F32), jax.ShapeDtypeStruct((m, d), h_dtype)),
        compiler_params=_cparams(("arbitrary",)),
        name="combine",
    )(pos0, pos1, ys, x1, wt_cols, g.reshape(1, d))


def _pick(total, prefs):
    for p in prefs:
        if total % p == 0:
            return p
    raise ValueError(f"no tile for {total} in {prefs}")


def _row_tile(m, cap):
    best = None
    for t in range(16, cap + 1, 16):
        if m % t == 0:
            best = t
    assert best is not None
    return best


def kernel(x_prompt, x_sample, state_conv_a, state_conv_b, state_pool, norm_mix_g, w_in, conv_a_w, w_br_a,
           conv_b_w, conv_b_b, ln_b_g, ln_b_b, w_br_b, ln_c_g, ln_c_b, sp_w, sp_b, w_br_c, pool_w,
           pool_scale, w_br_d, w_out, norm_ffn_g, ffn_w1, ffn_w3, ffn_w2, router_w, moe_w1, moe_w3, moe_w2,
           final_norm_g):
    batch, seq, d = x_prompt.shape
    nseq, steps, _ = x_sample.shape
    depth = w_in.shape[0]
    n_exp = router_w.shape[2]
    mp, ms = batch * seq, nseq * steps
    m = mp + ms

    x = jnp.concatenate([x_prompt.reshape(mp, d), jnp.swapaxes(x_sample, 0, 1).reshape(ms, d)], axis=0)
    st_a = jnp.swapaxes(state_conv_a, 1, 2)
    st_b = jnp.swapaxes(state_conv_b, 1, 2)
    st_d = jnp.swapaxes(state_pool, 1, 2)

    tm_big = _row_tile(m, 1088)
    tm_mid = _row_tile(m, 544)
    tm_small = _row_tile(m, 256)
    tm_merge = min(_row_tile(mp, 256), _row_tile(ms, 256))
    assert mp % tm_merge == 0 and ms % tm_merge == 0
    tm_moe = 512
    max_tiles = -(-(m * TOP_K) // tm_moe) + n_exp
    assert max_tiles <= LANES
    mix_rows = _pick(seq, (256, 128))
    moe_w1f = moe_w1.reshape((-1,) + moe_w1.shape[2:])
    moe_w3f = moe_w3.reshape((-1,) + moe_w3.shape[2:])
    moe_w2f = moe_w2.reshape((-1,) + moe_w2.shape[2:])
    dffe = moe_w1.shape[3]
    dff = ffn_w1.shape[2]

    h = _rmsnorm(x, norm_mix_g[0], BF16, tm_mid)
    outs_p = ([], [], [])
    outs_s = ([], [], [], [])
    y_final = None
    for i in range(depth):
        lw = dict(conv_a_w=conv_a_w[i], conv_b_w=conv_b_w[i], conv_b_b=conv_b_b[i], ln_b_g=ln_b_g[i],
                  ln_b_b=ln_b_b[i], ln_c_g=ln_c_g[i], ln_c_b=ln_c_b[i], sp_w=sp_w[i], sp_b=sp_b[i],
                  pool_w=pool_w[i], pool_scale=pool_scale[i])
        z = _dense_stream_matmul(h, w_in, i, tm=tm_big, tn=_pick(w_in.shape[2], (1024, 512, 256, 128)),
                                 mode="plain", out_dtype=BF16, name="inproj")
        y_p, na_p, nb_p, nd_p = _mix_prompt(z, lw, batch=batch, seq=seq, rows=mix_rows)
        y_s, na_s, nb_s, nd_s, cv_s = _mix_sample(z, st_a[i], st_b[i], st_d[i], lw, nseq=nseq, steps=steps,
                                                  m_prompt=mp)
        for acc, val in zip(outs_p, (na_p, nb_p, nd_p)):
            acc.append(val)
        for acc, val in zip(outs_s, (na_s, nb_s, nd_s, cv_s)):
            acc.append(val)
        merge_out = functools.partial(_merge_out, y_p, y_s, z, x, (w_br_a, w_br_b, w_br_c, w_br_d), w_out, i,
                                      norm_ffn_g[i], tm=tm_merge)
        last = i == depth - 1
        g_next = final_norm_g if last else norm_mix_g[i + 1]
        h_dtype = F32 if last else BF16
        j = i // 2
        if i % 2 == 0:
            x1, h2 = merge_out(with_h=True)
            a = _dense_stream_matmul(h2, ffn_w1, j, w2=ffn_w3, tm=tm_big, tn=_pick(dff, (512, 256, 128)),
                                     mode="gated", out_dtype=BF16, name="ffn_up")
            x = _dense_stream_matmul(a, ffn_w2, j, tm=tm_big, tn=_pick(d, (512, 256, 128)), mode="resid",
                                     out_dtype=F32, res=x1, name="ffn_down")
            h = _rmsnorm(x, g_next, h_dtype, tm_mid)
        else:
            (x1,) = merge_out(with_h=False)
            e, r, wt, cnt = _route(x1, norm_ffn_g[i], router_w[j], tm=_pick(m, (512, 256, 128)))
            pos, tgrp, tfirst, misc = _plan(e, r, cnt, tile=tm_moe)
            n_tiles = misc[0, 0:1]
            hs = _dispatch(x1, norm_ffn_g[i], pos[0], pos[1], n_tiles, misc[1, :n_exp], misc[2, :n_exp],
                           tm=tm_moe, max_tiles=max_tiles)
            grp = tgrp[0, :max_tiles] + j * n_exp
            first = tfirst[0, :max_tiles]
            nextg = misc[3, :max_tiles] + j * n_exp
            wrap = misc[4, :max_tiles]
            a = _stream_matmul(hs, [moe_w1f, moe_w3f], grp, first, nextg, wrap, n_tiles, tm=tm_moe,
                               tn=_pick(dffe, (1408, 1024, 512, 256, 128)), mode="gated", out_dtype=BF16,
                               name="moe_up", step_rows=_stream_step_rows(d, MOE_STREAM_STEPS))
            ys = _stream_matmul(a, [moe_w2f], grp, first, nextg, wrap, n_tiles, tm=tm_moe,
                                tn=_pick(d, (1024, 512, 256, 128)), mode="plain", out_dtype=F32, name="moe_down",
                                step_rows=_stream_step_rows(dffe, MOE_STREAM_STEPS))
            x, h = _combine(ys, x1, wt.T, g_next, pos[0], pos[1], tm=tm_small, h_dtype=h_dtype)
        if last:
            y_final = h

    y_prompt = y_final[:mp].reshape(batch, seq, d)
    y_sample = jnp.swapaxes(y_final[mp:].reshape(steps, nseq, d), 0, 1)
    stack_s = lambda vals: jnp.swapaxes(jnp.stack(vals), 1, 2)
    return (y_prompt, y_sample, jnp.stack(outs_p[0]), jnp.stack(outs_p[1]), jnp.stack(outs_p[2]),
            stack_s(outs_s[0]), stack_s(outs_s[1]), stack_s(outs_s[2]), stack_s(outs_s[3]))
```

```python
import functools

import jax
import jax.numpy as jnp
from jax import lax
from jax.experimental import pallas as pl
from jax.experimental.pallas import tpu as pltpu

F32 = jnp.float32
BF16 = jnp.bfloat16
I32 = jnp.int32

EPS = 1e-6
PAST_LEN = 16384
POOL_WINDOWS = (2, 4, 8, 16)
TOP_K = 2
LANES = 128
SUBLANES = 8
HIST = 32
ISSUE_UNROLL = 8
VMEM_LIMIT = 56 * 1024 * 1024


def _cparams(sem, vmem=VMEM_LIMIT):
    return pltpu.CompilerParams(dimension_semantics=sem, vmem_limit_bytes=vmem)


def _rms(x, g):
    return x * lax.rsqrt(jnp.mean(x * x, axis=-1, keepdims=True) + EPS) * g


def _layernorm(x, g, b):
    mu = jnp.mean(x, axis=-1, keepdims=True)
    xc = x - mu
    var = jnp.mean(xc * xc, axis=-1, keepdims=True)
    return xc * lax.rsqrt(var + EPS) * g + b


def _silu(x):
    return x * jax.nn.sigmoid(x)


def _rmsnorm_body(x_ref, g_ref, o_ref):
    o_ref[...] = _rms(x_ref[...], g_ref[...]).astype(o_ref.dtype)


def _rmsnorm(x, g, out_dtype, tm):
    m, d = x.shape
    return pl.pallas_call(
        _rmsnorm_body,
        grid=(m // tm,),
        in_specs=[pl.BlockSpec((tm, d), lambda i: (i, 0)),
                  pl.BlockSpec((1, d), lambda i: (0, 0))],
        out_specs=pl.BlockSpec((tm, d), lambda i: (i, 0)),
        out_shape=jax.ShapeDtypeStruct((m, d), out_dtype),
        compiler_params=_cparams(("arbitrary",)),
        name="rmsnorm",
    )(x, g.reshape(1, d))


STAGE_ROWS = 256
MOE_STREAM_STEPS = 4
STREAM_PIECES = 2


_SLOT, _ISSUED, _DONE, _NEXT_GRP, _NEXT_COL, _HAS_NEXT = range(6)


def _stream_body(grp_ref, first_ref, nextg_ref, wrap_ref, nv_ref, x_ref, *refs, n_w, mode, step_rows):
    res_ref = refs[0] if mode == "resid" else None
    refs = refs[1:] if mode == "resid" else refs
    w_hbm = refs[:n_w]
    o_ref, wb_ref, stage_ref, st_ref, sem_ref = refs[n_w:n_w + 5]
    n = pl.program_id(0)
    m = pl.program_id(1)
    k, tn = wb_ref.shape[2], wb_ref.shape[3]
    n_steps = k // step_rows
    part = step_rows // STREAM_PIECES
    valid = m < nv_ref[0]

    def pieces(step):
        row0 = pl.multiple_of(step * step_rows, 2 * SUBLANES)
        col = pl.multiple_of(st_ref[_NEXT_COL] * tn, LANES)
        out = []
        for j in range(n_w):
            for p in range(STREAM_PIECES):
                src = w_hbm[j].at[st_ref[_NEXT_GRP], pl.ds(row0 + p * part, part), pl.ds(col, tn)]
                dst = stage_ref.at[lax.rem(step, 2), j, pl.ds(p * part, part), :]
                out.append(pltpu.make_async_copy(src, dst, sem_ref.at[0]))
        return out

    def start(step):
        for c in pieces(step):
            c.start(priority=1)

    def wait(step):
        for c in pieces(step):
            c.wait()

    def cast(slot, step):
        row0 = pl.multiple_of(step * step_rows, 2 * SUBLANES)
        for j in range(n_w):
            wb_ref[slot, j, pl.ds(row0, step_rows), :] = stage_ref[lax.rem(step, 2), j].astype(BF16)

    def compute(slot):
        x = x_ref[...]
        if mode == "gated":
            a = jnp.dot(x, wb_ref[slot, 0], preferred_element_type=F32)
            b = jnp.dot(x, wb_ref[slot, 1], preferred_element_type=F32)
            o_ref[...] = (_silu(a) * b).astype(o_ref.dtype)
        else:
            acc = jnp.dot(x, wb_ref[slot, 0], preferred_element_type=F32)
            if mode == "resid":
                acc = res_ref[...] + acc
            o_ref[...] = acc.astype(o_ref.dtype)

    @pl.when(jnp.logical_and(valid, first_ref[m] == 1))
    def _switch():
        @pl.when(jnp.logical_and(n == 0, m == 0))
        def _boot():
            st_ref[_SLOT] = 1
            st_ref[_ISSUED] = 0
            st_ref[_DONE] = 0
            st_ref[_NEXT_GRP] = grp_ref[0]
            st_ref[_NEXT_COL] = 0

        def finish(step, carry):
            @pl.when(st_ref[_ISSUED] <= step)
            def _fetch():
                start(step)
                wait(step)
            cast(1 - st_ref[_SLOT], step)
            return carry
        lax.fori_loop(st_ref[_DONE], n_steps, finish, 0)

        st_ref[_SLOT] = 1 - st_ref[_SLOT]
        st_ref[_ISSUED] = 0
        st_ref[_DONE] = 0
        st_ref[_NEXT_GRP] = nextg_ref[m]
        st_ref[_NEXT_COL] = n + wrap_ref[m]
        last_col = n == pl.num_programs(0) - 1
        st_ref[_HAS_NEXT] = jnp.where(jnp.logical_and(wrap_ref[m] == 1, last_col), 0, 1)

    slot = st_ref[_SLOT]
    issued = st_ref[_ISSUED]
    done = st_ref[_DONE]
    in_flight = done < issued
    more = jnp.logical_and(st_ref[_HAS_NEXT] == 1, issued < n_steps)

    for do_cast in (False, True):
        for do_start in (False, True):
            @pl.when(jnp.logical_and(valid, jnp.logical_and(in_flight == do_cast, more == do_start)))
            def _step(do_cast=do_cast, do_start=do_start):
                if do_start:
                    start(issued)
                if do_cast:
                    cast(1 - slot, done)
                    st_ref[_DONE] = done + 1
                compute(slot)
                if do_start:
                    wait(issued)
                    st_ref[_ISSUED] = issued + 1

    @pl.when(jnp.logical_not(valid))
    def _zero():
        o_ref[...] = jnp.zeros_like(o_ref)


def _stream_matmul(x, ws, tile_grp, tile_first, tile_next, tile_wrap, n_valid, *, tm, tn, mode, out_dtype, name,
                   step_rows, res=None):
    m, k = x.shape
    n = ws[0].shape[2]
    assert m % tm == 0 and n % tn == 0 and tn % LANES == 0
    assert k % step_rows == 0 and step_rows % (STREAM_PIECES * SUBLANES) == 0
    n_w = len(ws)
    row_of = lambda mi, nv: jnp.minimum(mi, nv[0] - 1)
    extra_specs, extra_args = [], []
    if mode == "resid":
        extra_specs = [pl.BlockSpec((tm, tn), lambda ni, mi, g, f, nx, wr, nv: (row_of(mi, nv), ni))]
        extra_args = [res]
    grid_spec = pltpu.PrefetchScalarGridSpec(
        num_scalar_prefetch=5,
        grid=(n // tn, m // tm),
        in_specs=[pl.BlockSpec((tm, k), lambda ni, mi, g, f, nx, wr, nv: (row_of(mi, nv), 0))]
                 + extra_specs + [pl.BlockSpec(memory_space=pl.ANY)] * n_w,
        out_specs=pl.BlockSpec((tm, tn), lambda ni, mi, g, f, nx, wr, nv: (mi, ni)),
        scratch_shapes=[pltpu.VMEM((2, n_w, k, tn), BF16), pltpu.VMEM((2, n_w, step_rows, tn), F32),
                        pltpu.SMEM((8,), I32), pltpu.SemaphoreType.DMA((1,))],
    )
    return pl.pallas_call(
        functools.partial(_stream_body, n_w=n_w, mode=mode, step_rows=step_rows),
        grid_spec=grid_spec,
        out_shape=jax.ShapeDtypeStruct((m, n), out_dtype),
        compiler_params=_cparams(("arbitrary", "arbitrary"), vmem=60 * 1024 * 1024),
        name=name,
    )(tile_grp, tile_first, tile_next, tile_wrap, n_valid, x, *extra_args, *ws)


def _dense_stream_matmul(x, w, layer, *, tm, tn, mode, out_dtype, res=None, w2=None, name):
    tiles = x.shape[0] // tm
    grp = jnp.full((tiles,), layer, I32)
    first = jnp.zeros((tiles,), I32).at[0].set(1)
    wrap = jnp.ones((tiles,), I32)
    nv = jnp.full((1,), tiles, I32)
    ws = [w] if w2 is None else [w, w2]
    step_rows = _stream_step_rows(w.shape[1], tiles - 1)
    return _stream_matmul(x, ws, grp, first, grp, wrap, nv, tm=tm, tn=tn, mode=mode, out_dtype=out_dtype,
                          res=res, name=name, step_rows=step_rows)


def _stream_step_rows(k, steps_available):
    best = k
    for steps in range(1, max(steps_available, 1) + 1):
        if k % steps == 0 and (k // steps) % (4 * SUBLANES) == 0:
            best = k // steps
    return best


def _mix_prompt_body(z_ref, caw_ref, cbw_ref, cbb_ref, lbg_ref, lbb_ref, lcg_ref, lcb_ref, spw_ref,
                     spbt_ref, pw_ref, ps_ref, y_ref, na_ref, nb_ref, nd_ref,
                     qa_ext, gb_ext, pd_ext, vn_buf, wm_ref, gsh_ref, *, rows, width, ka, kb, nbuf_d):
    i = pl.program_id(1)
    last = pl.num_programs(1) - 1
    w = width
    gw = w // len(POOL_WINDOWS)
    chunk = wm_ref.shape[1]
    cg = wm_ref.shape[0]
    cgw = w // cg
    sub = 32

    @pl.when(i == 0)
    def _init():
        zeros = jnp.zeros((HIST, w), F32)
        qa_ext[0:HIST, :] = zeros
        gb_ext[0:HIST, :] = zeros
        pd_ext[0:HIST, :] = zeros
        tri = lax.broadcasted_iota(I32, (chunk, chunk), 0) >= lax.broadcasted_iota(I32, (chunk, chunk), 1)
        for g in range(cg):
            wm_ref[g] = jnp.where(tri, spw_ref[g], 0.0).astype(BF16)

    def col(k):
        return z_ref[:, k * w:(k + 1) * w].astype(F32)

    qa_ext[HIST:HIST + rows, :] = col(0) * col(1)
    gb_ext[HIST:HIST + rows, :] = col(3) * jax.nn.sigmoid(col(4))
    pd_ext[HIST:HIST + rows, :] = col(7)
    vn_buf[...] = _layernorm(col(6), lcg_ref[...], lcb_ref[...]).astype(BF16)

    conv_a = None
    for j in range(ka):
        term = caw_ref[j:j + 1, :] * qa_ext[HIST - (ka - 1) + j:HIST - (ka - 1) + j + rows, :]
        conv_a = term if conv_a is None else conv_a + term
    y_ref[:, 0:w] = (col(2) * conv_a).astype(y_ref.dtype)

    for f in range(1, SUBLANES):
        gsh_ref[f - 1, SUBLANES:HIST + rows, :] = gb_ext[SUBLANES - f:HIST + rows - f, :]
    for r0 in range(0, rows, sub):
        acc = None
        for j in range(kb):
            back = kb - 1 - j
            f, s = back % SUBLANES, HIST + r0 - (back - back % SUBLANES)
            src = gb_ext[s:s + sub, :] if f == 0 else gsh_ref[f - 1, s:s + sub, :]
            term = cbw_ref[j:j + 1, :] * src
            acc = term if acc is None else acc + term
        yb = _silu(_layernorm(acc + cbb_ref[...], lbg_ref[...], lbb_ref[...]))
        y_ref[r0:r0 + sub, w:2 * w] = yb.astype(y_ref.dtype)

    for c0 in range(0, rows, chunk):
        for g in range(cg):
            lanes = slice(g * cgw, (g + 1) * cgw)
            s = jnp.dot(wm_ref[g], vn_buf[c0:c0 + chunk, lanes], preferred_element_type=F32)
            s = s + spbt_ref[:, g:g + 1]
            u = z_ref[c0:c0 + chunk, 5 * w + g * cgw:5 * w + (g + 1) * cgw].astype(F32)
            y_ref[c0:c0 + chunk, 2 * w + g * cgw:2 * w + (g + 1) * cgw] = (u * s).astype(y_ref.dtype)

    pos = i * rows + lax.broadcasted_iota(I32, (rows, 1), 0)
    for gi, win in enumerate(POOL_WINDOWS):
        lanes = slice(gi * gw, (gi + 1) * gw)
        tot = None
        for k in range(win):
            term = pd_ext[HIST - k:HIST - k + rows, lanes]
            tot = term if tot is None else tot + term
        cnt = jnp.minimum(pos + 1, win).astype(F32)
        d = tot / cnt - pd_ext[HIST:HIST + rows, lanes]
        yd = jnp.dot(d.astype(BF16), pw_ref[gi].astype(BF16), preferred_element_type=F32)
        y_ref[:, 3 * w + gi * gw:3 * w + (gi + 1) * gw] = (yd * ps_ref[:, lanes]).astype(y_ref.dtype)

    @pl.when(i == last)
    def _state():
        na_ref[...] = qa_ext[HIST + rows - (ka - 1):HIST + rows, :]
        nb_ref[...] = gb_ext[HIST + rows - (kb - 1):HIST + rows, :]
        nd_ref[...] = pd_ext[HIST + rows - nbuf_d:HIST + rows, :]

    qa_ext[0:HIST, :] = qa_ext[rows:rows + HIST, :]
    gb_ext[0:HIST, :] = gb_ext[rows:rows + HIST, :]
    pd_ext[0:HIST, :] = pd_ext[rows:rows + HIST, :]


def _mix_prompt(z, lw, *, batch, seq, rows):
    w = lw["conv_a_w"].shape[1]
    ka = lw["conv_a_w"].shape[0]
    kb = lw["conv_b_w"].shape[0]
    nbuf_d = max(POOL_WINDOWS) - 1
    cg, chunk = lw["sp_w"].shape[0], lw["sp_w"].shape[1]
    assert seq % rows == 0 and rows % chunk == 0 and kb - 1 <= HIST and nbuf_d <= HIST
    nt = seq // rows
    full2 = lambda a: pl.BlockSpec(a.shape, lambda b, i: (0, 0))
    full3 = lambda a: pl.BlockSpec(a.shape, lambda b, i: (0, 0, 0))
    vec = lambda a: a.reshape(1, w)
    small = [lw["conv_a_w"], lw["conv_b_w"], vec(lw["conv_b_b"]), vec(lw["ln_b_g"]), vec(lw["ln_b_b"]),
             vec(lw["ln_c_g"]), vec(lw["ln_c_b"])]
    spbt = lw["sp_b"].T
    in_specs = ([pl.BlockSpec((rows, 8 * w), lambda b, i: (b * nt + i, 0))]
                + [full2(a) for a in small]
                + [full3(lw["sp_w"]), full2(spbt), full3(lw["pool_w"]), full2(vec(lw["pool_scale"]))])
    out_shapes = (jax.ShapeDtypeStruct((batch * seq, 4 * w), BF16),
                  jax.ShapeDtypeStruct((batch, ka - 1, w), F32),
                  jax.ShapeDtypeStruct((batch, kb - 1, w), F32),
                  jax.ShapeDtypeStruct((batch, nbuf_d, w), F32))
    out_specs = (pl.BlockSpec((rows, 4 * w), lambda b, i: (b * nt + i, 0)),
                 pl.BlockSpec((None, ka - 1, w), lambda b, i: (b, 0, 0)),
                 pl.BlockSpec((None, kb - 1, w), lambda b, i: (b, 0, 0)),
                 pl.BlockSpec((None, nbuf_d, w), lambda b, i: (b, 0, 0)))
    scratch = [pltpu.VMEM((HIST + rows, w), F32), pltpu.VMEM((HIST + rows, w), F32),
               pltpu.VMEM((HIST + rows, w), F32), pltpu.VMEM((rows, w), BF16),
               pltpu.VMEM((cg, chunk, chunk), BF16),
               pltpu.VMEM((SUBLANES - 1, HIST + rows, w), F32)]
    return pl.pallas_call(
        functools.partial(_mix_prompt_body, rows=rows, width=w, ka=ka, kb=kb, nbuf_d=nbuf_d),
        grid=(batch, nt),
        in_specs=in_specs,
        out_specs=out_specs,
        out_shape=out_shapes,
        scratch_shapes=scratch,
        compiler_params=_cparams(("arbitrary", "arbitrary")),
        name="mix_prompt",
    )(z, *small, lw["sp_w"], spbt, lw["pool_w"], vec(lw["pool_scale"]))


def _mix_sample_body(spw_ref, spb_ref, z_ref, sa_ref, sb_ref, sd_ref, caw_ref, cbw_ref, cbb_ref,
                     lbg_ref, lbb_ref, lcg_ref, lcb_ref, pw_ref, ps_ref,
                     y_ref, na_ref, nb_ref, nd_ref, cv_ref, *, nseq, steps, width, ka, kb, nbuf_d, cg):
    w = width
    gw = w // len(POOL_WINDOWS)
    cgw = w // cg
    lane = lax.broadcasted_iota(I32, (1, w), 1)

    def col(t, k):
        return z_ref[t * nseq:(t + 1) * nseq, k * w:(k + 1) * w].astype(F32)

    def put(t, k, val):
        y_ref[t * nseq:(t + 1) * nseq, k * w:(k + 1) * w] = val.astype(y_ref.dtype)

    ext = [sa_ref[k] for k in range(ka - 1)] + [col(t, 0) * col(t, 1) for t in range(steps)]
    for t in range(steps):
        conv = None
        for j in range(ka):
            term = caw_ref[j:j + 1, :] * ext[t + j]
            conv = term if conv is None else conv + term
        put(t, 0, col(t, 2) * conv)
    for k in range(ka - 1):
        na_ref[k] = ext[steps + k]

    for t in range(steps):
        nb_ref[kb - 1 - steps + t] = col(t, 3) * jax.nn.sigmoid(col(t, 4))
    for k in range(kb - 1 - steps):
        nb_ref[k] = sb_ref[k + steps]

    def ext_b(k):
        return sb_ref[k] if k < kb - 1 else nb_ref[k - steps]

    for t in range(steps):
        acc = None
        for j in range(kb):
            term = cbw_ref[j:j + 1, :] * ext_b(t + j)
            acc = term if acc is None else acc + term
        put(t, 1, _silu(_layernorm(acc + cbb_ref[...], lbg_ref[...], lbb_ref[...])))

    def lane_groups(vals):
        out = jnp.full((1, w), vals[cg - 1], F32)
        for g in range(cg - 2, -1, -1):
            out = jnp.where(lane < (g + 1) * cgw, vals[g], out)
        return out

    for t in range(steps):
        cv_ref[t] = _layernorm(col(t, 6), lcg_ref[...], lcb_ref[...])
    for t in range(steps):
        s = lane_groups([spb_ref[g, t] for g in range(cg)])
        for sp in range(t + 1):
            s = s + lane_groups([spw_ref[g, t, sp] for g in range(cg)]) * cv_ref[sp]
        put(t, 2, col(t, 5) * s)

    for t in range(steps):
        nd_ref[nbuf_d - steps + t] = col(t, 7)
    for k in range(nbuf_d - steps):
        nd_ref[k] = sd_ref[k + steps]

    def ext_d(k):
        return sd_ref[k] if k < nbuf_d else nd_ref[k - steps]

    for t in range(steps):
        for gi, win in enumerate(POOL_WINDOWS):
            lanes = slice(gi * gw, (gi + 1) * gw)
            tot = None
            for k in range(win):
                term = ext_d(nbuf_d + t - k)[:, lanes]
                tot = term if tot is None else tot + term
            cnt = float(min(PAST_LEN + t + 1, win))
            d = tot / cnt - ext_d(nbuf_d + t)[:, lanes]
            yd = jnp.dot(d.astype(BF16), pw_ref[gi].astype(BF16), preferred_element_type=F32)
            y_ref[t * nseq:(t + 1) * nseq, 3 * w + gi * gw:3 * w + (gi + 1) * gw] = (
                yd * ps_ref[:, lanes]).astype(y_ref.dtype)


def _mix_sample(z, sa, sb, sd, lw, *, nseq, steps, m_prompt):
    w = lw["conv_a_w"].shape[1]
    ka = lw["conv_a_w"].shape[0]
    kb = lw["conv_b_w"].shape[0]
    nbuf_d = max(POOL_WINDOWS) - 1
    cg = lw["sp_w"].shape[0]
    ms = nseq * steps
    assert m_prompt % ms == 0 and steps <= nbuf_d and steps <= kb - 1
    rb = m_prompt // ms
    spw_small = lw["sp_w"][:, :steps, :steps]
    spb_small = lw["sp_b"][:, :steps]
    vec = lambda a: a.reshape(1, w)
    smem = pl.BlockSpec(memory_space=pltpu.SMEM)
    full2 = lambda a: pl.BlockSpec(a.shape, lambda i: (0, 0))
    full3 = lambda a: pl.BlockSpec(a.shape, lambda i: (0, 0, 0))
    small = [lw["conv_a_w"], lw["conv_b_w"], vec(lw["conv_b_b"]), vec(lw["ln_b_g"]), vec(lw["ln_b_b"]),
             vec(lw["ln_c_g"]), vec(lw["ln_c_b"])]
    in_specs = ([smem, smem,
                 pl.BlockSpec((ms, 8 * w), lambda i: (rb, 0)),
                 full3(sa), full3(sb), full3(sd)]
                + [full2(a) for a in small]
                + [full3(lw["pool_w"]), full2(vec(lw["pool_scale"]))])
    out_shapes = (jax.ShapeDtypeStruct((ms, 4 * w), BF16),
                  jax.ShapeDtypeStruct(sa.shape, F32), jax.ShapeDtypeStruct(sb.shape, F32),
                  jax.ShapeDtypeStruct(sd.shape, F32), jax.ShapeDtypeStruct((steps, nseq, w), F32))
    out_specs = (pl.BlockSpec((ms, 4 * w), lambda i: (0, 0)),
                 full3(sa), full3(sb), full3(sd),
                 pl.BlockSpec((steps, nseq, w), lambda i: (0, 0, 0)))
    return pl.pallas_call(
        functools.partial(_mix_sample_body, nseq=nseq, steps=steps, width=w, ka=ka, kb=kb,
                          nbuf_d=nbuf_d, cg=cg),
        grid=(1,),
        in_specs=in_specs,
        out_specs=out_specs,
        out_shape=out_shapes,
        compiler_params=_cparams(("arbitrary",)),
        name="mix_sample",
    )(spw_small, spb_small, z, sa, sb, sd, *small, lw["pool_w"], vec(lw["pool_scale"]))


def _merge_out_body(yp_ref, ys_ref, g0_ref, g1_ref, g2_ref, g3_ref, x_ref, gn_ref, wa_hbm, wb_hbm, wc_hbm, wd_hbm,
                    wo_hbm, *refs, width, prompt_tiles, layer, with_h):
    n_out = 2 if with_h else 1
    x1_ref = refs[0]
    wbr_ref, wout_ref, stage_ref, sem_ref = refs[n_out:n_out + 4]
    i = pl.program_id(0)
    rows = stage_ref.shape[1]
    d = wout_ref.shape[0]

    @pl.when(i == 0)
    def _load_weights():
        chunks = []
        for k, w_hbm in enumerate((wa_hbm, wb_hbm, wc_hbm, wd_hbm)):
            for r in range(0, width, rows):
                chunks.append((w_hbm.at[layer, pl.ds(r, rows), :], wbr_ref.at[k, pl.ds(r, rows), :]))
        for r in range(0, d, rows):
            chunks.append((wo_hbm.at[layer, pl.ds(r, rows), :], wout_ref.at[pl.ds(r, rows), :]))

        def fetch(c):
            return pltpu.make_async_copy(chunks[c][0], stage_ref.at[c % 2], sem_ref.at[c % 2])

        fetch(0).start(priority=0)
        for c in range(len(chunks)):
            if c + 1 < len(chunks):
                fetch(c + 1).start(priority=(c + 1) % 2)
            fetch(c).wait()
            chunks[c][1][...] = stage_ref[c % 2].astype(BF16)

    def run(y_ref):
        acc = None
        for k, g_ref in enumerate((g0_ref, g1_ref, g2_ref, g3_ref)):
            br = jnp.dot(y_ref[:, k * width:(k + 1) * width], wbr_ref[k], preferred_element_type=F32)
            term = jax.nn.sigmoid(g_ref[...].astype(F32)) * br
            acc = term if acc is None else acc + term
        x1 = x_ref[...] + jnp.dot(acc.astype(BF16), wout_ref[...], preferred_element_type=F32)
        x1_ref[...] = x1
        if with_h:
            refs[1][...] = _rms(x1, gn_ref[...]).astype(BF16)

    @pl.when(i < prompt_tiles)
    def _prompt():
        run(yp_ref)

    @pl.when(i >= prompt_tiles)
    def _sample():
        run(ys_ref)


def _merge_out(y_p, y_s, z, x, w_brs, w_out, layer, g, *, tm, with_h):
    mp, yw = y_p.shape
    ms = y_s.shape[0]
    assert mp % tm == 0 and ms % tm == 0
    tp = mp // tm
    w = yw // 4
    d = w_out.shape[2]
    assert (2 * yw) % d == 0 and w % STAGE_ROWS == 0 and d % STAGE_ROWS == 0
    goff = (2 * yw) // d
    gate_spec = lambda k: pl.BlockSpec((tm, d), lambda i, k=k: (i, goff + k))
    row = pl.BlockSpec((tm, d), lambda i: (i, 0))
    hbm = pl.BlockSpec(memory_space=pl.ANY)
    out_shape = [jax.ShapeDtypeStruct((mp + ms, d), F32)]
    out_specs = [row]
    if with_h:
        out_shape.append(jax.ShapeDtypeStruct((mp + ms, d), BF16))
        out_specs.append(row)
    return pl.pallas_call(
        functools.partial(_merge_out_body, width=w, prompt_tiles=tp, layer=layer, with_h=with_h),
        grid=((mp + ms) // tm,),
        in_specs=[pl.BlockSpec((tm, yw), lambda i: (jnp.minimum(i, tp - 1), 0)),
                  pl.BlockSpec((tm, yw), lambda i: (jnp.maximum(i - tp, 0), 0))]
                 + [gate_spec(k) for k in range(4)]
                 + [row, pl.BlockSpec((1, d), lambda i: (0, 0))] + [hbm] * 5,
        out_specs=out_specs,
        out_shape=out_shape,
        scratch_shapes=[pltpu.VMEM((4, w, d), BF16), pltpu.VMEM((d, d), BF16),
                        pltpu.VMEM((2, STAGE_ROWS, d), F32), pltpu.SemaphoreType.DMA((2,))],
        compiler_params=_cparams(("arbitrary",)),
        name="merge_out",
    )(y_p, y_s, z, z, z, z, x, g.reshape(1, d), *w_brs, w_out)


def _route_body(x_ref, g_ref, rwt_ref, e_ref, r_ref, wt_ref, cnt_ref, carry_ref, tri_ref):
    i = pl.program_id(0)
    ne = rwt_ref.shape[0]
    tm = x_ref.shape[0]

    @pl.when(i == 0)
    def _init():
        carry_ref[...] = jnp.zeros_like(carry_ref)
        upper = lax.broadcasted_iota(I32, (tm, tm), 0) <= lax.broadcasted_iota(I32, (tm, tm), 1)
        tri_ref[...] = jnp.where(upper, 1.0, 0.0).astype(BF16)

    h = _rms(x_ref[...], g_ref[...])
    h_hi = h.astype(BF16)
    h_lo = (h - h_hi.astype(F32)).astype(BF16)
    rw = rwt_ref[...]
    rw_hi = rw.astype(BF16)
    rw_lo = (rw - rw_hi.astype(F32)).astype(BF16)
    dn = (((1,), (1,)), ((), ()))
    dg = lambda a, b: lax.dot_general(a, b, dn, preferred_element_type=F32)
    logits = dg(rw_hi, h_hi) + dg(rw_hi, h_lo) + dg(rw_lo, h_hi)
    ex = jnp.exp(logits - jnp.max(logits, axis=0, keepdims=True))
    p = ex / jnp.sum(ex, axis=0, keepdims=True)
    eid = lax.broadcasted_iota(I32, (ne, tm), 0)
    m1 = jnp.max(p, axis=0, keepdims=True)
    i1 = jnp.min(jnp.where(p == m1, eid, ne), axis=0, keepdims=True)
    oh1 = eid == i1
    p2 = jnp.where(oh1, -1.0, p)
    m2 = jnp.max(p2, axis=0, keepdims=True)
    i2 = jnp.min(jnp.where(p2 == m2, eid, ne), axis=0, keepdims=True)
    oh2 = eid == i2
    den = m1 + m2
    sel = jnp.logical_or(oh1, oh2)
    sel_f = jnp.where(sel, 1.0, 0.0)
    cum = jnp.dot(sel_f.astype(BF16), tri_ref[...], preferred_element_type=F32) + carry_ref[...]
    rank1 = jnp.sum(jnp.where(oh1, cum, 0.0), axis=0, keepdims=True) - 1.0
    rank2 = jnp.sum(jnp.where(oh2, cum, 0.0), axis=0, keepdims=True) - 1.0
    carry_ref[...] = carry_ref[...] + jnp.sum(sel_f, axis=1, keepdims=True)
    e_ref[0:1, :] = i1
    e_ref[1:2, :] = i2
    r_ref[0:1, :] = rank1.astype(I32)
    r_ref[1:2, :] = rank2.astype(I32)
    wt_ref[0:1, :] = m1 / den
    wt_ref[1:2, :] = m2 / den

    @pl.when(i == pl.num_programs(0) - 1)
    def _counts():
        cnt_ref[...] = jnp.broadcast_to(carry_ref[...], cnt_ref.shape).astype(I32)


def _route(x1, g, router_w, *, tm):
    m, d = x1.shape
    ne = router_w.shape[1]
    tok = lambda dt: jax.ShapeDtypeStruct((TOP_K, m), dt)
    tok_spec = pl.BlockSpec((TOP_K, tm), lambda i: (0, i))
    return pl.pallas_call(
        _route_body,
        grid=(m // tm,),
        in_specs=[pl.BlockSpec((tm, d), lambda i: (i, 0)),
                  pl.BlockSpec((1, d), lambda i: (0, 0)),
                  pl.BlockSpec((ne, d), lambda i: (0, 0))],
        out_specs=(tok_spec, tok_spec, tok_spec, pl.BlockSpec((ne, LANES), lambda i: (0, 0))),
        out_shape=(tok(I32), tok(I32), tok(F32), jax.ShapeDtypeStruct((ne, LANES), I32)),
        scratch_shapes=[pltpu.VMEM((ne, 1), F32), pltpu.VMEM((tm, tm), BF16)],
        compiler_params=_cparams(("arbitrary",)),
        name="route",
    )(x1, g.reshape(1, d), router_w.T)


def _plan_body(e_ref, r_ref, cnt_ref, pos_ref, tgrp_ref, tfirst_ref, misc_ref, *, tile_shift):
    ne = cnt_ref.shape[0]
    tile = 1 << tile_shift
    sub = lax.broadcasted_iota(I32, (ne, LANES), 0)
    lane = lax.broadcasted_iota(I32, (ne, LANES), 1)

    def as_row(col):
        return jnp.sum(jnp.where(sub == lane, col, 0), axis=0, keepdims=True)

    cnt = cnt_ref[:, 0:1]
    padded = lax.shift_left(lax.shift_right_logical(cnt + (tile - 1), tile_shift), tile_shift)
    gstart = jnp.sum(jnp.where(lane < sub, as_row(padded), 0), axis=1, keepdims=True)
    gend = gstart + padded
    m = e_ref.shape[1]
    eid = lax.broadcasted_iota(I32, (ne, m), 0)
    for k in range(TOP_K):
        base = jnp.sum(jnp.where(eid == e_ref[k:k + 1, :], gstart, 0), axis=0, keepdims=True)
        pos_ref[k:k + 1, :] = base + r_ref[k:k + 1, :]
    start = lax.broadcasted_iota(I32, (1, LANES), 1) * tile
    grp = jnp.sum((gend <= start).astype(I32), axis=0, keepdims=True)
    tgrp_ref[...] = jnp.minimum(grp, ne - 1)
    tfirst_ref[...] = jnp.sum(jnp.logical_and(gstart == start, padded > 0).astype(I32), axis=0, keepdims=True)
    n_tiles = lax.shift_right_logical(jnp.max(gend, axis=0, keepdims=True), tile_shift)
    misc = jnp.where(sub == 0, n_tiles, 0)
    misc = jnp.where(sub == 1, as_row(gstart + cnt), misc)
    misc = jnp.where(sub == 2, as_row(gend), misc)
    lane_f = lane.astype(F32)
    nonempty = jnp.logical_and(as_row(padded) > 0, lane < ne)
    later = jnp.min(jnp.where(jnp.logical_and(nonempty, lane > sub), lane_f, float(ne)), axis=1, keepdims=True)
    firstg = jnp.min(jnp.where(nonempty, lane_f, float(ne)), axis=1, keepdims=True)
    nxt = jnp.where(later < ne, later, firstg).astype(I32)
    wraps = (later >= ne).astype(I32)
    mine = sub == jnp.minimum(grp, ne - 1)
    misc = jnp.where(sub == 3, jnp.sum(jnp.where(mine, nxt, 0), axis=0, keepdims=True), misc)
    misc = jnp.where(sub == 4, jnp.sum(jnp.where(mine, wraps, 0), axis=0, keepdims=True), misc)
    misc_ref[...] = misc


def _plan(e, r, cnt, *, tile):
    tile_shift = tile.bit_length() - 1
    assert 1 << tile_shift == tile
    ne = cnt.shape[0]
    m = e.shape[1]
    whole = lambda a: pl.BlockSpec(a.shape, lambda i: (0, 0))
    row = jax.ShapeDtypeStruct((1, LANES), I32)
    out_shape = (jax.ShapeDtypeStruct((TOP_K, m), I32), row, row, jax.ShapeDtypeStruct((ne, LANES), I32))
    return pl.pallas_call(
        functools.partial(_plan_body, tile_shift=tile_shift),
        grid=(1,),
        in_specs=[whole(e), whole(r), whole(cnt)],
        out_specs=tuple(pl.BlockSpec(s.shape, lambda i: (0, 0)) for s in out_shape),
        out_shape=out_shape,
        compiler_params=_cparams(("arbitrary",)),
        name="plan",
    )(e, r, cnt)


def _dispatch_body(pos0_ref, pos1_ref, nv_ref, fill_ref, end_ref, x_hbm, g_ref, o_ref, src_ref, buf_ref, sem_ref):
    j = pl.program_id(0)
    tm = o_ref.shape[0]
    n_tok = pos0_ref.shape[0]
    nv = nv_ref[0]

    def start_tile(tile, slot):
        def body(pair, carry):
            for p in range(2):
                r = pair * 2 + p
                tok = src_ref[tile * tm + r]
                pltpu.make_async_copy(x_hbm.at[pl.ds(tok, 1), :], buf_ref.at[slot, pl.ds(r, 1), :],
                                      sem_ref.at[slot]).start(priority=p)
            return carry
        lax.fori_loop(0, tm // 2, body, 0, unroll=ISSUE_UNROLL // 2)

    @pl.when(j == 0)
    def _prologue():
        for g in range(fill_ref.shape[0]):
            def fill(p, carry):
                src_ref[p] = 0
                return carry
            lax.fori_loop(fill_ref[g], end_ref[g], fill, 0)

        def inv(t, carry):
            src_ref[pos0_ref[t]] = t
            src_ref[pos1_ref[t]] = t
            return carry
        lax.fori_loop(0, n_tok, inv, 0, unroll=ISSUE_UNROLL)
        start_tile(0, 0)

    slot = lax.rem(j, 2)

    @pl.when(j < nv)
    def _work():
        @pl.when(j + 1 < nv)
        def _prefetch():
            start_tile(j + 1, 1 - slot)

        pltpu.make_async_copy(x_hbm.at[pl.ds(0, tm), :], buf_ref.at[slot], sem_ref.at[slot]).wait()
        o_ref[...] = _rms(buf_ref[slot], g_ref[...]).astype(o_ref.dtype)

    @pl.when(j >= nv)
    def _skip():
        o_ref[...] = jnp.zeros_like(o_ref)


def _dispatch(x1, g, pos0, pos1, n_tiles, fill_row, end_row, *, tm, max_tiles):
    m, d = x1.shape
    grid_spec = pltpu.PrefetchScalarGridSpec(
        num_scalar_prefetch=5,
        grid=(max_tiles,),
        in_specs=[pl.BlockSpec(memory_space=pl.ANY),
                  pl.BlockSpec((1, d), lambda j, *_: (0, 0))],
        out_specs=pl.BlockSpec((tm, d), lambda j, *_: (j, 0)),
        scratch_shapes=[pltpu.SMEM((max_tiles * tm,), I32),
                        pltpu.VMEM((2, tm, d), F32),
                        pltpu.SemaphoreType.DMA((2,))],
    )
    return pl.pallas_call(
        _dispatch_body,
        grid_spec=grid_spec,
        out_shape=jax.ShapeDtypeStruct((max_tiles * tm, d), BF16),
        compiler_params=_cparams(("arbitrary",)),
        name="dispatch",
    )(pos0, pos1, n_tiles, fill_row, end_row, x1, g.reshape(1, d))


def _combine_body(pos0_ref, pos1_ref, ys_hbm, x_ref, wt_ref, g_ref, xn_ref, h_ref, buf_ref, sem_ref):
    i = pl.program_id(0)
    n = pl.num_programs(0)
    tm = x_ref.shape[0]

    def start_tile(tile, slot):
        def body(r, carry):
            t = tile * tm + r
            pltpu.make_async_copy(ys_hbm.at[pl.ds(pos0_ref[t], 1), :], buf_ref.at[slot, 0, pl.ds(r, 1), :],
                                  sem_ref.at[slot, 0]).start(priority=0)
            pltpu.make_async_copy(ys_hbm.at[pl.ds(pos1_ref[t], 1), :], buf_ref.at[slot, 1, pl.ds(r, 1), :],
                                  sem_ref.at[slot, 1]).start(priority=1)
            return carry
        lax.fori_loop(0, tm, body, 0, unroll=ISSUE_UNROLL)

    @pl.when(i == 0)
    def _prologue():
        start_tile(0, 0)

    slot = lax.rem(i, 2)

    @pl.when(i + 1 < n)
    def _prefetch():
        start_tile(i + 1, 1 - slot)

    for k in range(TOP_K):
        pltpu.make_async_copy(ys_hbm.at[pl.ds(0, tm), :], buf_ref.at[slot, k], sem_ref.at[slot, k]).wait()
    xn = x_ref[...] + wt_ref[:, 0:1] * buf_ref[slot, 0] + wt_ref[:, 1:2] * buf_ref[slot, 1]
    xn_ref[...] = xn
    h_ref[...] = _rms(xn, g_ref[...]).astype(h_ref.dtype)


def _combine(ys, x1, wt_cols, g, pos0, pos1, *, tm, h_dtype):
    m, d = x1.shape
    row = lambda: pl.BlockSpec((tm, d), lambda i, *_: (i, 0))
    grid_spec = pltpu.PrefetchScalarGridSpec(
        num_scalar_prefetch=2,
        grid=(m // tm,),
        in_specs=[pl.BlockSpec(memory_space=pl.ANY), row(),
                  pl.BlockSpec((tm, TOP_K), lambda i, *_: (i, 0)),
                  pl.BlockSpec((1, d), lambda i, *_: (0, 0))],
        out_specs=(row(), row()),
        scratch_shapes=[pltpu.VMEM((2, TOP_K, tm, d), F32), pltpu.SemaphoreType.DMA((2, TOP_K))],
    )
    return pl.pallas_call(
        _combine_body,
        grid_spec=grid_spec,
        out_shape=(jax.ShapeDtypeStruct((m, d), F32), jax.ShapeDtypeStruct((m, d), h_dtype)),
        compiler_params=_cparams(("arbitrary",)),
        name="combine",
    )(pos0, pos1, ys, x1, wt_cols, g.reshape(1, d))


def _pick(total, prefs):
    for p in prefs:
        if total % p == 0:
            return p
    raise ValueError(f"no tile for {total} in {prefs}")


def _row_tile(m, cap):
    best = None
    for t in range(16, cap + 1, 16):
        if m % t == 0:
            best = t
    assert best is not None
    return best


def kernel(x_prompt, x_sample, state_conv_a, state_conv_b, state_pool, norm_mix_g, w_in, conv_a_w, w_br_a,
           conv_b_w, conv_b_b, ln_b_g, ln_b_b, w_br_b, ln_c_g, ln_c_b, sp_w, sp_b, w_br_c, pool_w,
           pool_scale, w_br_d, w_out, norm_ffn_g, ffn_w1, ffn_w3, ffn_w2, router_w, moe_w1, moe_w3, moe_w2,
           final_norm_g):
    batch, seq, d = x_prompt.shape
    nseq, steps, _ = x_sample.shape
    depth = w_in.shape[0]
    n_exp = router_w.shape[2]
    mp, ms = batch * seq, nseq * steps
    m = mp + ms

    x = jnp.concatenate([x_prompt.reshape(mp, d), jnp.swapaxes(x_sample, 0, 1).reshape(ms, d)], axis=0)
    st_a = jnp.swapaxes(state_conv_a, 1, 2)
    st_b = jnp.swapaxes(state_conv_b, 1, 2)
    st_d = jnp.swapaxes(state_pool, 1, 2)

    tm_big = _row_tile(m, 1088)
    tm_mid = _row_tile(m, 544)
    tm_small = _row_tile(m, 256)
    tm_merge = min(_row_tile(mp, 256), _row_tile(ms, 256))
    assert mp % tm_merge == 0 and ms % tm_merge == 0
    tm_moe = 512
    max_tiles = -(-(m * TOP_K) // tm_moe) + n_exp
    assert max_tiles <= LANES
    mix_rows = _pick(seq, (256, 128))
    moe_w1f = moe_w1.reshape((-1,) + moe_w1.shape[2:])
    moe_w3f = moe_w3.reshape((-1,) + moe_w3.shape[2:])
    moe_w2f = moe_w2.reshape((-1,) + moe_w2.shape[2:])
    dffe = moe_w1.shape[3]
    dff = ffn_w1.shape[2]

    h = _rmsnorm(x, norm_mix_g[0], BF16, tm_mid)
    outs_p = ([], [], [])
    outs_s = ([], [], [], [])
    y_final = None
    for i in range(depth):
        lw = dict(conv_a_w=conv_a_w[i], conv_b_w=conv_b_w[i], conv_b_b=conv_b_b[i], ln_b_g=ln_b_g[i],
                  ln_b_b=ln_b_b[i], ln_c_g=ln_c_g[i], ln_c_b=ln_c_b[i], sp_w=sp_w[i], sp_b=sp_b[i],
                  pool_w=pool_w[i], pool_scale=pool_scale[i])
        z = _dense_stream_matmul(h, w_in, i, tm=_row_tile(m, 2176), tn=_pick(w_in.shape[2], (1024, 512, 256, 128)),
                                 mode="plain", out_dtype=BF16, name="inproj")
        y_p, na_p, nb_p, nd_p = _mix_prompt(z, lw, batch=batch, seq=seq, rows=mix_rows)
        y_s, na_s, nb_s, nd_s, cv_s = _mix_sample(z, st_a[i], st_b[i], st_d[i], lw, nseq=nseq, steps=steps,
                                                  m_prompt=mp)
        for acc, val in zip(outs_p, (na_p, nb_p, nd_p)):
            acc.append(val)
        for acc, val in zip(outs_s, (na_s, nb_s, nd_s, cv_s)):
            acc.append(val)
        merge_out = functools.partial(_merge_out, y_p, y_s, z, x, (w_br_a, w_br_b, w_br_c, w_br_d), w_out, i,
                                      norm_ffn_g[i], tm=tm_merge)
        last = i == depth - 1
        g_next = final_norm_g if last else norm_mix_g[i + 1]
        h_dtype = F32 if last else BF16
        j = i // 2
        if i % 2 == 0:
            x1, h2 = merge_out(with_h=True)
            a = _dense_stream_matmul(h2, ffn_w1, j, w2=ffn_w3, tm=tm_big, tn=_pick(dff, (512, 256, 128)),
                                     mode="gated", out_dtype=BF16, name="ffn_up")
            x = _dense_stream_matmul(a, ffn_w2, j, tm=tm_big, tn=_pick(d, (512, 256, 128)), mode="resid",
                                     out_dtype=F32, res=x1, name="ffn_down")
            h = _rmsnorm(x, g_next, h_dtype, tm_mid)
        else:
            (x1,) = merge_out(with_h=False)
            e, r, wt, cnt = _route(x1, norm_ffn_g[i], router_w[j], tm=_pick(m, (512, 256, 128)))
            pos, tgrp, tfirst, misc = _plan(e, r, cnt, tile=tm_moe)
            n_tiles = misc[0, 0:1]
            hs = _dispatch(x1, norm_ffn_g[i], pos[0], pos[1], n_tiles, misc[1, :n_exp], misc[2, :n_exp],
                           tm=tm_moe, max_tiles=max_tiles)
            grp = tgrp[0, :max_tiles] + j * n_exp
            first = tfirst[0, :max_tiles]
            nextg = misc[3, :max_tiles] + j * n_exp
            wrap = misc[4, :max_tiles]
            a = _stream_matmul(hs, [moe_w1f, moe_w3f], grp, first, nextg, wrap, n_tiles, tm=tm_moe,
                               tn=_pick(dffe, (1408, 1024, 512, 256, 128)), mode="gated", out_dtype=BF16,
                               name="moe_up", step_rows=_stream_step_rows(d, MOE_STREAM_STEPS))
            ys = _stream_matmul(a, [moe_w2f], grp, first, nextg, wrap, n_tiles, tm=tm_moe,
                                tn=_pick(d, (1024, 512, 256, 128)), mode="plain", out_dtype=F32, name="moe_down",
                                step_rows=_stream_step_rows(dffe, MOE_STREAM_STEPS))
            x, h = _combine(ys, x1, wt.T, g_next, pos[0], pos[1], tm=tm_small, h_dtype=h_dtype)
        if last:
            y_final = h

    y_prompt = y_final[:mp].reshape(batch, seq, d)
    y_sample = jnp.swapaxes(y_final[mp:].reshape(steps, nseq, d), 0, 1)
    stack_s = lambda vals: jnp.swapaxes(jnp.stack(vals), 1, 2)
    return (y_prompt, y_sample, jnp.stack(outs_p[0]), jnp.stack(outs_p[1]), jnp.stack(outs_p[2]),
            stack_s(outs_s[0]), stack_s(outs_s[1]), stack_s(outs_s[2]), stack_s(outs_s[3]))
```
